```python
import jax, jax.numpy as jnp
from jax import lax
import numpy as np

D_MODEL = 1024
BATCH = 1
SEQ = 16384
DEPTH = 2
DEC_BATCH = 8
DEC_SEQ = 64
PAST_LEN = 1024

CHUNK = 64
CONV_CH = 512
CONV_WIDTH = 31
N_HEADS = 8
HEAD_DIM = 64
ATTN_W = N_HEADS * HEAD_DIM
MIX_W = CONV_CH + ATTN_W
IN_COLS = 2 * CONV_CH + 3 * ATTN_W
D_FF = 4 * D_MODEL
Q_BLOCK = 128
EPS = 1e-6

kernel_name = "hymba_conformer_stickbreaking_stream_step"


def _rmsnorm(x, g):
    xf = x.astype(jnp.float32)
    y = xf * lax.rsqrt(jnp.mean(xf * xf, axis=-1, keepdims=True) + EPS)
    return (y * g.astype(jnp.float32)).astype(x.dtype)


def _layernorm(x, g, b):
    xf = x.astype(jnp.float32)
    mu = jnp.mean(xf, axis=-1, keepdims=True)
    xc = xf - mu
    y = xc * lax.rsqrt(jnp.mean(xc * xc, axis=-1, keepdims=True) + EPS)
    return (y * g.astype(jnp.float32) + b.astype(jnp.float32)).astype(x.dtype)


def _causal_dwconv(u_ext, w):
    c = w.shape[-1]
    return lax.conv_general_dilated(
        u_ext, w.reshape(CONV_WIDTH, 1, c).astype(u_ext.dtype),
        window_strides=(1,), padding='VALID',
        dimension_numbers=('NWC', 'WIO', 'NWC'), feature_group_count=c)


def _sb_block(qb, qpos, k, v, kpos):
    scale = HEAD_DIM ** -0.5
    z = jnp.einsum('bqhd,bkhd->bhqk', qb.astype(jnp.float32), k.astype(jnp.float32)) * scale
    mask = kpos[None, :] < qpos[:, None]
    log_1m_beta = jnp.where(mask, jax.nn.log_sigmoid(-z), 0.0)
    between = lax.cumsum(log_1m_beta, axis=3, reverse=True) - log_1m_beta
    a = jnp.where(mask, jnp.exp(jax.nn.log_sigmoid(z) + between), 0.0)
    return jnp.einsum('bhqk,bkhd->bqhd', a, v.astype(jnp.float32)).astype(v.dtype)


def _stick_breaking(q, k, v, q_pos, k_pos):
    b, t, h, d = q.shape
    blk = Q_BLOCK if t % Q_BLOCK == 0 else t
    nblk = t // blk
    qb = q.reshape(b, nblk, blk, h, d).transpose(1, 0, 2, 3, 4)
    pb = q_pos.reshape(nblk, blk)
    out = lax.map(lambda args: _sb_block(args[0], args[1], k, v, k_pos), (qb, pb))
    return out.transpose(1, 0, 2, 3, 4).reshape(b, t, h, d)


def _layer(x, past_conv, past_k, past_v, g_mix, w_in, w_dw, b_dw, g_cn, b_cn, w_pw,
           g_q, g_k, w_out, g_ff, w_ff1, w_ff2):
    b, t, _ = x.shape
    n_past = past_k.shape[1]
    h = _rmsnorm(x, g_mix)
    proj = h @ w_in
    a, gate, q, k, v = jnp.split(
        proj, [CONV_CH, 2 * CONV_CH, 2 * CONV_CH + ATTN_W, 2 * CONV_CH + 2 * ATTN_W], axis=-1)
    u = a * jax.nn.sigmoid(gate)
    u_ext = jnp.concatenate([past_conv.astype(u.dtype), u], axis=1)
    c = _causal_dwconv(u_ext, w_dw) + b_dw
    c = jax.nn.silu(_layernorm(c, g_cn, b_cn)) @ w_pw
    new_conv = u_ext[:, u_ext.shape[1] - (CONV_WIDTH - 1):]
    q = _rmsnorm(q.reshape(b, t, N_HEADS, HEAD_DIM), g_q)
    k = _rmsnorm(k.reshape(b, t, N_HEADS, HEAD_DIM), g_k)
    v = v.reshape(b, t, N_HEADS, HEAD_DIM)
    k_all = jnp.concatenate([past_k.astype(k.dtype), k], axis=1)
    v_all = jnp.concatenate([past_v.astype(v.dtype), v], axis=1)
    k_pos = jnp.arange(n_past + t, dtype=jnp.int32)
    q_pos = n_past + jnp.arange(t, dtype=jnp.int32)
    o = _stick_breaking(q, k_all, v_all, q_pos, k_pos)
    x = x + jnp.concatenate([c, o.reshape(b, t, ATTN_W)], axis=-1) @ w_out
    f = jax.nn.relu(_rmsnorm(x, g_ff) @ w_ff1)
    x = x + (f * f) @ w_ff2
    return x, k, v, new_conv


def setup_inputs(seed: int = 0) -> dict:
    key = jax.random.key(seed)
    ks = jax.random.split(key, 20)
    f32 = jnp.float32

    def nrm(k, shape, scale):
        return jax.random.normal(k, shape, f32) * scale

    return {
        "x_prompt": nrm(ks[0], (BATCH, SEQ, D_MODEL), 1.0),
        "x_sample": nrm(ks[1], (DEC_BATCH, DEC_SEQ, D_MODEL), 1.0),
        "cache_k": nrm(ks[2], (DEPTH, DEC_BATCH, PAST_LEN, N_HEADS, HEAD_DIM), 1.0),
        "cache_v": nrm(ks[3], (DEPTH, DEC_BATCH, PAST_LEN, N_HEADS, HEAD_DIM), 1.0),
        "state_conv": nrm(ks[4], (DEPTH, DEC_BATCH, CONV_WIDTH - 1, CONV_CH), 0.5),
        "g_mix": 1.0 + nrm(ks[5], (DEPTH, D_MODEL), 0.02),
        "w_in": nrm(ks[6], (DEPTH, D_MODEL, IN_COLS), D_MODEL ** -0.5),
        "w_dw": nrm(ks[7], (DEPTH, CONV_WIDTH, CONV_CH), CONV_WIDTH ** -0.5),
        "b_dw": nrm(ks[8], (DEPTH, CONV_CH), 0.02),
        "g_cn": 1.0 + nrm(ks[9], (DEPTH, CONV_CH), 0.02),
        "b_cn": nrm(ks[10], (DEPTH, CONV_CH), 0.02),
        "w_pw": nrm(ks[11], (DEPTH, CONV_CH, CONV_CH), CONV_CH ** -0.5),
        "g_q": 1.0 + nrm(ks[12], (DEPTH, HEAD_DIM), 0.02),
        "g_k": 1.0 + nrm(ks[13], (DEPTH, HEAD_DIM), 0.02),
        "w_out": nrm(ks[14], (DEPTH, MIX_W, D_MODEL), MIX_W ** -0.5),
        "g_ff": 1.0 + nrm(ks[15], (DEPTH, D_MODEL), 0.02),
        "w_ff1": nrm(ks[16], (DEPTH, D_MODEL, D_FF), D_MODEL ** -0.5),
        "w_ff2": nrm(ks[17], (DEPTH, D_FF, D_MODEL), D_FF ** -0.5),
    }


def reference(x_prompt, x_sample, cache_k, cache_v, state_conv, g_mix, w_in, w_dw, b_dw,
              g_cn, b_cn, w_pw, g_q, g_k, w_out, g_ff, w_ff1, w_ff2):
    assert x_sample.shape[1] <= CHUNK
    bp = x_prompt.shape[0]
    hp = x_prompt
    hs = x_sample
    kp, vp, cp, ksl, vsl, csl = [], [], [], [], [], []
    empty_kv = jnp.zeros((bp, 0, N_HEADS, HEAD_DIM), x_prompt.dtype)
    zero_conv = jnp.zeros((bp, CONV_WIDTH - 1, CONV_CH), x_prompt.dtype)
    for l in range(DEPTH):
        w = (g_mix[l], w_in[l], w_dw[l], b_dw[l], g_cn[l], b_cn[l], w_pw[l],
             g_q[l], g_k[l], w_out[l], g_ff[l], w_ff1[l], w_ff2[l])
        hp, k_new, v_new, conv_new = _layer(hp, zero_conv, empty_kv, empty_kv, *w)
        kp.append(k_new); vp.append(v_new); cp.append(conv_new)
        hs, k_new, v_new, conv_new = _layer(hs, state_conv[l], cache_k[l], cache_v[l], *w)
        ksl.append(k_new); vsl.append(v_new); csl.append(conv_new)
    return (hp, hs, jnp.stack(kp), jnp.stack(vp), jnp.stack(cp),
            jnp.stack(ksl), jnp.stack(vsl), jnp.stack(csl))
```

```python
import functools

import jax
import jax.numpy as jnp
from jax import lax
from jax.experimental import pallas as pl
from jax.experimental.pallas import tpu as pltpu

EPS = 1e-6
CONV_WIDTH = 31
HIST = CONV_WIDTH - 1
HIST_PAD = 32
N_HEADS = 8
HEAD_DIM = 64
ATTN_W = N_HEADS * HEAD_DIM
CONV_CH = 512
LANES = 128
KEY_CHUNK = 128
CONV_ROWS = 64
FF_CHUNK = 1024
ROW_BLOCK = 512
VMEM_LIMIT_BYTES = 56 * 1024 * 1024
STICK_LOG_FLOOR = -104.0

_F32 = jnp.float32
_BF16 = jnp.bfloat16


def _dot(a, b):
    return jnp.dot(a, b, preferred_element_type=_F32)


def _dot_nt(a, b):
    return lax.dot_general(a, b, (((1,), (1,)), ((), ())), preferred_element_type=_F32)


def _dot_split3(x, m):
    hi = x.astype(_BF16)
    r1 = x - hi.astype(_F32)
    mid = r1.astype(_BF16)
    lo = (r1 - mid.astype(_F32)).astype(_BF16)
    return _dot(hi, m) + _dot(mid, m) + _dot(lo, m)


def _dot_split2(x, m):
    hi = x.astype(_BF16)
    lo = (x - hi.astype(_F32)).astype(_BF16)
    return _dot(hi, m) + _dot(lo, m)


def _inproj_kernel(x_ref, hist_ref, gmix_ref, win_ref, wdw_ref, bdw_ref, gcn_ref, bcn_ref, wpw_ref,
                   gq_ref, gk_ref, gmat_ref,
                   c_ref, q_ref, k_ref, v_ref, nconv_ref,
                   uext_ref, act_ref, *, n_seg, seg_rows):
    t = pl.program_id(1)
    n_t = pl.num_programs(1)
    rows = n_seg * seg_rows
    d_model = x_ref.shape[-1]

    x = x_ref[...].reshape(rows, d_model)
    h = x * lax.rsqrt(jnp.mean(x * x, axis=-1, keepdims=True) + EPS) * gmix_ref[...]
    h = h.astype(_BF16)

    a = _dot(h, win_ref[:, 0:CONV_CH])
    gate = _dot(h, win_ref[:, CONV_CH:2 * CONV_CH])
    u = a * jax.nn.sigmoid(gate)

    @pl.when(t == 0)
    def _():
        uext_ref[:, 0:HIST_PAD, :] = hist_ref[...]

    uext_ref[:, HIST_PAD:HIST_PAD + seg_rows, :] = u.reshape(n_seg, seg_rows, CONV_CH)

    for s in range(n_seg):
        for r in range(seg_rows // CONV_ROWS):
            r0 = r * CONV_ROWS
            acc = jnp.zeros((CONV_ROWS, CONV_CH), _F32)
            for j in range(CONV_WIDTH):
                off = r0 + j + (HIST_PAD - HIST)
                acc = acc + wdw_ref[j:j + 1, :] * uext_ref[s, off:off + CONV_ROWS, :]
            cv = acc + bdw_ref[...]
            mu = jnp.mean(cv, axis=-1, keepdims=True)
            xc = cv - mu
            y = xc * lax.rsqrt(jnp.mean(xc * xc, axis=-1, keepdims=True) + EPS)
            y = y * gcn_ref[...] + bcn_ref[...]
            y = y * jax.nn.sigmoid(y)
            act_ref[s * seg_rows + r0:s * seg_rows + r0 + CONV_ROWS, :] = y.astype(_BF16)

    c_ref[...] = _dot(act_ref[...], wpw_ref[...]).astype(_BF16).reshape(c_ref.shape)

    @pl.when(t == n_t - 1)
    def _():
        nconv_ref[...] = uext_ref[:, seg_rows + (HIST_PAD - HIST):seg_rows + HIST_PAD, :]

    uext_ref[:, 0:HIST_PAD, :] = uext_ref[:, seg_rows:seg_rows + HIST_PAD, :]

    def head_norm(z, g_ref):
        ssq = _dot_split2(z * z, gmat_ref[...])
        return z * lax.rsqrt(ssq * (1.0 / HEAD_DIM) + EPS) * g_ref[...]

    q = _dot(h, win_ref[:, 2 * CONV_CH:2 * CONV_CH + ATTN_W])
    q_ref[...] = head_norm(q, gq_ref).reshape(q_ref.shape)
    k = _dot(h, win_ref[:, 2 * CONV_CH + ATTN_W:2 * CONV_CH + 2 * ATTN_W])
    k_ref[...] = head_norm(k, gk_ref).reshape(k_ref.shape)
    v_ref[...] = _dot(h, win_ref[:, 2 * CONV_CH + 2 * ATTN_W:]).reshape(v_ref.shape)


def _inproj(x, hist, gmix, win, wdw, bdw, gcn, bcn, wpw, gq, gk, gmat, *, n_seg, seg_rows):
    batch, t_len, d_model = x.shape
    assert batch % n_seg == 0 and t_len % seg_rows == 0 and seg_rows % CONV_ROWS == 0
    n_t = t_len // seg_rows
    assert n_seg == 1 or n_t == 1
    grid = (batch // n_seg, n_t)
    rows_spec = lambda w: pl.BlockSpec((n_seg, seg_rows, w), lambda b, t: (b, t, 0))
    const = lambda arr: pl.BlockSpec(arr.shape, lambda b, t: (0,) * arr.ndim)
    out_shape = (
        jax.ShapeDtypeStruct((batch, t_len, CONV_CH), _BF16),
        jax.ShapeDtypeStruct((batch, t_len, ATTN_W), _F32),
        jax.ShapeDtypeStruct((batch, t_len, ATTN_W), _F32),
        jax.ShapeDtypeStruct((batch, t_len, ATTN_W), _F32),
        jax.ShapeDtypeStruct((batch, HIST, CONV_CH), _F32),
    )
    return pl.pallas_call(
        functools.partial(_inproj_kernel, n_seg=n_seg, seg_rows=seg_rows),
        out_shape=out_shape,
        grid=grid,
        in_specs=[rows_spec(d_model),
                  pl.BlockSpec((n_seg, HIST_PAD, CONV_CH), lambda b, t: (b, 0, 0)),
                  const(gmix), const(win), const(wdw), const(bdw), const(gcn), const(bcn), const(wpw),
                  const(gq), const(gk), const(gmat)],
        out_specs=(rows_spec(CONV_CH), rows_spec(ATTN_W), rows_spec(ATTN_W), rows_spec(ATTN_W),
                   pl.BlockSpec((n_seg, HIST, CONV_CH), lambda b, t: (b, 0, 0))),
        scratch_shapes=[pltpu.VMEM((n_seg, seg_rows + HIST_PAD, CONV_CH), _F32),
                        pltpu.VMEM((n_seg * seg_rows, CONV_CH), _BF16)],
        compiler_params=pltpu.CompilerParams(dimension_semantics=("arbitrary", "arbitrary"),
                                             vmem_limit_bytes=VMEM_LIMIT_BYTES),
        name="inproj_conv",
    )(x, hist, gmix, win, wdw, bdw, gcn, bcn, wpw, gq, gk, gmat)


def _attn_kernel(q_ref, kd_ref, vd_ref, kp1_ref, vp1_ref, kp2_ref, vp2_ref, kany_ref, vany_ref,
                 ud_ref, up_ref, ones_ref,
                 o_ref,
                 oacc_ref, run_ref, kbuf_ref, vbuf_ref, sem_ref, *, past_chunks0, chunks_per_q):
    b = pl.program_id(0)
    i = pl.program_id(1)
    tq = q_ref.shape[0]
    n_pairs = ATTN_W // LANES
    n_past = past_chunks0 + i * chunks_per_q

    lane = lax.broadcasted_iota(jnp.int32, (tq, LANES), 1)
    even_head = lane < HEAD_DIM
    row_id = lax.broadcasted_iota(jnp.int32, (tq, tq), 0)
    col_id = lax.broadcasted_iota(jnp.int32, (tq, tq), 1)
    causal = col_id < row_id

    def head_chunk(qh, kc, vc, run, mask, u_mat):
        z = _dot_nt(qh, kc)
        sp = jnp.log1p(jnp.exp(-jnp.abs(z)))
        log_beta = jnp.minimum(z, 0.0) - sp
        log_1m = -jnp.maximum(z, 0.0) - sp
        if mask is not None:
            log_1m = jnp.where(mask, log_1m, 0.0)
        between = _dot_split3(log_1m, u_mat)
        if run is not None:
            between = between + run
        a = jnp.exp(log_beta + between)
        if mask is not None:
            a = jnp.where(mask, a, 0.0)
        total = _dot_split3(log_1m, ones_ref[0:log_1m.shape[1], :])
        new_run = total if run is None else run + total
        return _dot(a.astype(_BF16), vc), new_run

    def pair_chunk(p, kc_ref, vc_ref, runs, mask, u_mat):
        sl = slice(p * LANES, (p + 1) * LANES)
        qp = q_ref[:, sl] * (HEAD_DIM ** -0.5)
        qe = jnp.where(even_head, qp, 0.0).astype(_BF16)
        qo = jnp.where(even_head, 0.0, qp).astype(_BF16)
        kc = kc_ref[:, sl].astype(_BF16)
        vc = vc_ref[:, sl].astype(_BF16)
        pe, run_e = head_chunk(qe, kc, vc, runs[0], mask, u_mat)
        po, run_o = head_chunk(qo, kc, vc, runs[1], mask, u_mat)
        return jnp.where(even_head, pe, po), (run_e, run_o)

    key_id = lax.broadcasted_iota(jnp.int32, (tq, KEY_CHUNK), 1)
    mask1 = key_id < jnp.where(n_past >= 1, KEY_CHUNK, 0)
    mask2 = key_id < jnp.where(n_past >= 2, KEY_CHUNK, 0)
    for p in range(n_pairs):
        o0, runs = pair_chunk(p, kd_ref, vd_ref, (None, None), causal, ud_ref[...])
        o1, runs = pair_chunk(p, kp1_ref, vp1_ref, runs, mask1, up_ref[...])
        o2, runs = pair_chunk(p, kp2_ref, vp2_ref, runs, mask2, up_ref[...])
        oacc_ref[:, p * LANES:(p + 1) * LANES] = o0 + o1 + o2
        run_ref[2 * p] = runs[0]
        run_ref[2 * p + 1] = runs[1]

    def stick_max():
        m = run_ref[0]
        for hh in range(1, N_HEADS):
            m = jnp.maximum(m, run_ref[hh])
        return jnp.max(m)

    def more(carry):
        c, m = carry
        return jnp.logical_and(c >= 0, m > STICK_LOG_FLOOR)

    def older(carry):
        c, _ = carry
        start = pl.multiple_of(c * KEY_CHUNK, KEY_CHUNK)
        k_copy = pltpu.make_async_copy(kany_ref.at[b, pl.ds(start, KEY_CHUNK), :], kbuf_ref, sem_ref.at[0])
        v_copy = pltpu.make_async_copy(vany_ref.at[b, pl.ds(start, KEY_CHUNK), :], vbuf_ref, sem_ref.at[1])
        k_copy.start()
        v_copy.start()
        k_copy.wait()
        v_copy.wait()
        for p in range(n_pairs):
            oc, runs = pair_chunk(p, kbuf_ref, vbuf_ref, (run_ref[2 * p], run_ref[2 * p + 1]), None,
                                  up_ref[...])
            oacc_ref[:, p * LANES:(p + 1) * LANES] += oc
            run_ref[2 * p] = runs[0]
            run_ref[2 * p + 1] = runs[1]
        return c - 1, stick_max()

    lax.while_loop(more, older, (n_past - 3, stick_max()))

    o_ref[...] = oacc_ref[...].astype(o_ref.dtype)


def _attention(q, k_new, v_new, k_past, v_past, *, past_is_self):
    batch, t_len, _ = q.shape
    if past_is_self:
        tq = KEY_CHUNK
        past_chunks0, chunks_per_q = 0, 1
    else:
        tq = t_len
        assert k_past.shape[1] % KEY_CHUNK == 0
        past_chunks0, chunks_per_q = k_past.shape[1] // KEY_CHUNK, 0
    assert t_len % tq == 0 and tq <= KEY_CHUNK
    n_q = t_len // tq
    assert past_is_self or n_q == 1

    def newer(n):
        return (lax.broadcasted_iota(jnp.int32, (n, n), 0) >
                lax.broadcasted_iota(jnp.int32, (n, n), 1)).astype(_BF16)
    u_diag, u_past = newer(tq), newer(KEY_CHUNK)
    ones = jnp.ones((KEY_CHUNK, LANES), _BF16)

    def past_map(back):
        return lambda b, i: (b, jnp.maximum(past_chunks0 + i * chunks_per_q - back, 0), 0)

    blk = lambda rows, imap: pl.BlockSpec((None, rows, ATTN_W), imap)
    own = lambda b, i: (b, i, 0)
    const = lambda arr: pl.BlockSpec(arr.shape, lambda b, i: (0,) * arr.ndim)
    any_spec = pl.BlockSpec(memory_space=pl.ANY)
    return pl.pallas_call(
        functools.partial(_attn_kernel, past_chunks0=past_chunks0, chunks_per_q=chunks_per_q),
        out_shape=jax.ShapeDtypeStruct((batch, t_len, ATTN_W), _BF16),
        grid=(batch, n_q),
        in_specs=[blk(tq, own), blk(tq, own), blk(tq, own),
                  blk(KEY_CHUNK, past_map(1)), blk(KEY_CHUNK, past_map(1)),
                  blk(KEY_CHUNK, past_map(2)), blk(KEY_CHUNK, past_map(2)),
                  any_spec, any_spec, const(u_diag), const(u_past), const(ones)],
        out_specs=blk(tq, own),
        scratch_shapes=[pltpu.VMEM((tq, ATTN_W), _F32),
                        pltpu.VMEM((N_HEADS, tq, LANES), _F32),
                        pltpu.VMEM((KEY_CHUNK, ATTN_W), _F32),
                        pltpu.VMEM((KEY_CHUNK, ATTN_W), _F32),
                        pltpu.SemaphoreType.DMA((2,))],
        compiler_params=pltpu.CompilerParams(dimension_semantics=("arbitrary", "arbitrary"),
                                             vmem_limit_bytes=VMEM_LIMIT_BYTES),
        name="stick_attention",
    )(q, k_new, v_new, k_past, v_past, k_past, v_past, k_past, v_past, u_diag, u_past, ones)


def _outffn_kernel(x_ref, c_ref, o_ref, wout_ref, gff_ref, w1_ref, w2_ref, y_ref):
    d_ff = w1_ref.shape[1]
    x1 = x_ref[...] + _dot(c_ref[...], wout_ref[0:CONV_CH, :]) + _dot(o_ref[...], wout_ref[CONV_CH:, :])
    h = x1 * lax.rsqrt(jnp.mean(x1 * x1, axis=-1, keepdims=True) + EPS) * gff_ref[...]
    h = h.astype(_BF16)
    ffn = None
    for cc in range(d_ff // FF_CHUNK):
        f = jnp.maximum(_dot(h, w1_ref[:, cc * FF_CHUNK:(cc + 1) * FF_CHUNK]), 0.0)
        part = _dot((f * f).astype(_BF16), w2_ref[cc * FF_CHUNK:(cc + 1) * FF_CHUNK, :])
        ffn = part if ffn is None else ffn + part
    y_ref[...] = x1 + ffn


def _outffn(x, c, o, wout, gff, w1, w2):
    rows, d_model = x.shape
    tb = min(ROW_BLOCK, rows)
    assert rows % tb == 0 and w1.shape[1] % FF_CHUNK == 0
    rows_spec = lambda w: pl.BlockSpec((tb, w), lambda r: (r, 0))
    const = lambda arr: pl.BlockSpec(arr.shape, lambda r: (0,) * arr.ndim, pipeline_mode=pl.Buffered(1))
    return pl.pallas_call(
        _outffn_kernel,
        out_shape=jax.ShapeDtypeStruct((rows, d_model), _F32),
        grid=(rows // tb,),
        in_specs=[rows_spec(d_model), rows_spec(CONV_CH), rows_spec(ATTN_W),
                  const(wout), const(gff), const(w1), const(w2)],
        out_specs=rows_spec(d_model),
        compiler_params=pltpu.CompilerParams(dimension_semantics=("arbitrary",),
                                             vmem_limit_bytes=VMEM_LIMIT_BYTES),
        name="outproj_ffn",
    )(x, c, o, wout, gff, w1, w2)


def _layer(x, hist, k_past, v_past, w, *, n_seg, seg_rows):
    batch, t_len, d_model = x.shape
    c, q, k, v, new_conv = _inproj(x, hist, w["gmix"], w["win"], w["wdw"], w["bdw"], w["gcn"], w["bcn"],
                                   w["wpw"], w["gq"], w["gk"], w["gmat"], n_seg=n_seg, seg_rows=seg_rows)
    if k_past is None:
        o = _attention(q, k, v, k, v, past_is_self=True)
    else:
        o = _attention(q, k, v, k_past, v_past, past_is_self=False)
    rows = batch * t_len
    y = _outffn(x.reshape(rows, d_model), c.reshape(rows, CONV_CH), o.reshape(rows, ATTN_W),
                w["wout"], w["gff"], w["w1"], w["w2"])
    return y.reshape(batch, t_len, d_model), k, v, new_conv


def kernel(x_prompt, x_sample, cache_k, cache_v, state_conv, g_mix, w_in, w_dw, b_dw, g_cn, b_cn, w_pw,
           g_q, g_k, w_out, g_ff, w_ff1, w_ff2):
    depth = w_in.shape[0]
    bp, tp, _ = x_prompt.shape
    bs, ts, _ = x_sample.shape
    n_past = cache_k.shape[2]
    head_id = lax.broadcasted_iota(jnp.int32, (ATTN_W, ATTN_W), 0) // HEAD_DIM
    gmat = (head_id == head_id.T).astype(_BF16)
    zero_hist = jnp.zeros((bp, HIST_PAD, CONV_CH), _F32)
    row = lambda a: a.reshape(1, -1)

    hp, hs = x_prompt, x_sample
    outs = [[] for _ in range(6)]
    for l in range(depth):
        w = dict(gmix=row(g_mix[l]), win=w_in[l].astype(_BF16), wdw=w_dw[l], bdw=row(b_dw[l]),
                 gcn=row(g_cn[l]), bcn=row(b_cn[l]), wpw=w_pw[l].astype(_BF16),
                 gq=row(jnp.tile(g_q[l], N_HEADS)), gk=row(jnp.tile(g_k[l], N_HEADS)), gmat=gmat,
                 wout=w_out[l].astype(_BF16), gff=row(g_ff[l]),
                 w1=w_ff1[l].astype(_BF16), w2=w_ff2[l].astype(_BF16))
        hp, k, v, cv = _layer(hp, zero_hist, None, None, w, n_seg=1, seg_rows=min(ROW_BLOCK, tp))
        outs[0].append(k.reshape(bp, tp, N_HEADS, HEAD_DIM))
        outs[1].append(v.reshape(bp, tp, N_HEADS, HEAD_DIM))
        outs[2].append(cv)
        hist = jnp.pad(state_conv[l], ((0, 0), (HIST_PAD - HIST, 0), (0, 0)))
        hs, k, v, cv = _layer(hs, hist, cache_k[l].reshape(bs, n_past, ATTN_W),
                              cache_v[l].reshape(bs, n_past, ATTN_W), w, n_seg=bs, seg_rows=ts)
        outs[3].append(k.reshape(bs, ts, N_HEADS, HEAD_DIM))
        outs[4].append(v.reshape(bs, ts, N_HEADS, HEAD_DIM))
        outs[5].append(cv)
    return (hp, hs) + tuple(jnp.stack(o) for o in outs)
```

```python
import functools

import jax
import jax.numpy as jnp
from jax import lax
from jax.experimental import pallas as pl
from jax.experimental.pallas import tpu as pltpu

EPS = 1e-6
CONV_WIDTH = 31
HIST = CONV_WIDTH - 1
HIST_PAD = 32
N_HEADS = 8
HEAD_DIM = 64
ATTN_W = N_HEADS * HEAD_DIM
CONV_CH = 512
LANES = 128
KEY_CHUNK = 128
CONV_ROWS = 64
FF_CHUNK = 1024
ROW_BLOCK = 512
VMEM_LIMIT_BYTES = 56 * 1024 * 1024
STICK_LOG_FLOOR = -104.0
MASKED_SCORE = -1e30

_F32 = jnp.float32
_BF16 = jnp.bfloat16


def _dot(a, b):
    return jnp.dot(a, b, preferred_element_type=_F32)


def _dot_nt(a, b):
    return lax.dot_general(a, b, (((1,), (1,)), ((), ())), preferred_element_type=_F32)


def _dot_split3(x, m):
    hi = x.astype(_BF16)
    r1 = x - hi.astype(_F32)
    mid = r1.astype(_BF16)
    lo = (r1 - mid.astype(_F32)).astype(_BF16)
    return _dot(hi, m) + _dot(mid, m) + _dot(lo, m)


def _dot_split2(x, m):
    hi = x.astype(_BF16)
    lo = (x - hi.astype(_F32)).astype(_BF16)
    return _dot(hi, m) + _dot(lo, m)


def _inproj_kernel(x_ref, hist_ref, gmix_ref, win_ref, wdw_ref, bdw_ref, gcn_ref, bcn_ref, wpw_ref,
                   gq_ref, gk_ref, gmat_ref,
                   c_ref, q_ref, k_ref, v_ref, nconv_ref,
                   uext_ref, act_ref, *, n_seg, seg_rows):
    t = pl.program_id(1)
    n_t = pl.num_programs(1)
    rows = n_seg * seg_rows
    d_model = x_ref.shape[-1]

    x = x_ref[...].reshape(rows, d_model)
    h = x * lax.rsqrt(jnp.mean(x * x, axis=-1, keepdims=True) + EPS) * gmix_ref[...]
    h = h.astype(_BF16)

    a = _dot(h, win_ref[:, 0:CONV_CH])
    gate = _dot(h, win_ref[:, CONV_CH:2 * CONV_CH])
    u = a * jax.nn.sigmoid(gate)

    @pl.when(t == 0)
    def _():
        uext_ref[:, 0:HIST_PAD, :] = hist_ref[...]

    uext_ref[:, HIST_PAD:HIST_PAD + seg_rows, :] = u.reshape(n_seg, seg_rows, CONV_CH)

    for s in range(n_seg):
        for r in range(seg_rows // CONV_ROWS):
            r0 = r * CONV_ROWS
            acc = jnp.zeros((CONV_ROWS, CONV_CH), _F32)
            for j in range(CONV_WIDTH):
                off = r0 + j + (HIST_PAD - HIST)
                acc = acc + wdw_ref[j:j + 1, :] * uext_ref[s, off:off + CONV_ROWS, :]
            cv = acc + bdw_ref[...]
            mu = jnp.mean(cv, axis=-1, keepdims=True)
            xc = cv - mu
            y = xc * lax.rsqrt(jnp.mean(xc * xc, axis=-1, keepdims=True) + EPS)
            y = y * gcn_ref[...] + bcn_ref[...]
            y = y * jax.nn.sigmoid(y)
            act_ref[s * seg_rows + r0:s * seg_rows + r0 + CONV_ROWS, :] = y.astype(_BF16)

    c_ref[...] = _dot(act_ref[...], wpw_ref[...]).astype(_BF16).reshape(c_ref.shape)

    @pl.when(t == n_t - 1)
    def _():
        nconv_ref[...] = uext_ref[:, seg_rows + (HIST_PAD - HIST):seg_rows + HIST_PAD, :]

    uext_ref[:, 0:HIST_PAD, :] = uext_ref[:, seg_rows:seg_rows + HIST_PAD, :]

    def head_norm(z, g_ref):
        ssq = _dot_split2(z * z, gmat_ref[...])
        return z * lax.rsqrt(ssq * (1.0 / HEAD_DIM) + EPS) * g_ref[...]

    q = _dot(h, win_ref[:, 2 * CONV_CH:2 * CONV_CH + ATTN_W])
    q_ref[...] = head_norm(q, gq_ref).reshape(q_ref.shape)
    k = _dot(h, win_ref[:, 2 * CONV_CH + ATTN_W:2 * CONV_CH + 2 * ATTN_W])
    k_ref[...] = head_norm(k, gk_ref).reshape(k_ref.shape)
    v_ref[...] = _dot(h, win_ref[:, 2 * CONV_CH + 2 * ATTN_W:]).reshape(v_ref.shape)


def _inproj(x, hist, gmix, win, wdw, bdw, gcn, bcn, wpw, gq, gk, gmat, *, n_seg, seg_rows):
    batch, t_len, d_model = x.shape
    assert batch % n_seg == 0 and t_len % seg_rows == 0 and seg_rows % CONV_ROWS == 0
    n_t = t_len // seg_rows
    assert n_seg == 1 or n_t == 1
    grid = (batch // n_seg, n_t)
    rows_spec = lambda w: pl.BlockSpec((n_seg, seg_rows, w), lambda b, t: (b, t, 0))
    const = lambda arr: pl.BlockSpec(arr.shape, lambda b, t: (0,) * arr.ndim)
    out_shape = (
        jax.ShapeDtypeStruct((batch, t_len, CONV_CH), _BF16),
        jax.ShapeDtypeStruct((batch, t_len, ATTN_W), _F32),
        jax.ShapeDtypeStruct((batch, t_len, ATTN_W), _F32),
        jax.ShapeDtypeStruct((batch, t_len, ATTN_W), _F32),
        jax.ShapeDtypeStruct((batch, HIST, CONV_CH), _F32),
    )
    return pl.pallas_call(
        functools.partial(_inproj_kernel, n_seg=n_seg, seg_rows=seg_rows),
        out_shape=out_shape,
        grid=grid,
        in_specs=[rows_spec(d_model),
                  pl.BlockSpec((n_seg, HIST_PAD, CONV_CH), lambda b, t: (b, 0, 0)),
                  const(gmix), const(win), const(wdw), const(bdw), const(gcn), const(bcn), const(wpw),
                  const(gq), const(gk), const(gmat)],
        out_specs=(rows_spec(CONV_CH), rows_spec(ATTN_W), rows_spec(ATTN_W), rows_spec(ATTN_W),
                   pl.BlockSpec((n_seg, HIST, CONV_CH), lambda b, t: (b, 0, 0))),
        scratch_shapes=[pltpu.VMEM((n_seg, seg_rows + HIST_PAD, CONV_CH), _F32),
                        pltpu.VMEM((n_seg * seg_rows, CONV_CH), _BF16)],
        compiler_params=pltpu.CompilerParams(dimension_semantics=("arbitrary", "arbitrary"),
                                             vmem_limit_bytes=VMEM_LIMIT_BYTES),
        name="inproj_conv",
    )(x, hist, gmix, win, wdw, bdw, gcn, bcn, wpw, gq, gk, gmat)


def _attn_kernel(q_ref, kd_ref, vd_ref, kp1_ref, vp1_ref, kp2_ref, vp2_ref, kany_ref, vany_ref,
                 ud_ref, up_ref,
                 o_ref,
                 oacc_ref, run_ref, kbuf_ref, vbuf_ref, sem_ref, *, past_chunks0, chunks_per_q):
    b = pl.program_id(0)
    i = pl.program_id(1)
    tq = q_ref.shape[0]
    n_pairs = ATTN_W // LANES
    n_past = past_chunks0 + i * chunks_per_q
    even_head = lax.broadcasted_iota(jnp.int32, (tq, LANES), 1) < HEAD_DIM
    pair_lanes = [slice(p * LANES, (p + 1) * LANES) for p in range(n_pairs)]

    def stacked_queries():
        qs = []
        for sl in pair_lanes:
            qp = q_ref[:, sl] * (HEAD_DIM ** -0.5)
            qs.append(jnp.concatenate([jnp.where(even_head, qp, 0.0), jnp.where(even_head, 0.0, qp)],
                                      axis=0).astype(_BF16))
        return qs

    def scores(qs, key_refs):
        keys = [jnp.concatenate([r[:, sl] for r in key_refs], axis=0).astype(_BF16) for sl in pair_lanes]
        return jnp.concatenate([_dot_nt(qs[p], keys[p]) for p in range(n_pairs)], axis=0)

    def log_terms(z):
        sp = jnp.log1p(jnp.exp(-jnp.abs(z)))
        return jnp.minimum(z, 0.0) - sp, -jnp.maximum(z, 0.0) - sp

    def suffix_sums(log_1m, u_ref):
        r = _dot_split2(log_1m, u_ref[...])
        return r[:, LANES:], r[:, :LANES]

    def weighted_values(weights, value_refs):
        wb = [a.astype(_BF16) for a in weights]
        outs = []
        for p, sl in enumerate(pair_lanes):
            rows = slice(2 * tq * p, 2 * tq * (p + 1))
            pv = None
            for a, v_ref in zip(wb, value_refs):
                part = _dot(a[rows], v_ref[:, sl].astype(_BF16))
                pv = part if pv is None else pv + part
            outs.append(jnp.where(even_head, pv[:tq], pv[tq:]))
        return jnp.concatenate(outs, axis=1)

    z = scores(stacked_queries(), (kp2_ref, kp1_ref, kd_ref))
    q_row = lax.broadcasted_iota(jnp.int32, (N_HEADS * tq, tq), 0) & (tq - 1)
    causal = lax.broadcasted_iota(jnp.int32, (N_HEADS * tq, tq), 1) < q_row
    z_2 = z[:, 0:KEY_CHUNK] + jnp.where(n_past >= 2, 0.0, MASKED_SCORE)
    z_1 = z[:, KEY_CHUNK:2 * KEY_CHUNK] + jnp.where(n_past >= 1, 0.0, MASKED_SCORE)
    z_d = jnp.where(causal, z[:, 2 * KEY_CHUNK:], MASKED_SCORE)
    lb_d, l1_d = log_terms(z_d)
    lb_1, l1_1 = log_terms(z_1)
    lb_2, l1_2 = log_terms(z_2)
    in_d, tot_d = suffix_sums(l1_d, ud_ref)
    in_1, tot_1 = suffix_sums(l1_1, up_ref)
    in_2, tot_2 = suffix_sums(l1_2, up_ref)
    a_d = jnp.exp(lb_d + in_d)
    a_1 = jnp.exp(lb_1 + (in_1 + tot_d))
    run = tot_d + tot_1
    a_2 = jnp.exp(lb_2 + (in_2 + run))
    run = run + tot_2
    oacc_ref[...] = weighted_values((a_2, a_1, a_d), (vp2_ref, vp1_ref, vd_ref))
    run_ref[...] = run

    def more(carry):
        c, m = carry
        return jnp.logical_and(c >= 0, m > STICK_LOG_FLOOR)

    def older(carry):
        c, _ = carry
        start = pl.multiple_of(c * KEY_CHUNK, KEY_CHUNK)
        k_copy = pltpu.make_async_copy(kany_ref.at[b, pl.ds(start, KEY_CHUNK), :], kbuf_ref, sem_ref.at[0])
        v_copy = pltpu.make_async_copy(vany_ref.at[b, pl.ds(start, KEY_CHUNK), :], vbuf_ref, sem_ref.at[1])
        k_copy.start()
        v_copy.start()
        k_copy.wait()
        v_copy.wait()
        lb, l1 = log_terms(scores(stacked_queries(), (kbuf_ref,)))
        inside, tot = suffix_sums(l1, up_ref)
        run_c = run_ref[...]
        oacc_ref[...] += weighted_values((jnp.exp(lb + (inside + run_c)),), (vbuf_ref,))
        run_c = run_c + tot
        run_ref[...] = run_c
        return c - 1, jnp.max(run_c)

    lax.while_loop(more, older, (n_past - 3, jnp.max(run)))

    o_ref[...] = oacc_ref[...].astype(o_ref.dtype)


def _attention(q, k_new, v_new, k_past, v_past, *, past_is_self):
    batch, t_len, _ = q.shape
    if past_is_self:
        tq = KEY_CHUNK
        past_chunks0, chunks_per_q = 0, 1
    else:
        tq = t_len
        assert k_past.shape[1] % KEY_CHUNK == 0
        past_chunks0, chunks_per_q = k_past.shape[1] // KEY_CHUNK, 0
    assert t_len % tq == 0 and tq <= KEY_CHUNK
    n_q = t_len // tq
    assert past_is_self or n_q == 1

    def newer(n):
        u = lax.broadcasted_iota(jnp.int32, (n, n), 0) > lax.broadcasted_iota(jnp.int32, (n, n), 1)
        return jnp.concatenate([jnp.ones((n, LANES), _BF16), u.astype(_BF16)], axis=1)
    u_diag, u_past = newer(tq), newer(KEY_CHUNK)

    def past_map(back):
        return lambda b, i: (b, jnp.maximum(past_chunks0 + i * chunks_per_q - back, 0), 0)

    blk = lambda rows, imap: pl.BlockSpec((None, rows, ATTN_W), imap)
    own = lambda b, i: (b, i, 0)
    const = lambda arr: pl.BlockSpec(arr.shape, lambda b, i: (0,) * arr.ndim)
    any_spec = pl.BlockSpec(memory_space=pl.ANY)
    return pl.pallas_call(
        functools.partial(_attn_kernel, past_chunks0=past_chunks0, chunks_per_q=chunks_per_q),
        out_shape=jax.ShapeDtypeStruct((batch, t_len, ATTN_W), _BF16),
        grid=(batch, n_q),
        in_specs=[blk(tq, own), blk(tq, own), blk(tq, own),
                  blk(KEY_CHUNK, past_map(1)), blk(KEY_CHUNK, past_map(1)),
                  blk(KEY_CHUNK, past_map(2)), blk(KEY_CHUNK, past_map(2)),
                  any_spec, any_spec, const(u_diag), const(u_past)],
        out_specs=blk(tq, own),
        scratch_shapes=[pltpu.VMEM((tq, ATTN_W), _F32),
                        pltpu.VMEM((N_HEADS * tq, LANES), _F32),
                        pltpu.VMEM((KEY_CHUNK, ATTN_W), _F32),
                        pltpu.VMEM((KEY_CHUNK, ATTN_W), _F32),
                        pltpu.SemaphoreType.DMA((2,))],
        compiler_params=pltpu.CompilerParams(dimension_semantics=("arbitrary", "arbitrary"),
                                             vmem_limit_bytes=VMEM_LIMIT_BYTES),
        name="stick_attention",
    )(q, k_new, v_new, k_past, v_past, k_past, v_past, k_past, v_past, u_diag, u_past)


def _outffn_kernel(x_ref, c_ref, o_ref, wout_ref, gff_ref, w1_ref, w2_ref, y_ref):
    d_ff = w1_ref.shape[1]
    x1 = x_ref[...] + _dot(c_ref[...], wout_ref[0:CONV_CH, :]) + _dot(o_ref[...], wout_ref[CONV_CH:, :])
    h = x1 * lax.rsqrt(jnp.mean(x1 * x1, axis=-1, keepdims=True) + EPS) * gff_ref[...]
    h = h.astype(_BF16)
    ffn = None
    for cc in range(d_ff // FF_CHUNK):
        f = jnp.maximum(_dot(h, w1_ref[:, cc * FF_CHUNK:(cc + 1) * FF_CHUNK]), 0.0)
        part = _dot((f * f).astype(_BF16), w2_ref[cc * FF_CHUNK:(cc + 1) * FF_CHUNK, :])
        ffn = part if ffn is None else ffn + part
    y_ref[...] = x1 + ffn


def _outffn(x, c, o, wout, gff, w1, w2):
    rows, d_model = x.shape
    tb = min(ROW_BLOCK, rows)
    assert rows % tb == 0 and w1.shape[1] % FF_CHUNK == 0
    rows_spec = lambda w: pl.BlockSpec((tb, w), lambda r: (r, 0))
    const = lambda arr: pl.BlockSpec(arr.shape, lambda r: (0,) * arr.ndim, pipeline_mode=pl.Buffered(1))
    return pl.pallas_call(
        _outffn_kernel,
        out_shape=jax.ShapeDtypeStruct((rows, d_model), _F32),
        grid=(rows // tb,),
        in_specs=[rows_spec(d_model), rows_spec(CONV_CH), rows_spec(ATTN_W),
                  const(wout), const(gff), const(w1), const(w2)],
        out_specs=rows_spec(d_model),
        compiler_params=pltpu.CompilerParams(dimension_semantics=("arbitrary",),
                                             vmem_limit_bytes=VMEM_LIMIT_BYTES),
        name="outproj_ffn",
    )(x, c, o, wout, gff, w1, w2)


def _layer(x, hist, k_past, v_past, w, *, n_seg, seg_rows):
    batch, t_len, d_model = x.shape
    c, q, k, v, new_conv = _inproj(x, hist, w["gmix"], w["win"], w["wdw"], w["bdw"], w["gcn"], w["bcn"],
                                   w["wpw"], w["gq"], w["gk"], w["gmat"], n_seg=n_seg, seg_rows=seg_rows)
    if k_past is None:
        o = _attention(q, k, v, k, v, past_is_self=True)
    else:
        o = _attention(q, k, v, k_past, v_past, past_is_self=False)
    rows = batch * t_len
    y = _outffn(x.reshape(rows, d_model), c.reshape(rows, CONV_CH), o.reshape(rows, ATTN_W),
                w["wout"], w["gff"], w["w1"], w["w2"])
    return y.reshape(batch, t_len, d_model), k, v, new_conv


def kernel(x_prompt, x_sample, cache_k, cache_v, state_conv, g_mix, w_in, w_dw, b_dw, g_cn, b_cn, w_pw,
           g_q, g_k, w_out, g_ff, w_ff1, w_ff2):
    depth = w_in.shape[0]
    bp, tp, _ = x_prompt.shape
    bs, ts, _ = x_sample.shape
    n_past = cache_k.shape[2]
    head_id = lax.broadcasted_iota(jnp.int32, (ATTN_W, ATTN_W), 0) // HEAD_DIM
    gmat = (head_id == head_id.T).astype(_BF16)
    zero_hist = jnp.zeros((bp, HIST_PAD, CONV_CH), _F32)
    row = lambda a: a.reshape(1, -1)

    hp, hs = x_prompt, x_sample
    outs = [[] for _ in range(6)]
    for l in range(depth):
        w = dict(gmix=row(g_mix[l]), win=w_in[l].astype(_BF16), wdw=w_dw[l], bdw=row(b_dw[l]),
                 gcn=row(g_cn[l]), bcn=row(b_cn[l]), wpw=w_pw[l].astype(_BF16),
                 gq=row(jnp.tile(g_q[l], N_HEADS)), gk=row(jnp.tile(g_k[l], N_HEADS)), gmat=gmat,
                 wout=w_out[l].astype(_BF16), gff=row(g_ff[l]),
                 w1=w_ff1[l].astype(_BF16), w2=w_ff2[l].astype(_BF16))
        hp, k, v, cv = _layer(hp, zero_hist, None, None, w, n_seg=1, seg_rows=min(ROW_BLOCK, tp))
        outs[0].append(k.reshape(bp, tp, N_HEADS, HEAD_DIM))
        outs[1].append(v.reshape(bp, tp, N_HEADS, HEAD_DIM))
        outs[2].append(cv)
        hist = jnp.pad(state_conv[l], ((0, 0), (HIST_PAD - HIST, 0), (0, 0)))
        hs, k, v, cv = _layer(hs, hist, cache_k[l].reshape(bs, n_past, ATTN_W),
                              cache_v[l].reshape(bs, n_past, ATTN_W), w, n_seg=bs, seg_rows=ts)
        outs[3].append(k.reshape(bs, ts, N_HEADS, HEAD_DIM))
        outs[4].append(v.reshape(bs, ts, N_HEADS, HEAD_DIM))
        outs[5].append(cv)
    return (hp, hs) + tuple(jnp.stack(o) for o in outs)
```

```python
import functools

import jax
import jax.numpy as jnp
from jax import lax
from jax.experimental import pallas as pl
from jax.experimental.pallas import tpu as pltpu

EPS = 1e-6
CONV_WIDTH = 31
HIST = CONV_WIDTH - 1
HIST_PAD = 32
SUBLANES = 8
N_HEADS = 8
HEAD_DIM = 64
ATTN_W = N_HEADS * HEAD_DIM
CONV_CH = 512
LANES = 128
KEY_CHUNK = 128
CONV_ROWS = 64
FF_CHUNK = 1024
ROW_BLOCK = 512
VMEM_LIMIT_BYTES = 56 * 1024 * 1024
STICK_LOG_FLOOR = -104.0
MASKED_SCORE = -1e30

_F32 = jnp.float32
_BF16 = jnp.bfloat16


def _dot(a, b):
    return jnp.dot(a, b, preferred_element_type=_F32)


def _dot_nt(a, b):
    return lax.dot_general(a, b, (((1,), (1,)), ((), ())), preferred_element_type=_F32)


def _dot_split2(x, m):
    hi = x.astype(_BF16)
    lo = (x - hi.astype(_F32)).astype(_BF16)
    return _dot(hi, m) + _dot(lo, m)


def _inproj_kernel(*refs, n_seg, seg_rows, feature_major_kv, first_layer):
    if feature_major_kv:
        (x_ref, hist_ref, gmix_ref, win_ref, wkv_ref, wdw_ref, bdw_ref, gcn_ref, bcn_ref, wpw_ref,
         gq_ref, gk_ref, gmat_ref, _, _,
         c_ref, q_ref, k_ref, v_ref, nconv_ref, uext_ref, shift_ref, act_ref) = refs
        if first_layer:
            for stack_ref in (k_ref, v_ref):
                if stack_ref.shape[0] > 1:
                    stack_ref[1:] = jnp.zeros((stack_ref.shape[0] - 1,) + stack_ref.shape[1:], _F32)
            k_ref, v_ref = k_ref.at[0], v_ref.at[0]
    else:
        (x_ref, hist_ref, gmix_ref, win_ref, wdw_ref, bdw_ref, gcn_ref, bcn_ref, wpw_ref,
         gq_ref, gk_ref, gmat_ref,
         c_ref, q_ref, k_ref, v_ref, nconv_ref, uext_ref, shift_ref, act_ref) = refs
    t = pl.program_id(1)
    n_t = pl.num_programs(1)
    rows = n_seg * seg_rows
    d_model = x_ref.shape[-1]

    x = x_ref[...].reshape(rows, d_model)
    h = x * lax.rsqrt(jnp.mean(x * x, axis=-1, keepdims=True) + EPS) * gmix_ref[...]
    h = h.astype(_BF16)

    a = _dot(h, win_ref[:, 0:CONV_CH])
    gate = _dot(h, win_ref[:, CONV_CH:2 * CONV_CH])
    u = a * jax.nn.sigmoid(gate)

    @pl.when(t == 0)
    def _():
        uext_ref[:, 0:HIST_PAD, :] = hist_ref[...]

    uext_ref[:, HIST_PAD:HIST_PAD + seg_rows, :] = u.reshape(n_seg, seg_rows, CONV_CH)

    shift_rows = shift_ref.shape[1]
    for s in range(n_seg):
        for b in range(1, SUBLANES):
            shift_ref[b - 1] = uext_ref[s, b:b + shift_rows, :]
        for r in range(seg_rows // CONV_ROWS):
            r0 = r * CONV_ROWS
            acc = jnp.zeros((CONV_ROWS, CONV_CH), _F32)
            for j in range(CONV_WIDTH):
                off = j + (HIST_PAD - HIST)
                b = off % SUBLANES
                lo = r0 + off - b
                if b == 0:
                    tap = uext_ref[s, lo:lo + CONV_ROWS, :]
                else:
                    tap = shift_ref[b - 1, lo:lo + CONV_ROWS, :]
                acc = acc + wdw_ref[j:j + 1, :] * tap
            cv = acc + bdw_ref[...]
            mu = jnp.mean(cv, axis=-1, keepdims=True)
            xc = cv - mu
            y = xc * lax.rsqrt(jnp.mean(xc * xc, axis=-1, keepdims=True) + EPS)
            y = y * gcn_ref[...] + bcn_ref[...]
            y = y * jax.nn.sigmoid(y)
            act_ref[s * seg_rows + r0:s * seg_rows + r0 + CONV_ROWS, :] = y.astype(_BF16)

    c_ref[...] = _dot(act_ref[...], wpw_ref[...]).astype(_BF16).reshape(c_ref.shape)

    @pl.when(t == n_t - 1)
    def _():
        nconv_ref[...] = uext_ref[:, seg_rows + (HIST_PAD - HIST):seg_rows + HIST_PAD, :]

    uext_ref[:, 0:HIST_PAD, :] = uext_ref[:, seg_rows:seg_rows + HIST_PAD, :]

    def head_norm(z, g_ref):
        ssq = _dot_split2(z * z, gmat_ref[...])
        return z * lax.rsqrt(ssq * (1.0 / HEAD_DIM) + EPS) * g_ref[...]

    q = _dot(h, win_ref[:, 2 * CONV_CH:2 * CONV_CH + ATTN_W])
    q_ref[...] = (head_norm(q, gq_ref) * (HEAD_DIM ** -0.5)).astype(_BF16).reshape(q_ref.shape)
    if feature_major_kv:
        k_t = _dot_nt(wkv_ref[0:ATTN_W, :], h)
        gain = jnp.concatenate([gk_ref[...]] * (rows // LANES), axis=1)
        for hh in range(N_HEADS):
            rs = slice(hh * HEAD_DIM, (hh + 1) * HEAD_DIM)
            kh = k_t[rs]
            ssq = jnp.sum(kh * kh, axis=0, keepdims=True)
            k_ref[rs, :] = kh * lax.rsqrt(ssq * (1.0 / HEAD_DIM) + EPS) * gain[rs]
        v_ref[...] = _dot_nt(wkv_ref[ATTN_W:, :], h)
    else:
        k = _dot(h, win_ref[:, 2 * CONV_CH + ATTN_W:2 * CONV_CH + 2 * ATTN_W])
        k_ref[...] = head_norm(k, gk_ref).reshape(k_ref.shape)
        v_ref[...] = _dot(h, win_ref[:, 2 * CONV_CH + 2 * ATTN_W:]).reshape(v_ref.shape)


def _inproj(x, hist, w, *, n_seg, seg_rows, kv_stack=None, layer=0, depth=1):
    batch, t_len, d_model = x.shape
    assert batch % n_seg == 0 and t_len % seg_rows == 0 and seg_rows % CONV_ROWS == 0
    n_t = t_len // seg_rows
    assert n_seg == 1 or n_t == 1
    feature_major_kv = kv_stack is not None
    rows_spec = lambda width: pl.BlockSpec((n_seg, seg_rows, width), lambda b, t: (b, t, 0))
    const = lambda arr: pl.BlockSpec(arr.shape, lambda b, t: (0,) * arr.ndim)
    hist_spec = pl.BlockSpec((n_seg, HIST_PAD, CONV_CH), lambda b, t: (b, 0, 0))
    nconv_spec = pl.BlockSpec((n_seg, HIST, CONV_CH), lambda b, t: (b, 0, 0))
    small = [w["wdw"], w["bdw"], w["gcn"], w["bcn"], w["wpw"]]
    aliases = {}
    if feature_major_kv:
        assert n_seg == 1 and seg_rows % LANES == 0
        args = [x, hist, w["gmix"], w["w_agq"], w["w_kv_t"]] + small + [w["gq"], w["gk_col"], w["gmat"]]
        in_specs = [rows_spec(d_model), hist_spec] + [const(a) for a in args[2:]]
        kv_shape = jax.ShapeDtypeStruct((depth, batch, ATTN_W, t_len), _F32)
        first_layer = kv_stack[0] is None
        if first_layer:
            kv_spec = pl.BlockSpec((depth, None, ATTN_W, seg_rows), lambda b, t: (0, b, 0, t))
            dummy = jnp.zeros((SUBLANES, LANES), _F32)
            args += [dummy, dummy]
            in_specs += [const(dummy)] * 2
        else:
            kv_spec = pl.BlockSpec((None, None, ATTN_W, seg_rows), lambda b, t: (layer, b, 0, t))
            aliases = {len(args): 2, len(args) + 1: 3}
            args += list(kv_stack)
            in_specs += [pl.BlockSpec(memory_space=pl.ANY)] * 2
    else:
        args = [x, hist, w["gmix"], w["win"]] + small + [w["gq"], w["gk"], w["gmat"]]
        in_specs = [rows_spec(d_model), hist_spec] + [const(a) for a in args[2:]]
        kv_shape = jax.ShapeDtypeStruct((batch, t_len, ATTN_W), _F32)
        kv_spec = rows_spec(ATTN_W)
        first_layer = False
    out_shape = (
        jax.ShapeDtypeStruct((batch, t_len, CONV_CH), _BF16),
        jax.ShapeDtypeStruct((batch, t_len, ATTN_W), _BF16),
        kv_shape, kv_shape,
        jax.ShapeDtypeStruct((batch, HIST, CONV_CH), _F32),
    )
    return pl.pallas_call(
        functools.partial(_inproj_kernel, n_seg=n_seg, seg_rows=seg_rows, feature_major_kv=feature_major_kv,
                          first_layer=first_layer),
        out_shape=out_shape,
        grid=(batch // n_seg, n_t),
        in_specs=in_specs,
        out_specs=(rows_spec(CONV_CH), rows_spec(ATTN_W), kv_spec, kv_spec, nconv_spec),
        scratch_shapes=[pltpu.VMEM((n_seg, seg_rows + HIST_PAD, CONV_CH), _F32),
                        pltpu.VMEM((SUBLANES - 1, seg_rows + HIST_PAD - SUBLANES, CONV_CH), _F32),
                        pltpu.VMEM((n_seg * seg_rows, CONV_CH), _BF16)],
        input_output_aliases=aliases,
        compiler_params=pltpu.CompilerParams(dimension_semantics=("arbitrary", "arbitrary"),
                                             vmem_limit_bytes=VMEM_LIMIT_BYTES),
        name="inproj_conv",
    )(*args)


def _attn_kernel(*refs, past_chunks0, chunks_per_q, layer, own_feature_major):
    (q_ref, kd_ref, vd_ref, kp1_ref, vp1_ref, kp2_ref, vp2_ref, kany_ref, vany_ref, ud_ref, up_ref,
     o_ref, oacc_ref, run_ref, kbuf_ref, vbuf_ref, sem_ref) = refs
    b = pl.program_id(0)
    i = pl.program_id(1)
    tq = q_ref.shape[0]
    n_pairs = ATTN_W // LANES
    n_past = past_chunks0 + i * chunks_per_q
    even_head = lax.broadcasted_iota(jnp.int32, (tq, LANES), 1) < HEAD_DIM
    pair_sl = [slice(p * LANES, (p + 1) * LANES) for p in range(n_pairs)]

    def stacked_queries():
        qs = []
        for sl in pair_sl:
            qp = q_ref[:, sl]
            qs.append(jnp.concatenate([jnp.where(even_head, qp, 0.0), jnp.where(even_head, 0.0, qp)], axis=0))
        return qs

    def scores_fm(qs, key_refs):
        out = []
        for p, sl in enumerate(pair_sl):
            keys = jnp.concatenate([r[sl, :] for r in key_refs], axis=1).astype(_BF16)
            out.append(_dot(qs[p], keys))
        return jnp.concatenate(out, axis=0)

    def log_terms(z):
        sp = jnp.log1p(jnp.exp(-jnp.abs(z)))
        return jnp.minimum(z, 0.0) - sp, -jnp.maximum(z, 0.0) - sp

    def suffix_sums(log_1m, u_ref):
        r = _dot_split2(log_1m, u_ref[...])
        return r[:, LANES:], r[:, :LANES]

    def weighted_values(fm_weights, fm_value_refs, rm_weight=None, rm_value_ref=None):
        wb = jnp.concatenate(fm_weights, axis=1).astype(_BF16)
        outs = []
        for p, sl in enumerate(pair_sl):
            rows = slice(2 * tq * p, 2 * tq * (p + 1))
            vals = jnp.concatenate([r[sl, :] for r in fm_value_refs], axis=1).astype(_BF16)
            pv = _dot_nt(wb[rows], vals)
            if rm_weight is not None:
                pv = pv + _dot(rm_weight[rows].astype(_BF16), rm_value_ref[:, sl].astype(_BF16))
            outs.append(jnp.where(even_head, pv[:tq], pv[tq:]))
        return jnp.concatenate(outs, axis=1)

    qs = stacked_queries()
    if own_feature_major:
        z = scores_fm(qs, (kp2_ref, kp1_ref, kd_ref))
        z_own = z[:, 2 * KEY_CHUNK:]
    else:
        z = scores_fm(qs, (kp2_ref, kp1_ref))
        z_own = jnp.concatenate([_dot_nt(qs[p], kd_ref[:, sl].astype(_BF16)) for p, sl in enumerate(pair_sl)],
                                axis=0)
    q_row = lax.broadcasted_iota(jnp.int32, (N_HEADS * tq, tq), 0) & (tq - 1)
    causal = lax.broadcasted_iota(jnp.int32, (N_HEADS * tq, tq), 1) < q_row
    z_2 = z[:, 0:KEY_CHUNK] + jnp.where(n_past >= 2, 0.0, MASKED_SCORE)
    z_1 = z[:, KEY_CHUNK:2 * KEY_CHUNK] + jnp.where(n_past >= 1, 0.0, MASKED_SCORE)
    z_d = jnp.where(causal, z_own, MASKED_SCORE)
    lb_d, l1_d = log_terms(z_d)
    lb_1, l1_1 = log_terms(z_1)
    lb_2, l1_2 = log_terms(z_2)
    in_d, tot_d = suffix_sums(l1_d, ud_ref)
    in_1, tot_1 = suffix_sums(l1_1, up_ref)
    in_2, tot_2 = suffix_sums(l1_2, up_ref)
    a_d = jnp.exp(lb_d + in_d)
    a_1 = jnp.exp(lb_1 + (in_1 + tot_d))
    run = tot_d + tot_1
    a_2 = jnp.exp(lb_2 + (in_2 + run))
    run = run + tot_2
    if own_feature_major:
        oacc_ref[...] = weighted_values((a_2, a_1, a_d), (vp2_ref, vp1_ref, vd_ref))
    else:
        oacc_ref[...] = weighted_values((a_2, a_1), (vp2_ref, vp1_ref), a_d, vd_ref)
    run_ref[...] = run

    def more(carry):
        c, m = carry
        return jnp.logical_and(c >= 0, m > STICK_LOG_FLOOR)

    def older(carry):
        c, _ = carry
        start = pl.multiple_of(c * KEY_CHUNK, KEY_CHUNK)
        k_copy = pltpu.make_async_copy(kany_ref.at[layer, b, :, pl.ds(start, KEY_CHUNK)], kbuf_ref, sem_ref.at[0])
        v_copy = pltpu.make_async_copy(vany_ref.at[layer, b, :, pl.ds(start, KEY_CHUNK)], vbuf_ref, sem_ref.at[1])
        k_copy.start()
        v_copy.start()
        k_copy.wait()
        v_copy.wait()
        lb, l1 = log_terms(scores_fm(stacked_queries(), (kbuf_ref,)))
        inside, tot = suffix_sums(l1, up_ref)
        run_c = run_ref[...]
        oacc_ref[...] += weighted_values((jnp.exp(lb + (inside + run_c)),), (vbuf_ref,))
        run_c = run_c + tot
        run_ref[...] = run_c
        return c - 1, jnp.max(run_c)

    lax.while_loop(more, older, (n_past - 3, jnp.max(run)))

    o_ref[...] = oacc_ref[...].astype(o_ref.dtype)


def _attention(q, k_own, v_own, k_past, v_past, *, layer, past_is_own):
    batch, t_len, _ = q.shape
    if past_is_own:
        tq = KEY_CHUNK
        past_chunks0, chunks_per_q = 0, 1
    else:
        tq = t_len
        assert k_past.shape[-1] % KEY_CHUNK == 0
        past_chunks0, chunks_per_q = k_past.shape[-1] // KEY_CHUNK, 0
    assert t_len % tq == 0 and tq <= KEY_CHUNK and tq & (tq - 1) == 0
    n_q = t_len // tq
    assert past_is_own or n_q == 1

    def newer(n):
        u = lax.broadcasted_iota(jnp.int32, (n, n), 0) > lax.broadcasted_iota(jnp.int32, (n, n), 1)
        return jnp.concatenate([jnp.ones((n, LANES), _BF16), u.astype(_BF16)], axis=1)
    u_diag, u_past = newer(tq), newer(KEY_CHUNK)

    def fm_chunk(back):
        imap = lambda b, i: (layer, b, 0, jnp.maximum(past_chunks0 + i * chunks_per_q - back, 0))
        return pl.BlockSpec((None, None, ATTN_W, KEY_CHUNK), imap)

    rm_block = pl.BlockSpec((None, tq, ATTN_W), lambda b, i: (b, i, 0))
    const = lambda arr: pl.BlockSpec(arr.shape, lambda b, i: (0,) * arr.ndim)
    any_spec = pl.BlockSpec(memory_space=pl.ANY)
    if past_is_own:
        k_own, v_own = k_past, v_past
        own_spec = fm_chunk(0)
    else:
        own_spec = rm_block
    return pl.pallas_call(
        functools.partial(_attn_kernel, past_chunks0=past_chunks0, chunks_per_q=chunks_per_q, layer=layer,
                          own_feature_major=past_is_own),
        out_shape=jax.ShapeDtypeStruct((batch, t_len, ATTN_W), _BF16),
        grid=(batch, n_q),
        in_specs=[rm_block, own_spec, own_spec, fm_chunk(1), fm_chunk(1), fm_chunk(2), fm_chunk(2),
                  any_spec, any_spec, const(u_diag), const(u_past)],
        out_specs=rm_block,
        scratch_shapes=[pltpu.VMEM((tq, ATTN_W), _F32),
                        pltpu.VMEM((N_HEADS * tq, LANES), _F32),
                        pltpu.VMEM((ATTN_W, KEY_CHUNK), _F32),
                        pltpu.VMEM((ATTN_W, KEY_CHUNK), _F32),
                        pltpu.SemaphoreType.DMA((2,))],
        compiler_params=pltpu.CompilerParams(dimension_semantics=("arbitrary", "arbitrary"),
                                             vmem_limit_bytes=VMEM_LIMIT_BYTES),
        name="stick_attention",
    )(q, k_own, v_own, k_past, v_past, k_past, v_past, k_past, v_past, u_diag, u_past)


def _outffn_kernel(x_ref, c_ref, o_ref, wout_ref, gff_ref, w1_ref, w2_ref, y_ref):
    d_ff = w1_ref.shape[1]
    x1 = x_ref[...] + _dot(c_ref[...], wout_ref[0:CONV_CH, :]) + _dot(o_ref[...], wout_ref[CONV_CH:, :])
    h = x1 * lax.rsqrt(jnp.mean(x1 * x1, axis=-1, keepdims=True) + EPS) * gff_ref[...]
    h = h.astype(_BF16)
    ffn = None
    for cc in range(d_ff // FF_CHUNK):
        f = jnp.maximum(_dot(h, w1_ref[:, cc * FF_CHUNK:(cc + 1) * FF_CHUNK]), 0.0)
        part = _dot((f * f).astype(_BF16), w2_ref[cc * FF_CHUNK:(cc + 1) * FF_CHUNK, :])
        ffn = part if ffn is None else ffn + part
    y_ref[...] = x1 + ffn


def _outffn(x, c, o, wout, gff, w1, w2):
    rows, d_model = x.shape
    tb = min(ROW_BLOCK, rows)
    assert rows % tb == 0 and w1.shape[1] % FF_CHUNK == 0
    rows_spec = lambda w: pl.BlockSpec((tb, w), lambda r: (r, 0))
    const = lambda arr: pl.BlockSpec(arr.shape, lambda r: (0,) * arr.ndim, pipeline_mode=pl.Buffered(1))
    return pl.pallas_call(
        _outffn_kernel,
        out_shape=jax.ShapeDtypeStruct((rows, d_model), _F32),
        grid=(rows // tb,),
        in_specs=[rows_spec(d_model), rows_spec(CONV_CH), rows_spec(ATTN_W),
                  const(wout), const(gff), const(w1), const(w2)],
        out_specs=rows_spec(d_model),
        compiler_params=pltpu.CompilerParams(dimension_semantics=("arbitrary",),
                                             vmem_limit_bytes=VMEM_LIMIT_BYTES),
        name="outproj_ffn",
    )(x, c, o, wout, gff, w1, w2)


def _mix_and_ffn(x, c, o, w):
    batch, t_len, d_model = x.shape
    rows = batch * t_len
    y = _outffn(x.reshape(rows, d_model), c.reshape(rows, CONV_CH), o.reshape(rows, ATTN_W),
                w["wout"], w["gff"], w["w1"], w["w2"])
    return y.reshape(batch, t_len, d_model)


def kernel(x_prompt, x_sample, cache_k, cache_v, state_conv, g_mix, w_in, w_dw, b_dw, g_cn, b_cn, w_pw,
           g_q, g_k, w_out, g_ff, w_ff1, w_ff2):
    depth = w_in.shape[0]
    bp, tp, _ = x_prompt.shape
    bs, ts, _ = x_sample.shape
    n_past = cache_k.shape[2]
    head_id = lax.broadcasted_iota(jnp.int32, (ATTN_W, ATTN_W), 0) // HEAD_DIM
    gmat = (head_id == head_id.T).astype(_BF16)
    zero_hist = jnp.zeros((bp, HIST_PAD, CONV_CH), _F32)
    row = lambda a: a.reshape(1, -1)
    to_fm = lambda a: jnp.transpose(a, (0, 1, 3, 4, 2)).reshape(a.shape[0], a.shape[1], ATTN_W, a.shape[2])
    from_fm = lambda a: jnp.transpose(a.reshape(a.shape[0], a.shape[1], N_HEADS, HEAD_DIM, a.shape[3]),
                                      (0, 1, 4, 2, 3))
    cache_k_fm, cache_v_fm = to_fm(cache_k), to_fm(cache_v)

    hp, hs = x_prompt, x_sample
    kv_stack = (None, None)
    conv_p, k_s, v_s, conv_s = [], [], [], []
    for l in range(depth):
        win = w_in[l].astype(_BF16)
        w = dict(gmix=row(g_mix[l]), win=win, w_agq=win[:, :2 * CONV_CH + ATTN_W],
                 w_kv_t=win[:, 2 * CONV_CH + ATTN_W:].T, wdw=w_dw[l], bdw=row(b_dw[l]),
                 gcn=row(g_cn[l]), bcn=row(b_cn[l]), wpw=w_pw[l].astype(_BF16),
                 gq=row(jnp.tile(g_q[l], N_HEADS)), gk=row(jnp.tile(g_k[l], N_HEADS)),
                 gk_col=jnp.broadcast_to(jnp.tile(g_k[l], N_HEADS)[:, None], (ATTN_W, LANES)), gmat=gmat,
                 wout=w_out[l].astype(_BF16), gff=row(g_ff[l]),
                 w1=w_ff1[l].astype(_BF16), w2=w_ff2[l].astype(_BF16))
        c, q, k_fm, v_fm, cv = _inproj(hp, zero_hist, w, n_seg=1, seg_rows=min(ROW_BLOCK, tp),
                                       kv_stack=kv_stack, layer=l, depth=depth)
        kv_stack = (k_fm, v_fm)
        o = _attention(q, None, None, k_fm, v_fm, layer=l, past_is_own=True)
        hp = _mix_and_ffn(hp, c, o, w)
        conv_p.append(cv)
        hist = jnp.pad(state_conv[l], ((0, 0), (HIST_PAD - HIST, 0), (0, 0)))
        c, q, k, v, cv = _inproj(hs, hist, w, n_seg=bs, seg_rows=ts)
        o = _attention(q, k, v, cache_k_fm, cache_v_fm, layer=l, past_is_own=False)
        hs = _mix_and_ffn(hs, c, o, w)
        k_s.append(k.reshape(bs, ts, N_HEADS, HEAD_DIM))
        v_s.append(v.reshape(bs, ts, N_HEADS, HEAD_DIM))
        conv_s.append(cv)
    return (hp, hs, from_fm(kv_stack[0]), from_fm(kv_stack[1]), jnp.stack(conv_p),
            jnp.stack(k_s), jnp.stack(v_s), jnp.stack(conv_s))
```

```python
import functools

import jax
import jax.numpy as jnp
from jax import lax
from jax.experimental import pallas as pl
from jax.experimental.pallas import tpu as pltpu

EPS = 1e-6
CONV_WIDTH = 31
HIST = CONV_WIDTH - 1
HIST_PAD = 32
SUBLANES = 8
N_HEADS = 8
HEAD_DIM = 64
ATTN_W = N_HEADS * HEAD_DIM
CONV_CH = 512
LANES = 128
KEY_CHUNK = 128
CONV_ROWS = 64
FF_CHUNK = 1024
ROW_BLOCK = 512
VMEM_LIMIT_BYTES = 56 * 1024 * 1024
STICK_LOG_FLOOR = -104.0
MASKED_SCORE = -1e30

_F32 = jnp.float32
_BF16 = jnp.bfloat16


def _dot(a, b):
    return jnp.dot(a, b, preferred_element_type=_F32)


def _dot_nt(a, b):
    return lax.dot_general(a, b, (((1,), (1,)), ((), ())), preferred_element_type=_F32)


def _dot_split2(x, m):
    hi = x.astype(_BF16)
    lo = (x - hi.astype(_F32)).astype(_BF16)
    return _dot(hi, m) + _dot(lo, m)


def _inproj_kernel(*refs, n_seg, seg_rows, feature_major_kv, first_layer):
    if feature_major_kv:
        (x_ref, hist_ref, gmix_ref, win_ref, wkv_ref, wdw_ref, bdw_ref, gcn_ref, bcn_ref, wpw_ref,
         gq_ref, gk_ref, gmat_ref, _, _,
         c_ref, q_ref, k_ref, v_ref, nconv_ref, uext_ref, shift_ref, act_ref) = refs
        if first_layer:
            for stack_ref in (k_ref, v_ref):
                if stack_ref.shape[0] > 1:
                    stack_ref[1:] = jnp.zeros((stack_ref.shape[0] - 1,) + stack_ref.shape[1:], _F32)
            k_ref, v_ref = k_ref.at[0], v_ref.at[0]
    else:
        (x_ref, hist_ref, gmix_ref, win_ref, wdw_ref, bdw_ref, gcn_ref, bcn_ref, wpw_ref,
         gq_ref, gk_ref, gmat_ref,
         c_ref, q_ref, k_ref, v_ref, nconv_ref, uext_ref, shift_ref, act_ref) = refs
    t = pl.program_id(1)
    n_t = pl.num_programs(1)
    rows = n_seg * seg_rows
    d_model = x_ref.shape[-1]

    x = x_ref[...].reshape(rows, d_model)
    h = x * lax.rsqrt(jnp.mean(x * x, axis=-1, keepdims=True) + EPS) * gmix_ref[...]
    h = h.astype(_BF16)

    a = _dot(h, win_ref[:, 0:CONV_CH])
    gate = _dot(h, win_ref[:, CONV_CH:2 * CONV_CH])
    u = a * jax.nn.sigmoid(gate)

    @pl.when(t == 0)
    def _():
        uext_ref[:, 0:HIST_PAD, :] = hist_ref[...]

    uext_ref[:, HIST_PAD:HIST_PAD + seg_rows, :] = u.reshape(n_seg, seg_rows, CONV_CH)

    shift_rows = shift_ref.shape[1]
    for s in range(n_seg):
        for b in range(1, SUBLANES):
            shift_ref[b - 1] = uext_ref[s, b:b + shift_rows, :]
        for r in range(seg_rows // CONV_ROWS):
            r0 = r * CONV_ROWS
            acc = jnp.zeros((CONV_ROWS, CONV_CH), _F32)
            for j in range(CONV_WIDTH):
                off = j + (HIST_PAD - HIST)
                b = off % SUBLANES
                lo = r0 + off - b
                if b == 0:
                    tap = uext_ref[s, lo:lo + CONV_ROWS, :]
                else:
                    tap = shift_ref[b - 1, lo:lo + CONV_ROWS, :]
                acc = acc + wdw_ref[j:j + 1, :] * tap
            cv = acc + bdw_ref[...]
            mu = jnp.mean(cv, axis=-1, keepdims=True)
            xc = cv - mu
            y = xc * lax.rsqrt(jnp.mean(xc * xc, axis=-1, keepdims=True) + EPS)
            y = y * gcn_ref[...] + bcn_ref[...]
            y = y * jax.nn.sigmoid(y)
            act_ref[s * seg_rows + r0:s * seg_rows + r0 + CONV_ROWS, :] = y.astype(_BF16)

    c_ref[...] = _dot(act_ref[...], wpw_ref[...]).astype(_BF16).reshape(c_ref.shape)

    @pl.when(t == n_t - 1)
    def _():
        nconv_ref[...] = uext_ref[:, seg_rows + (HIST_PAD - HIST):seg_rows + HIST_PAD, :]

    uext_ref[:, 0:HIST_PAD, :] = uext_ref[:, seg_rows:seg_rows + HIST_PAD, :]

    def head_norm(z, g_ref):
        ssq = _dot_split2(z * z, gmat_ref[...])
        return z * lax.rsqrt(ssq * (1.0 / HEAD_DIM) + EPS) * g_ref[...]

    q = _dot(h, win_ref[:, 2 * CONV_CH:2 * CONV_CH + ATTN_W])
    q_ref[...] = (head_norm(q, gq_ref) * (HEAD_DIM ** -0.5)).astype(_BF16).reshape(q_ref.shape)
    if feature_major_kv:
        k_t = _dot_nt(wkv_ref[0:ATTN_W, :], h)
        gain = jnp.concatenate([gk_ref[...]] * (rows // LANES), axis=1)
        for hh in range(N_HEADS):
            rs = slice(hh * HEAD_DIM, (hh + 1) * HEAD_DIM)
            kh = k_t[rs]
            ssq = jnp.sum(kh * kh, axis=0, keepdims=True)
            k_ref[rs, :] = kh * lax.rsqrt(ssq * (1.0 / HEAD_DIM) + EPS) * gain[rs]
        v_ref[...] = _dot_nt(wkv_ref[ATTN_W:, :], h)
    else:
        k = _dot(h, win_ref[:, 2 * CONV_CH + ATTN_W:2 * CONV_CH + 2 * ATTN_W])
        k_ref[...] = head_norm(k, gk_ref).reshape(k_ref.shape)
        v_ref[...] = _dot(h, win_ref[:, 2 * CONV_CH + 2 * ATTN_W:]).reshape(v_ref.shape)


def _inproj(x, hist, w, *, n_seg, seg_rows, kv_stack=None, layer=0, depth=1):
    batch, t_len, d_model = x.shape
    assert batch % n_seg == 0 and t_len % seg_rows == 0 and seg_rows % CONV_ROWS == 0
    n_t = t_len // seg_rows
    assert n_seg == 1 or n_t == 1
    feature_major_kv = kv_stack is not None
    rows_spec = lambda width: pl.BlockSpec((n_seg, seg_rows, width), lambda b, t: (b, t, 0))
    const = lambda arr: pl.BlockSpec(arr.shape, lambda b, t: (0,) * arr.ndim)
    hist_spec = pl.BlockSpec((n_seg, HIST_PAD, CONV_CH), lambda b, t: (b, 0, 0))
    nconv_spec = pl.BlockSpec((n_seg, HIST, CONV_CH), lambda b, t: (b, 0, 0))
    small = [w["wdw"], w["bdw"], w["gcn"], w["bcn"], w["wpw"]]
    aliases = {}
    if feature_major_kv:
        assert n_seg == 1 and seg_rows % LANES == 0
        args = [x, hist, w["gmix"], w["w_agq"], w["w_kv_t"]] + small + [w["gq"], w["gk_col"], w["gmat"]]
        in_specs = [rows_spec(d_model), hist_spec] + [const(a) for a in args[2:]]
        kv_shape = jax.ShapeDtypeStruct((depth, batch, ATTN_W, t_len), _F32)
        first_layer = kv_stack[0] is None
        if first_layer:
            kv_spec = pl.BlockSpec((depth, None, ATTN_W, seg_rows), lambda b, t: (0, b, 0, t))
            dummy = jnp.zeros((SUBLANES, LANES), _F32)
            args += [dummy, dummy]
            in_specs += [const(dummy)] * 2
        else:
            kv_spec = pl.BlockSpec((None, None, ATTN_W, seg_rows), lambda b, t: (layer, b, 0, t))
            aliases = {len(args): 2, len(args) + 1: 3}
            args += list(kv_stack)
            in_specs += [pl.BlockSpec(memory_space=pl.ANY)] * 2
    else:
        args = [x, hist, w["gmix"], w["win"]] + small + [w["gq"], w["gk"], w["gmat"]]
        in_specs = [rows_spec(d_model), hist_spec] + [const(a) for a in args[2:]]
        kv_shape = jax.ShapeDtypeStruct((batch, t_len, ATTN_W), _F32)
        kv_spec = rows_spec(ATTN_W)
        first_layer = False
    out_shape = (
        jax.ShapeDtypeStruct((batch, t_len, CONV_CH), _BF16),
        jax.ShapeDtypeStruct((batch, t_len, ATTN_W), _BF16),
        kv_shape, kv_shape,
        jax.ShapeDtypeStruct((batch, HIST, CONV_CH), _F32),
    )
    return pl.pallas_call(
        functools.partial(_inproj_kernel, n_seg=n_seg, seg_rows=seg_rows, feature_major_kv=feature_major_kv,
                          first_layer=first_layer),
        out_shape=out_shape,
        grid=(batch // n_seg, n_t),
        in_specs=in_specs,
        out_specs=(rows_spec(CONV_CH), rows_spec(ATTN_W), kv_spec, kv_spec, nconv_spec),
        scratch_shapes=[pltpu.VMEM((n_seg, seg_rows + HIST_PAD, CONV_CH), _F32),
                        pltpu.VMEM((SUBLANES - 1, seg_rows + HIST_PAD - SUBLANES, CONV_CH), _F32),
                        pltpu.VMEM((n_seg * seg_rows, CONV_CH), _BF16)],
        input_output_aliases=aliases,
        compiler_params=pltpu.CompilerParams(dimension_semantics=("arbitrary", "arbitrary"),
                                             vmem_limit_bytes=VMEM_LIMIT_BYTES),
        name="inproj_conv",
    )(*args)


def _attn_kernel(*refs, past_chunks0, chunks_per_q, layer, own_feature_major):
    (q_ref, kd_ref, vd_ref, kp1_ref, vp1_ref, kp2_ref, vp2_ref, kany_ref, vany_ref, ud_ref, up_ref,
     o_ref, oacc_ref, run_ref, kbuf_ref, vbuf_ref, sem_ref) = refs
    b = pl.program_id(0)
    i = pl.program_id(1)
    tq = q_ref.shape[0]
    n_pairs = ATTN_W // LANES
    n_past = past_chunks0 + i * chunks_per_q
    even_head = lax.broadcasted_iota(jnp.int32, (tq, LANES), 1) < HEAD_DIM
    pair_sl = [slice(p * LANES, (p + 1) * LANES) for p in range(n_pairs)]

    def stacked_queries():
        qs = []
        for sl in pair_sl:
            qp = q_ref[:, sl]
            qs.append(jnp.concatenate([jnp.where(even_head, qp, 0.0), jnp.where(even_head, 0.0, qp)], axis=0))
        return qs

    def scores_fm(qs, key_refs):
        out = []
        for p, sl in enumerate(pair_sl):
            keys = jnp.concatenate([r[sl, :] for r in key_refs], axis=1).astype(_BF16)
            out.append(_dot(qs[p], keys))
        return jnp.concatenate(out, axis=0)

    def log_terms(z):
        sp = jnp.log(1.0 + jnp.exp(-jnp.abs(z)))
        neg_part = jnp.minimum(z, 0.0)
        return neg_part - sp, (neg_part - z) - sp

    def suffix_sums(log_1m, u_ref):
        hi = log_1m.astype(_BF16)
        lo = (log_1m - hi.astype(_F32)).astype(_BF16)
        r = _dot(jnp.concatenate([hi, lo], axis=1), u_ref[...])
        return r[:, LANES:], r[:, :LANES]

    def weighted_values(fm_weights, fm_value_refs, rm_weight=None, rm_value_ref=None):
        wb = jnp.concatenate(fm_weights, axis=1).astype(_BF16)
        outs = []
        for p, sl in enumerate(pair_sl):
            rows = slice(2 * tq * p, 2 * tq * (p + 1))
            vals = jnp.concatenate([r[sl, :] for r in fm_value_refs], axis=1).astype(_BF16)
            pv = _dot_nt(wb[rows], vals)
            if rm_weight is not None:
                pv = pv + _dot(rm_weight[rows].astype(_BF16), rm_value_ref[:, sl].astype(_BF16))
            outs.append(jnp.where(even_head, pv[:tq], pv[tq:]))
        return jnp.concatenate(outs, axis=1)

    qs = stacked_queries()
    if own_feature_major:
        z = scores_fm(qs, (kp2_ref, kp1_ref, kd_ref))
        z_own = z[:, 2 * KEY_CHUNK:]
    else:
        z = scores_fm(qs, (kp2_ref, kp1_ref))
        z_own = jnp.concatenate([_dot_nt(qs[p], kd_ref[:, sl].astype(_BF16)) for p, sl in enumerate(pair_sl)],
                                axis=0)
    q_row = lax.broadcasted_iota(jnp.int32, (N_HEADS * tq, tq), 0) & (tq - 1)
    causal = lax.broadcasted_iota(jnp.int32, (N_HEADS * tq, tq), 1) < q_row
    z_2 = z[:, 0:KEY_CHUNK] + jnp.where(n_past >= 2, 0.0, MASKED_SCORE)
    z_1 = z[:, KEY_CHUNK:2 * KEY_CHUNK] + jnp.where(n_past >= 1, 0.0, MASKED_SCORE)
    z_d = jnp.where(causal, z_own, MASKED_SCORE)
    lb_d, l1_d = log_terms(z_d)
    lb_1, l1_1 = log_terms(z_1)
    lb_2, l1_2 = log_terms(z_2)
    in_d, tot_d = suffix_sums(l1_d, ud_ref)
    in_1, tot_1 = suffix_sums(l1_1, up_ref)
    in_2, tot_2 = suffix_sums(l1_2, up_ref)
    a_d = jnp.exp(lb_d + in_d)
    a_1 = jnp.exp(lb_1 + (in_1 + tot_d))
    run = tot_d + tot_1
    a_2 = jnp.exp(lb_2 + (in_2 + run))
    run = run + tot_2
    if own_feature_major:
        oacc_ref[...] = weighted_values((a_2, a_1, a_d), (vp2_ref, vp1_ref, vd_ref))
    else:
        oacc_ref[...] = weighted_values((a_2, a_1), (vp2_ref, vp1_ref), a_d, vd_ref)
    run_ref[...] = run

    def more(carry):
        c, m = carry
        return jnp.logical_and(c >= 0, m > STICK_LOG_FLOOR)

    def older(carry):
        c, _ = carry
        start = pl.multiple_of(c * KEY_CHUNK, KEY_CHUNK)
        k_copy = pltpu.make_async_copy(kany_ref.at[layer, b, :, pl.ds(start, KEY_CHUNK)], kbuf_ref, sem_ref.at[0])
        v_copy = pltpu.make_async_copy(vany_ref.at[layer, b, :, pl.ds(start, KEY_CHUNK)], vbuf_ref, sem_ref.at[1])
        k_copy.start()
        v_copy.start()
        k_copy.wait()
        v_copy.wait()
        lb, l1 = log_terms(scores_fm(stacked_queries(), (kbuf_ref,)))
        inside, tot = suffix_sums(l1, up_ref)
        run_c = run_ref[...]
        oacc_ref[...] += weighted_values((jnp.exp(lb + (inside + run_c)),), (vbuf_ref,))
        run_c = run_c + tot
        run_ref[...] = run_c
        return c - 1, jnp.max(run_c)

    lax.while_loop(more, older, (n_past - 3, jnp.max(run)))

    o_ref[...] = oacc_ref[...].astype(o_ref.dtype)


def _attention(q, k_own, v_own, k_past, v_past, *, layer, past_is_own):
    batch, t_len, _ = q.shape
    if past_is_own:
        tq = KEY_CHUNK
        past_chunks0, chunks_per_q = 0, 1
    else:
        tq = t_len
        assert k_past.shape[-1] % KEY_CHUNK == 0
        past_chunks0, chunks_per_q = k_past.shape[-1] // KEY_CHUNK, 0
    assert t_len % tq == 0 and tq <= KEY_CHUNK and tq & (tq - 1) == 0
    n_q = t_len // tq
    assert past_is_own or n_q == 1

    def newer(n):
        u = lax.broadcasted_iota(jnp.int32, (n, n), 0) > lax.broadcasted_iota(jnp.int32, (n, n), 1)
        once = jnp.concatenate([jnp.ones((n, LANES), _BF16), u.astype(_BF16)], axis=1)
        return jnp.concatenate([once, once], axis=0)
    u_diag, u_past = newer(tq), newer(KEY_CHUNK)

    def fm_chunk(back):
        imap = lambda b, i: (layer, b, 0, jnp.maximum(past_chunks0 + i * chunks_per_q - back, 0))
        return pl.BlockSpec((None, None, ATTN_W, KEY_CHUNK), imap)

    rm_block = pl.BlockSpec((None, tq, ATTN_W), lambda b, i: (b, i, 0))
    const = lambda arr: pl.BlockSpec(arr.shape, lambda b, i: (0,) * arr.ndim)
    any_spec = pl.BlockSpec(memory_space=pl.ANY)
    if past_is_own:
        k_own, v_own = k_past, v_past
        own_spec = fm_chunk(0)
    else:
        own_spec = rm_block
    return pl.pallas_call(
        functools.partial(_attn_kernel, past_chunks0=past_chunks0, chunks_per_q=chunks_per_q, layer=layer,
                          own_feature_major=past_is_own),
        out_shape=jax.ShapeDtypeStruct((batch, t_len, ATTN_W), _BF16),
        grid=(batch, n_q),
        in_specs=[rm_block, own_spec, own_spec, fm_chunk(1), fm_chunk(1), fm_chunk(2), fm_chunk(2),
                  any_spec, any_spec, const(u_diag), const(u_past)],
        out_specs=rm_block,
        scratch_shapes=[pltpu.VMEM((tq, ATTN_W), _F32),
                        pltpu.VMEM((N_HEADS * tq, LANES), _F32),
                        pltpu.VMEM((ATTN_W, KEY_CHUNK), _F32),
                        pltpu.VMEM((ATTN_W, KEY_CHUNK), _F32),
                        pltpu.SemaphoreType.DMA((2,))],
        compiler_params=pltpu.CompilerParams(dimension_semantics=("arbitrary", "arbitrary"),
                                             vmem_limit_bytes=VMEM_LIMIT_BYTES),
        name="stick_attention",
    )(q, k_own, v_own, k_past, v_past, k_past, v_past, k_past, v_past, u_diag, u_past)


def _outffn_kernel(x_ref, c_ref, o_ref, wout_ref, gff_ref, w1_ref, w2_ref, y_ref):
    d_ff = w1_ref.shape[1]
    x1 = x_ref[...] + _dot(c_ref[...], wout_ref[0:CONV_CH, :]) + _dot(o_ref[...], wout_ref[CONV_CH:, :])
    h = x1 * lax.rsqrt(jnp.mean(x1 * x1, axis=-1, keepdims=True) + EPS) * gff_ref[...]
    h = h.astype(_BF16)
    ffn = None
    for cc in range(d_ff // FF_CHUNK):
        f = jnp.maximum(_dot(h, w1_ref[:, cc * FF_CHUNK:(cc + 1) * FF_CHUNK]), 0.0)
        part = _dot((f * f).astype(_BF16), w2_ref[cc * FF_CHUNK:(cc + 1) * FF_CHUNK, :])
        ffn = part if ffn is None else ffn + part
    y_ref[...] = x1 + ffn


def _outffn(x, c, o, wout, gff, w1, w2):
    rows, d_model = x.shape
    tb = min(ROW_BLOCK, rows)
    assert rows % tb == 0 and w1.shape[1] % FF_CHUNK == 0
    rows_spec = lambda w: pl.BlockSpec((tb, w), lambda r: (r, 0))
    const = lambda arr: pl.BlockSpec(arr.shape, lambda r: (0,) * arr.ndim, pipeline_mode=pl.Buffered(1))
    return pl.pallas_call(
        _outffn_kernel,
        out_shape=jax.ShapeDtypeStruct((rows, d_model), _F32),
        grid=(rows // tb,),
        in_specs=[rows_spec(d_model), rows_spec(CONV_CH), rows_spec(ATTN_W),
                  const(wout), const(gff), const(w1), const(w2)],
        out_specs=rows_spec(d_model),
        compiler_params=pltpu.CompilerParams(dimension_semantics=("arbitrary",),
                                             vmem_limit_bytes=VMEM_LIMIT_BYTES),
        name="outproj_ffn",
    )(x, c, o, wout, gff, w1, w2)


def _mix_and_ffn(x, c, o, w):
    batch, t_len, d_model = x.shape
    rows = batch * t_len
    y = _outffn(x.reshape(rows, d_model), c.reshape(rows, CONV_CH), o.reshape(rows, ATTN_W),
                w["wout"], w["gff"], w["w1"], w["w2"])
    return y.reshape(batch, t_len, d_model)


def kernel(x_prompt, x_sample, cache_k, cache_v, state_conv, g_mix, w_in, w_dw, b_dw, g_cn, b_cn, w_pw,
           g_q, g_k, w_out, g_ff, w_ff1, w_ff2):
    depth = w_in.shape[0]
    bp, tp, _ = x_prompt.shape
    bs, ts, _ = x_sample.shape
    n_past = cache_k.shape[2]
    head_id = lax.broadcasted_iota(jnp.int32, (ATTN_W, ATTN_W), 0) // HEAD_DIM
    gmat = (head_id == head_id.T).astype(_BF16)
    zero_hist = jnp.zeros((bp, HIST_PAD, CONV_CH), _F32)
    row = lambda a: a.reshape(1, -1)
    to_fm = lambda a: jnp.transpose(a, (0, 1, 3, 4, 2)).reshape(a.shape[0], a.shape[1], ATTN_W, a.shape[2])
    from_fm = lambda a: jnp.transpose(a.reshape(a.shape[0], a.shape[1], N_HEADS, HEAD_DIM, a.shape[3]),
                                      (0, 1, 4, 2, 3))
    cache_k_fm, cache_v_fm = to_fm(cache_k), to_fm(cache_v)

    hp, hs = x_prompt, x_sample
    kv_stack = (None, None)
    conv_p, k_s, v_s, conv_s = [], [], [], []
    for l in range(depth):
        win = w_in[l].astype(_BF16)
        w = dict(gmix=row(g_mix[l]), win=win, w_agq=win[:, :2 * CONV_CH + ATTN_W],
                 w_kv_t=win[:, 2 * CONV_CH + ATTN_W:].T, wdw=w_dw[l], bdw=row(b_dw[l]),
                 gcn=row(g_cn[l]), bcn=row(b_cn[l]), wpw=w_pw[l].astype(_BF16),
                 gq=row(jnp.tile(g_q[l], N_HEADS)), gk=row(jnp.tile(g_k[l], N_HEADS)),
                 gk_col=jnp.broadcast_to(jnp.tile(g_k[l], N_HEADS)[:, None], (ATTN_W, LANES)), gmat=gmat,
                 wout=w_out[l].astype(_BF16), gff=row(g_ff[l]),
                 w1=w_ff1[l].astype(_BF16), w2=w_ff2[l].astype(_BF16))
        c, q, k_fm, v_fm, cv = _inproj(hp, zero_hist, w, n_seg=1, seg_rows=min(ROW_BLOCK, tp),
                                       kv_stack=kv_stack, layer=l, depth=depth)
        kv_stack = (k_fm, v_fm)
        o = _attention(q, None, None, k_fm, v_fm, layer=l, past_is_own=True)
        hp = _mix_and_ffn(hp, c, o, w)
        conv_p.append(cv)
        hist = jnp.pad(state_conv[l], ((0, 0), (HIST_PAD - HIST, 0), (0, 0)))
        c, q, k, v, cv = _inproj(hs, hist, w, n_seg=bs, seg_rows=ts)
        o = _attention(q, k, v, cache_k_fm, cache_v_fm, layer=l, past_is_own=False)
        hs = _mix_and_ffn(hs, c, o, w)
        k_s.append(k.reshape(bs, ts, N_HEADS, HEAD_DIM))
        v_s.append(v.reshape(bs, ts, N_HEADS, HEAD_DIM))
        conv_s.append(cv)
    return (hp, hs, from_fm(kv_stack[0]), from_fm(kv_stack[1]), jnp.stack(conv_p),
            jnp.stack(k_s), jnp.stack(v_s), jnp.stack(conv_s))
```

```python
import functools

import jax
import jax.numpy as jnp
from jax import lax
from jax.experimental import pallas as pl
from jax.experimental.pallas import tpu as pltpu

EPS = 1e-6
CONV_WIDTH = 31
HIST = CONV_WIDTH - 1
HIST_PAD = 32
SUBLANES = 8
N_HEADS = 8
HEAD_DIM = 64
ATTN_W = N_HEADS * HEAD_DIM
CONV_CH = 512
LANES = 128
KEY_CHUNK = 128
CONV_ROWS = 64
FF_CHUNK = 1024
ROW_BLOCK = 512
VMEM_LIMIT_BYTES = 56 * 1024 * 1024
ATTN_BLOCKS_PER_STEP = 2
STICK_LOG_FLOOR = -87.5
MASKED_SCORE = -1e30

_F32 = jnp.float32
_BF16 = jnp.bfloat16


def _dot(a, b):
    return jnp.dot(a, b, preferred_element_type=_F32)


def _dot_nt(a, b):
    return lax.dot_general(a, b, (((1,), (1,)), ((), ())), preferred_element_type=_F32)


def _dot_split2(x, m):
    hi = x.astype(_BF16)
    lo = (x - hi.astype(_F32)).astype(_BF16)
    return _dot(hi, m) + _dot(lo, m)


def _inproj_kernel(*refs, n_seg, seg_rows, feature_major_kv, first_layer):
    if feature_major_kv:
        (x_ref, hist_ref, gmix_ref, win_ref, wkv_ref, wdw_ref, bdw_ref, gcn_ref, bcn_ref, wpw_ref,
         gq_ref, gk_ref, gmat_ref, _, _,
         c_ref, q_ref, k_ref, v_ref, nconv_ref, uext_ref, shift_ref, act_ref) = refs
        if first_layer:
            for stack_ref in (k_ref, v_ref):
                if stack_ref.shape[0] > 1:
                    stack_ref[1:] = jnp.zeros((stack_ref.shape[0] - 1,) + stack_ref.shape[1:], _F32)
            k_ref, v_ref = k_ref.at[0], v_ref.at[0]
    else:
        (x_ref, hist_ref, gmix_ref, win_ref, wdw_ref, bdw_ref, gcn_ref, bcn_ref, wpw_ref,
         gq_ref, gk_ref, gmat_ref,
         c_ref, q_ref, k_ref, v_ref, nconv_ref, uext_ref, shift_ref, act_ref) = refs
    t = pl.program_id(1)
    n_t = pl.num_programs(1)
    rows = n_seg * seg_rows
    d_model = x_ref.shape[-1]

    x = x_ref[...].reshape(rows, d_model)
    h = x * lax.rsqrt(jnp.mean(x * x, axis=-1, keepdims=True) + EPS) * gmix_ref[...]
    h = h.astype(_BF16)

    a = _dot(h, win_ref[:, 0:CONV_CH])
    gate = _dot(h, win_ref[:, CONV_CH:2 * CONV_CH])
    u = a * jax.nn.sigmoid(gate)

    @pl.when(t == 0)
    def _():
        uext_ref[:, 0:HIST_PAD, :] = hist_ref[...]

    uext_ref[:, HIST_PAD:HIST_PAD + seg_rows, :] = u.reshape(n_seg, seg_rows, CONV_CH)

    shift_rows = shift_ref.shape[1]
    for s in range(n_seg):
        for b in range(1, SUBLANES):
            shift_ref[b - 1] = uext_ref[s, b:b + shift_rows, :]
        for r in range(seg_rows // CONV_ROWS):
            r0 = r * CONV_ROWS
            acc = jnp.zeros((CONV_ROWS, CONV_CH), _F32)
            for j in range(CONV_WIDTH):
                off = j + (HIST_PAD - HIST)
                b = off % SUBLANES
                lo = r0 + off - b
                if b == 0:
                    tap = uext_ref[s, lo:lo + CONV_ROWS, :]
                else:
                    tap = shift_ref[b - 1, lo:lo + CONV_ROWS, :]
                acc = acc + wdw_ref[j:j + 1, :] * tap
            cv = acc + bdw_ref[...]
            mu = jnp.mean(cv, axis=-1, keepdims=True)
            xc = cv - mu
            y = xc * lax.rsqrt(jnp.mean(xc * xc, axis=-1, keepdims=True) + EPS)
            y = y * gcn_ref[...] + bcn_ref[...]
            y = y * jax.nn.sigmoid(y)
            act_ref[s * seg_rows + r0:s * seg_rows + r0 + CONV_ROWS, :] = y.astype(_BF16)

    c_ref[...] = _dot(act_ref[...], wpw_ref[...]).astype(_BF16).reshape(c_ref.shape)

    @pl.when(t == n_t - 1)
    def _():
        nconv_ref[...] = uext_ref[:, seg_rows + (HIST_PAD - HIST):seg_rows + HIST_PAD, :]

    uext_ref[:, 0:HIST_PAD, :] = uext_ref[:, seg_rows:seg_rows + HIST_PAD, :]

    def head_norm(z, g_ref):
        ssq = _dot_split2(z * z, gmat_ref[...])
        return z * lax.rsqrt(ssq * (1.0 / HEAD_DIM) + EPS) * g_ref[...]

    q = _dot(h, win_ref[:, 2 * CONV_CH:2 * CONV_CH + ATTN_W])
    q_ref[...] = (head_norm(q, gq_ref) * (HEAD_DIM ** -0.5)).astype(_BF16).reshape(q_ref.shape)
    if feature_major_kv:
        k_t = _dot_nt(wkv_ref[0:ATTN_W, :], h)
        gain = jnp.concatenate([gk_ref[...]] * (rows // LANES), axis=1)
        for hh in range(N_HEADS):
            rs = slice(hh * HEAD_DIM, (hh + 1) * HEAD_DIM)
            kh = k_t[rs]
            ssq = jnp.sum(kh * kh, axis=0, keepdims=True)
            k_ref[rs, :] = kh * lax.rsqrt(ssq * (1.0 / HEAD_DIM) + EPS) * gain[rs]
        v_ref[...] = _dot_nt(wkv_ref[ATTN_W:, :], h)
    else:
        k = _dot(h, win_ref[:, 2 * CONV_CH + ATTN_W:2 * CONV_CH + 2 * ATTN_W])
        k_ref[...] = head_norm(k, gk_ref).reshape(k_ref.shape)
        v_ref[...] = _dot(h, win_ref[:, 2 * CONV_CH + 2 * ATTN_W:]).reshape(v_ref.shape)


def _inproj(x, hist, w, *, n_seg, seg_rows, kv_stack=None, layer=0, depth=1):
    batch, t_len, d_model = x.shape
    assert batch % n_seg == 0 and t_len % seg_rows == 0 and seg_rows % CONV_ROWS == 0
    n_t = t_len // seg_rows
    assert n_seg == 1 or n_t == 1
    feature_major_kv = kv_stack is not None
    rows_spec = lambda width: pl.BlockSpec((n_seg, seg_rows, width), lambda b, t: (b, t, 0))
    const = lambda arr: pl.BlockSpec(arr.shape, lambda b, t: (0,) * arr.ndim)
    hist_spec = pl.BlockSpec((n_seg, HIST_PAD, CONV_CH), lambda b, t: (b, 0, 0))
    nconv_spec = pl.BlockSpec((n_seg, HIST, CONV_CH), lambda b, t: (b, 0, 0))
    small = [w["wdw"], w["bdw"], w["gcn"], w["bcn"], w["wpw"]]
    aliases = {}
    if feature_major_kv:
        assert n_seg == 1 and seg_rows % LANES == 0
        args = [x, hist, w["gmix"], w["w_agq"], w["w_kv_t"]] + small + [w["gq"], w["gk_col"], w["gmat"]]
        in_specs = [rows_spec(d_model), hist_spec] + [const(a) for a in args[2:]]
        kv_shape = jax.ShapeDtypeStruct((depth, batch, ATTN_W, t_len), _F32)
        first_layer = kv_stack[0] is None
        if first_layer:
            kv_spec = pl.BlockSpec((depth, None, ATTN_W, seg_rows), lambda b, t: (0, b, 0, t))
            dummy = jnp.zeros((SUBLANES, LANES), _F32)
            args += [dummy, dummy]
            in_specs += [const(dummy)] * 2
        else:
            kv_spec = pl.BlockSpec((None, None, ATTN_W, seg_rows), lambda b, t: (layer, b, 0, t))
            aliases = {len(args): 2, len(args) + 1: 3}
            args += list(kv_stack)
            in_specs += [pl.BlockSpec(memory_space=pl.ANY)] * 2
    else:
        args = [x, hist, w["gmix"], w["win"]] + small + [w["gq"], w["gk"], w["gmat"]]
        in_specs = [rows_spec(d_model), hist_spec] + [const(a) for a in args[2:]]
        kv_shape = jax.ShapeDtypeStruct((batch, t_len, ATTN_W), _F32)
        kv_spec = rows_spec(ATTN_W)
        first_layer = False
    out_shape = (
        jax.ShapeDtypeStruct((batch, t_len, CONV_CH), _BF16),
        jax.ShapeDtypeStruct((batch, t_len, ATTN_W), _BF16),
        kv_shape, kv_shape,
        jax.ShapeDtypeStruct((batch, HIST, CONV_CH), _F32),
    )
    return pl.pallas_call(
        functools.partial(_inproj_kernel, n_seg=n_seg, seg_rows=seg_rows, feature_major_kv=feature_major_kv,
                          first_layer=first_layer),
        out_shape=out_shape,
        grid=(batch // n_seg, n_t),
        in_specs=in_specs,
        out_specs=(rows_spec(CONV_CH), rows_spec(ATTN_W), kv_spec, kv_spec, nconv_spec),
        scratch_shapes=[pltpu.VMEM((n_seg, seg_rows + HIST_PAD, CONV_CH), _F32),
                        pltpu.VMEM((SUBLANES - 1, seg_rows + HIST_PAD - SUBLANES, CONV_CH), _F32),
                        pltpu.VMEM((n_seg * seg_rows, CONV_CH), _BF16)],
        input_output_aliases=aliases,
        compiler_params=pltpu.CompilerParams(dimension_semantics=("arbitrary", "arbitrary"),
                                             vmem_limit_bytes=VMEM_LIMIT_BYTES),
        name="inproj_conv",
    )(*args)


def _attn_kernel(*refs, n_sub, past_chunks0, chunks_per_step, layer, own_feature_major):
    n_fm = n_sub + 2 if own_feature_major else 2
    q_ref, k_fm, v_fm, rest = refs[0], refs[1:1 + n_fm], refs[1 + n_fm:1 + 2 * n_fm], refs[1 + 2 * n_fm:]
    if not own_feature_major:
        kd_ref, vd_ref, rest = rest[0], rest[1], rest[2:]
    kany_ref, vany_ref, ud_ref, up_ref, o_ref, oacc_ref, run_ref, kbuf_ref, vbuf_ref, sem_ref = rest
    b = pl.program_id(0)
    i = pl.program_id(1)
    tq = q_ref.shape[0] // n_sub
    n_pairs = ATTN_W // LANES
    subs = range(n_sub)
    n_past = [past_chunks0 + i * chunks_per_step + j for j in subs]
    even_head = lax.broadcasted_iota(jnp.int32, (tq, LANES), 1) < HEAD_DIM
    pair_sl = [slice(p * LANES, (p + 1) * LANES) for p in range(n_pairs)]

    def stacked_queries(j):
        qs = []
        for sl in pair_sl:
            qp = q_ref[j * tq:(j + 1) * tq, sl]
            qs.append(jnp.concatenate([jnp.where(even_head, qp, 0.0), jnp.where(even_head, 0.0, qp)], axis=0))
        return qs

    def scores_fm(qs, key_refs):
        out = []
        for p, sl in enumerate(pair_sl):
            keys = jnp.concatenate([r[sl, :] for r in key_refs], axis=1).astype(_BF16)
            out.append(_dot(qs[p], keys))
        return jnp.concatenate(out, axis=0)

    def log_terms(z):
        sp = jnp.log(1.0 + jnp.exp(-jnp.abs(z)))
        neg_part = jnp.minimum(z, 0.0)
        return neg_part - sp, (neg_part - z) - sp

    def suffix_sums(log_1m, u_ref):
        hi = log_1m.astype(_BF16)
        lo = (log_1m - hi.astype(_F32)).astype(_BF16)
        r = _dot(jnp.concatenate([hi, lo], axis=1), u_ref[...])
        return r[:, LANES:], r[:, :LANES]

    def weighted_values(fm_weights, fm_value_refs, rm_weight=None, rm_value_ref=None):
        wb = jnp.concatenate(fm_weights, axis=1).astype(_BF16)
        outs = []
        for p, sl in enumerate(pair_sl):
            rows = slice(2 * tq * p, 2 * tq * (p + 1))
            vals = jnp.concatenate([r[sl, :] for r in fm_value_refs], axis=1).astype(_BF16)
            pv = _dot_nt(wb[rows], vals)
            if rm_weight is not None:
                pv = pv + _dot(rm_weight[rows].astype(_BF16), rm_value_ref[:, sl].astype(_BF16))
            outs.append(jnp.where(even_head, pv[:tq], pv[tq:]))
        return jnp.concatenate(outs, axis=1)

    qs = [stacked_queries(j) for j in subs]
    if own_feature_major:
        z = [scores_fm(qs[j], k_fm[j:j + 3]) for j in subs]
        z_own = [zj[:, 2 * KEY_CHUNK:] for zj in z]
    else:
        z = [scores_fm(qs[0], k_fm)]
        z_own = [jnp.concatenate([_dot_nt(qs[0][p], kd_ref[:, sl].astype(_BF16))
                                  for p, sl in enumerate(pair_sl)], axis=0)]
    q_row = lax.broadcasted_iota(jnp.int32, (N_HEADS * tq, tq), 0) & (tq - 1)
    causal = lax.broadcasted_iota(jnp.int32, (N_HEADS * tq, tq), 1) < q_row
    z_2 = [z[j][:, 0:KEY_CHUNK] + jnp.where(n_past[j] >= 2, 0.0, MASKED_SCORE) for j in subs]
    z_1 = [z[j][:, KEY_CHUNK:2 * KEY_CHUNK] + jnp.where(n_past[j] >= 1, 0.0, MASKED_SCORE) for j in subs]
    z_d = [jnp.where(causal, z_own[j], MASKED_SCORE) for j in subs]
    t_d = [log_terms(zz) for zz in z_d]
    t_1 = [log_terms(zz) for zz in z_1]
    t_2 = [log_terms(zz) for zz in z_2]
    s_d = [suffix_sums(t_d[j][1], ud_ref) for j in subs]
    s_1 = [suffix_sums(t_1[j][1], up_ref) for j in subs]
    s_2 = [suffix_sums(t_2[j][1], up_ref) for j in subs]
    a_d = [jnp.exp(t_d[j][0] + s_d[j][0]) for j in subs]
    a_1 = [jnp.exp(t_1[j][0] + (s_1[j][0] + s_d[j][1])) for j in subs]
    run_1 = [s_d[j][1] + s_1[j][1] for j in subs]
    a_2 = [jnp.exp(t_2[j][0] + (s_2[j][0] + run_1[j])) for j in subs]
    run = [run_1[j] + s_2[j][1] for j in subs]
    for j in subs:
        if own_feature_major:
            out = weighted_values((a_2[j], a_1[j], a_d[j]), v_fm[j:j + 3])
        else:
            out = weighted_values((a_2[j], a_1[j]), v_fm, a_d[j], vd_ref)
        oacc_ref[j * tq:(j + 1) * tq, :] = out
        run_ref[j] = run[j]

    def more(carry):
        c, m = carry
        return jnp.logical_and(c >= 0, m > STICK_LOG_FLOOR)

    def older(j, carry):
        c, _ = carry
        start = pl.multiple_of(c * KEY_CHUNK, KEY_CHUNK)
        k_copy = pltpu.make_async_copy(kany_ref.at[layer, b, :, pl.ds(start, KEY_CHUNK)], kbuf_ref, sem_ref.at[0])
        v_copy = pltpu.make_async_copy(vany_ref.at[layer, b, :, pl.ds(start, KEY_CHUNK)], vbuf_ref, sem_ref.at[1])
        k_copy.start()
        v_copy.start()
        k_copy.wait()
        v_copy.wait()
        lb, l1 = log_terms(scores_fm(stacked_queries(j), (kbuf_ref,)))
        inside, tot = suffix_sums(l1, up_ref)
        run_c = run_ref[j]
        oacc_ref[j * tq:(j + 1) * tq, :] += weighted_values((jnp.exp(lb + (inside + run_c)),), (vbuf_ref,))
        run_c = run_c + tot
        run_ref[j] = run_c
        return c - 1, jnp.max(run_c)

    for j in subs:
        lax.while_loop(more, functools.partial(older, j), (n_past[j] - 3, jnp.max(run[j])))

    o_ref[...] = oacc_ref[...].astype(o_ref.dtype)


def _attention(q, k_own, v_own, k_past, v_past, *, layer, past_is_own):
    batch, t_len, _ = q.shape
    if past_is_own:
        tq = KEY_CHUNK
        n_sub = ATTN_BLOCKS_PER_STEP if t_len % (ATTN_BLOCKS_PER_STEP * tq) == 0 else 1
        past_chunks0, chunks_per_step, n_fm = 0, n_sub, n_sub + 2
    else:
        tq, n_sub = t_len, 1
        assert k_past.shape[-1] % KEY_CHUNK == 0
        past_chunks0, chunks_per_step, n_fm = k_past.shape[-1] // KEY_CHUNK, 0, 2
    assert t_len % (n_sub * tq) == 0 and tq <= KEY_CHUNK and tq & (tq - 1) == 0
    n_steps = t_len // (n_sub * tq)
    assert past_is_own or n_steps == 1

    def newer(n):
        u = lax.broadcasted_iota(jnp.int32, (n, n), 0) > lax.broadcasted_iota(jnp.int32, (n, n), 1)
        once = jnp.concatenate([jnp.ones((n, LANES), _BF16), u.astype(_BF16)], axis=1)
        return jnp.concatenate([once, once], axis=0)
    u_diag, u_past = newer(tq), newer(KEY_CHUNK)

    def fm_chunk(m):
        imap = lambda b, i: (layer, b, 0, jnp.maximum(past_chunks0 + i * chunks_per_step + m - 2, 0))
        return pl.BlockSpec((None, None, ATTN_W, KEY_CHUNK), imap)

    rm_block = pl.BlockSpec((None, n_sub * tq, ATTN_W), lambda b, i: (b, i, 0))
    const = lambda arr: pl.BlockSpec(arr.shape, lambda b, i: (0,) * arr.ndim)
    any_spec = pl.BlockSpec(memory_space=pl.ANY)
    fm_specs = [fm_chunk(m) for m in range(n_fm)]
    args = [q] + [k_past] * n_fm + [v_past] * n_fm
    in_specs = [rm_block] + fm_specs + fm_specs
    if not past_is_own:
        args += [k_own, v_own]
        in_specs += [rm_block, rm_block]
    args += [k_past, v_past, u_diag, u_past]
    in_specs += [any_spec, any_spec, const(u_diag), const(u_past)]
    return pl.pallas_call(
        functools.partial(_attn_kernel, n_sub=n_sub, past_chunks0=past_chunks0, chunks_per_step=chunks_per_step,
                          layer=layer, own_feature_major=past_is_own),
        out_shape=jax.ShapeDtypeStruct((batch, t_len, ATTN_W), _BF16),
        grid=(batch, n_steps),
        in_specs=in_specs,
        out_specs=rm_block,
        scratch_shapes=[pltpu.VMEM((n_sub * tq, ATTN_W), _F32),
                        pltpu.VMEM((n_sub, N_HEADS * tq, LANES), _F32),
                        pltpu.VMEM((ATTN_W, KEY_CHUNK), _F32),
                        pltpu.VMEM((ATTN_W, KEY_CHUNK), _F32),
                        pltpu.SemaphoreType.DMA((2,))],
        compiler_params=pltpu.CompilerParams(dimension_semantics=("arbitrary", "arbitrary"),
                                             vmem_limit_bytes=VMEM_LIMIT_BYTES),
        name="stick_attention",
    )(*args)


def _outffn_kernel(x_ref, c_ref, o_ref, wout_ref, gff_ref, w1_ref, w2_ref, y_ref):
    d_ff = w1_ref.shape[1]
    x1 = x_ref[...] + _dot(c_ref[...], wout_ref[0:CONV_CH, :]) + _dot(o_ref[...], wout_ref[CONV_CH:, :])
    h = x1 * lax.rsqrt(jnp.mean(x1 * x1, axis=-1, keepdims=True) + EPS) * gff_ref[...]
    h = h.astype(_BF16)
    ffn = None
    for cc in range(d_ff // FF_CHUNK):
        f = jnp.maximum(_dot(h, w1_ref[:, cc * FF_CHUNK:(cc + 1) * FF_CHUNK]), 0.0)
        part = _dot((f * f).astype(_BF16), w2_ref[cc * FF_CHUNK:(cc + 1) * FF_CHUNK, :])
        ffn = part if ffn is None else ffn + part
    y_ref[...] = x1 + ffn


def _outffn(x, c, o, wout, gff, w1, w2):
    rows, d_model = x.shape
    tb = min(ROW_BLOCK, rows)
    assert rows % tb == 0 and w1.shape[1] % FF_CHUNK == 0
    rows_spec = lambda w: pl.BlockSpec((tb, w), lambda r: (r, 0))
    const = lambda arr: pl.BlockSpec(arr.shape, lambda r: (0,) * arr.ndim, pipeline_mode=pl.Buffered(1))
    return pl.pallas_call(
        _outffn_kernel,
        out_shape=jax.ShapeDtypeStruct((rows, d_model), _F32),
        grid=(rows // tb,),
        in_specs=[rows_spec(d_model), rows_spec(CONV_CH), rows_spec(ATTN_W),
                  const(wout), const(gff), const(w1), const(w2)],
        out_specs=rows_spec(d_model),
        compiler_params=pltpu.CompilerParams(dimension_semantics=("arbitrary",),
                                             vmem_limit_bytes=VMEM_LIMIT_BYTES),
        name="outproj_ffn",
    )(x, c, o, wout, gff, w1, w2)


def _mix_and_ffn(x, c, o, w):
    batch, t_len, d_model = x.shape
    rows = batch * t_len
    y = _outffn(x.reshape(rows, d_model), c.reshape(rows, CONV_CH), o.reshape(rows, ATTN_W),
                w["wout"], w["gff"], w["w1"], w["w2"])
    return y.reshape(batch, t_len, d_model)


def kernel(x_prompt, x_sample, cache_k, cache_v, state_conv, g_mix, w_in, w_dw, b_dw, g_cn, b_cn, w_pw,
           g_q, g_k, w_out, g_ff, w_ff1, w_ff2):
    depth = w_in.shape[0]
    bp, tp, _ = x_prompt.shape
    bs, ts, _ = x_sample.shape
    n_past = cache_k.shape[2]
    head_id = lax.broadcasted_iota(jnp.int32, (ATTN_W, ATTN_W), 0) // HEAD_DIM
    gmat = (head_id == head_id.T).astype(_BF16)
    zero_hist = jnp.zeros((bp, HIST_PAD, CONV_CH), _F32)
    row = lambda a: a.reshape(1, -1)
    to_fm = lambda a: jnp.transpose(a, (0, 1, 3, 4, 2)).reshape(a.shape[0], a.shape[1], ATTN_W, a.shape[2])
    from_fm = lambda a: jnp.transpose(a.reshape(a.shape[0], a.shape[1], N_HEADS, HEAD_DIM, a.shape[3]),
                                      (0, 1, 4, 2, 3))
    cache_k_fm, cache_v_fm = to_fm(cache_k), to_fm(cache_v)

    hp, hs = x_prompt, x_sample
    kv_stack = (None, None)
    conv_p, k_s, v_s, conv_s = [], [], [], []
    for l in range(depth):
        win = w_in[l].astype(_BF16)
        w = dict(gmix=row(g_mix[l]), win=win, w_agq=win[:, :2 * CONV_CH + ATTN_W],
                 w_kv_t=win[:, 2 * CONV_CH + ATTN_W:].T, wdw=w_dw[l], bdw=row(b_dw[l]),
                 gcn=row(g_cn[l]), bcn=row(b_cn[l]), wpw=w_pw[l].astype(_BF16),
                 gq=row(jnp.tile(g_q[l], N_HEADS)), gk=row(jnp.tile(g_k[l], N_HEADS)),
                 gk_col=jnp.broadcast_to(jnp.tile(g_k[l], N_HEADS)[:, None], (ATTN_W, LANES)), gmat=gmat,
                 wout=w_out[l].astype(_BF16), gff=row(g_ff[l]),
                 w1=w_ff1[l].astype(_BF16), w2=w_ff2[l].astype(_BF16))
        c, q, k_fm, v_fm, cv = _inproj(hp, zero_hist, w, n_seg=1, seg_rows=min(ROW_BLOCK, tp),
                                       kv_stack=kv_stack, layer=l, depth=depth)
        kv_stack = (k_fm, v_fm)
        o = _attention(q, None, None, k_fm, v_fm, layer=l, past_is_own=True)
        hp = _mix_and_ffn(hp, c, o, w)
        conv_p.append(cv)
        hist = jnp.pad(state_conv[l], ((0, 0), (HIST_PAD - HIST, 0), (0, 0)))
        c, q, k, v, cv = _inproj(hs, hist, w, n_seg=bs, seg_rows=ts)
        o = _attention(q, k, v, cache_k_fm, cache_v_fm, layer=l, past_is_own=False)
        hs = _mix_and_ffn(hs, c, o, w)
        k_s.append(k.reshape(bs, ts, N_HEADS, HEAD_DIM))
        v_s.append(v.reshape(bs, ts, N_HEADS, HEAD_DIM))
        conv_s.append(cv)
    return (hp, hs, from_fm(kv_stack[0]), from_fm(kv_stack[1]), jnp.stack(conv_p),
            jnp.stack(k_s), jnp.stack(v_s), jnp.stack(conv_s))
```

```python
import functools

import jax
import jax.numpy as jnp
from jax import lax
from jax.experimental import pallas as pl
from jax.experimental.pallas import tpu as pltpu

EPS = 1e-6
CONV_WIDTH = 31
HIST = CONV_WIDTH - 1
HIST_PAD = 32
SUBLANES = 8
N_HEADS = 8
HEAD_DIM = 64
ATTN_W = N_HEADS * HEAD_DIM
CONV_CH = 512
LANES = 128
KEY_CHUNK = 128
CONV_ROWS = 64
FF_CHUNK = 1024
ROW_BLOCK = 512
VMEM_LIMIT_BYTES = 56 * 1024 * 1024
ATTN_BLOCKS_PER_STEP = 4
STICK_LOG_FLOOR = -87.5
MASKED_SCORE = -1e30

_F32 = jnp.float32
_BF16 = jnp.bfloat16


def _dot(a, b):
    return jnp.dot(a, b, preferred_element_type=_F32)


def _dot_nt(a, b):
    return lax.dot_general(a, b, (((1,), (1,)), ((), ())), preferred_element_type=_F32)


def _dot_split2(x, m):
    hi = x.astype(_BF16)
    lo = (x - hi.astype(_F32)).astype(_BF16)
    return _dot(hi, m) + _dot(lo, m)


def _inproj_kernel(*refs, n_seg, seg_rows, feature_major_kv, first_layer):
    if feature_major_kv:
        (x_ref, hist_ref, gmix_ref, win_ref, wkv_ref, wdw_ref, bdw_ref, gcn_ref, bcn_ref, wpw_ref,
         gq_ref, gk_ref, gmat_ref, _, _,
         c_ref, q_ref, k_ref, v_ref, nconv_ref, uext_ref, shift_ref, act_ref) = refs
        if first_layer:
            for stack_ref in (k_ref, v_ref):
                if stack_ref.shape[0] > 1:
                    stack_ref[1:] = jnp.zeros((stack_ref.shape[0] - 1,) + stack_ref.shape[1:], _F32)
            k_ref, v_ref = k_ref.at[0], v_ref.at[0]
    else:
        (x_ref, hist_ref, gmix_ref, win_ref, wdw_ref, bdw_ref, gcn_ref, bcn_ref, wpw_ref,
         gq_ref, gk_ref, gmat_ref,
         c_ref, q_ref, k_ref, v_ref, nconv_ref, uext_ref, shift_ref, act_ref) = refs
    t = pl.program_id(1)
    n_t = pl.num_programs(1)
    rows = n_seg * seg_rows
    d_model = x_ref.shape[-1]

    x = x_ref[...].reshape(rows, d_model)
    h = x * lax.rsqrt(jnp.mean(x * x, axis=-1, keepdims=True) + EPS) * gmix_ref[...]
    h = h.astype(_BF16)

    a = _dot(h, win_ref[:, 0:CONV_CH])
    gate = _dot(h, win_ref[:, CONV_CH:2 * CONV_CH])
    u = a * jax.nn.sigmoid(gate)

    @pl.when(t == 0)
    def _():
        uext_ref[:, 0:HIST_PAD, :] = hist_ref[...]

    uext_ref[:, HIST_PAD:HIST_PAD + seg_rows, :] = u.reshape(n_seg, seg_rows, CONV_CH)

    shift_rows = shift_ref.shape[1]
    for s in range(n_seg):
        for b in range(1, SUBLANES):
            shift_ref[b - 1] = uext_ref[s, b:b + shift_rows, :]
        for r in range(seg_rows // CONV_ROWS):
            r0 = r * CONV_ROWS
            acc = jnp.zeros((CONV_ROWS, CONV_CH), _F32)
            for j in range(CONV_WIDTH):
                off = j + (HIST_PAD - HIST)
                b = off % SUBLANES
                lo = r0 + off - b
                if b == 0:
                    tap = uext_ref[s, lo:lo + CONV_ROWS, :]
                else:
                    tap = shift_ref[b - 1, lo:lo + CONV_ROWS, :]
                acc = acc + wdw_ref[j:j + 1, :] * tap
            cv = acc + bdw_ref[...]
            mu = jnp.mean(cv, axis=-1, keepdims=True)
            xc = cv - mu
            y = xc * lax.rsqrt(jnp.mean(xc * xc, axis=-1, keepdims=True) + EPS)
            y = y * gcn_ref[...] + bcn_ref[...]
            y = y * jax.nn.sigmoid(y)
            act_ref[s * seg_rows + r0:s * seg_rows + r0 + CONV_ROWS, :] = y.astype(_BF16)

    c_ref[...] = _dot(act_ref[...], wpw_ref[...]).astype(_BF16).reshape(c_ref.shape)

    @pl.when(t == n_t - 1)
    def _():
        nconv_ref[...] = uext_ref[:, seg_rows + (HIST_PAD - HIST):seg_rows + HIST_PAD, :]

    uext_ref[:, 0:HIST_PAD, :] = uext_ref[:, seg_rows:seg_rows + HIST_PAD, :]

    def head_norm(z, g_ref):
        ssq = _dot_split2(z * z, gmat_ref[...])
        return z * lax.rsqrt(ssq * (1.0 / HEAD_DIM) + EPS) * g_ref[...]

    q = _dot(h, win_ref[:, 2 * CONV_CH:2 * CONV_CH + ATTN_W])
    ssq_q = _dot((q * q).astype(_BF16), gmat_ref[...])
    q = q * lax.rsqrt(ssq_q * (1.0 / HEAD_DIM) + EPS) * gq_ref[...]
    q_ref[...] = (q * (HEAD_DIM ** -0.5)).astype(_BF16).reshape(q_ref.shape)
    if feature_major_kv:
        k_t = _dot_nt(wkv_ref[0:ATTN_W, :], h)
        gain = jnp.concatenate([gk_ref[...]] * (rows // LANES), axis=1)
        for hh in range(N_HEADS):
            rs = slice(hh * HEAD_DIM, (hh + 1) * HEAD_DIM)
            kh = k_t[rs]
            ssq = jnp.sum(kh * kh, axis=0, keepdims=True)
            k_ref[rs, :] = kh * lax.rsqrt(ssq * (1.0 / HEAD_DIM) + EPS) * gain[rs]
        v_ref[...] = _dot_nt(wkv_ref[ATTN_W:, :], h)
    else:
        k = _dot(h, win_ref[:, 2 * CONV_CH + ATTN_W:2 * CONV_CH + 2 * ATTN_W])
        k_ref[...] = head_norm(k, gk_ref).reshape(k_ref.shape)
        v_ref[...] = _dot(h, win_ref[:, 2 * CONV_CH + 2 * ATTN_W:]).reshape(v_ref.shape)


def _inproj(x, hist, hist_layer, w, *, layer, n_seg, seg_rows, kv_stack=None):
    batch, t_len, d_model = x.shape
    depth = w["win"].shape[0]
    assert batch % n_seg == 0 and t_len % seg_rows == 0 and seg_rows % CONV_ROWS == 0
    n_t = t_len // seg_rows
    assert n_seg == 1 or n_t == 1
    feature_major_kv = kv_stack is not None
    rows_spec = lambda width: pl.BlockSpec((n_seg, seg_rows, width), lambda b, t: (b, t, 0))
    shared = lambda arr: pl.BlockSpec(arr.shape, lambda b, t: (0,) * arr.ndim)
    const = lambda arr: pl.BlockSpec((None,) + arr.shape[1:], lambda b, t: (layer,) + (0,) * (arr.ndim - 1))
    hist_spec = pl.BlockSpec((None, n_seg, HIST_PAD, CONV_CH), lambda b, t: (hist_layer, b, 0, 0))
    nconv_spec = pl.BlockSpec((n_seg, HIST, CONV_CH), lambda b, t: (b, 0, 0))
    small = [w["wdw"], w["bdw"], w["gcn"], w["bcn"], w["wpw"]]
    aliases = {}
    if feature_major_kv:
        assert n_seg == 1 and seg_rows % LANES == 0
        args = [x, hist, w["gmix"], w["win"], w["w_kv_t"]] + small + [w["gq"], w["gk_col"], w["gmat"]]
        in_specs = [rows_spec(d_model), hist_spec] + [const(a) for a in args[2:-1]] + [shared(w["gmat"])]
        kv_shape = jax.ShapeDtypeStruct((depth, batch, ATTN_W, t_len), _F32)
        first_layer = kv_stack[0] is None
        if first_layer:
            kv_spec = pl.BlockSpec((depth, None, ATTN_W, seg_rows), lambda b, t: (0, b, 0, t))
            dummy = jnp.zeros((SUBLANES, LANES), _F32)
            args += [dummy, dummy]
            in_specs += [shared(dummy)] * 2
        else:
            kv_spec = pl.BlockSpec((None, None, ATTN_W, seg_rows), lambda b, t: (layer, b, 0, t))
            aliases = {len(args): 2, len(args) + 1: 3}
            args += list(kv_stack)
            in_specs += [pl.BlockSpec(memory_space=pl.ANY)] * 2
    else:
        args = [x, hist, w["gmix"], w["win"]] + small + [w["gq"], w["gk"], w["gmat"]]
        in_specs = [rows_spec(d_model), hist_spec] + [const(a) for a in args[2:-1]] + [shared(w["gmat"])]
        kv_shape = jax.ShapeDtypeStruct((batch, t_len, ATTN_W), _F32)
        kv_spec = rows_spec(ATTN_W)
        first_layer = False
    out_shape = (
        jax.ShapeDtypeStruct((batch, t_len, CONV_CH), _BF16),
        jax.ShapeDtypeStruct((batch, t_len, ATTN_W), _BF16),
        kv_shape, kv_shape,
        jax.ShapeDtypeStruct((batch, HIST, CONV_CH), _F32),
    )
    return pl.pallas_call(
        functools.partial(_inproj_kernel, n_seg=n_seg, seg_rows=seg_rows, feature_major_kv=feature_major_kv,
                          first_layer=first_layer),
        out_shape=out_shape,
        grid=(batch // n_seg, n_t),
        in_specs=in_specs,
        out_specs=(rows_spec(CONV_CH), rows_spec(ATTN_W), kv_spec, kv_spec, nconv_spec),
        scratch_shapes=[pltpu.VMEM((n_seg, seg_rows + HIST_PAD, CONV_CH), _F32),
                        pltpu.VMEM((SUBLANES - 1, seg_rows + HIST_PAD - SUBLANES, CONV_CH), _F32),
                        pltpu.VMEM((n_seg * seg_rows, CONV_CH), _BF16)],
        input_output_aliases=aliases,
        compiler_params=pltpu.CompilerParams(dimension_semantics=("arbitrary", "arbitrary"),
                                             vmem_limit_bytes=VMEM_LIMIT_BYTES),
        name="inproj_conv",
    )(*args)


def _attn_kernel(*refs, n_sub, past_chunks0, chunks_per_step, layer, own_feature_major):
    n_fm = n_sub + 2 if own_feature_major else 2
    q_ref, k_fm, v_fm, rest = refs[0], refs[1:1 + n_fm], refs[1 + n_fm:1 + 2 * n_fm], refs[1 + 2 * n_fm:]
    if not own_feature_major:
        kd_ref, vd_ref, rest = rest[0], rest[1], rest[2:]
    kany_ref, vany_ref, ud_ref, up_ref, o_ref, oacc_ref, run_ref, kbuf_ref, vbuf_ref, sem_ref = rest
    b = pl.program_id(0)
    i = pl.program_id(1)
    tq = q_ref.shape[0] // n_sub
    n_pairs = ATTN_W // LANES
    subs = range(n_sub)
    n_past = [past_chunks0 + i * chunks_per_step + j for j in subs]
    even_head = lax.broadcasted_iota(jnp.int32, (tq, LANES), 1) < HEAD_DIM
    pair_sl = [slice(p * LANES, (p + 1) * LANES) for p in range(n_pairs)]

    def stacked_queries(j):
        qs = []
        for sl in pair_sl:
            qp = q_ref[j * tq:(j + 1) * tq, sl]
            qs.append(jnp.concatenate([jnp.where(even_head, qp, 0.0), jnp.where(even_head, 0.0, qp)], axis=0))
        return qs

    def scores_fm(qs, key_refs):
        out = []
        for p, sl in enumerate(pair_sl):
            keys = jnp.concatenate([r[sl, :] for r in key_refs], axis=1).astype(_BF16)
            out.append(_dot(qs[p], keys))
        return jnp.concatenate(out, axis=0)

    def log_terms(z):
        sp = jnp.log(1.0 + jnp.exp(-jnp.abs(z)))
        neg_part = jnp.minimum(z, 0.0)
        return neg_part - sp, (neg_part - z) - sp

    def suffix_sums(log_1m, u_ref):
        hi = log_1m.astype(_BF16)
        lo = (log_1m - hi.astype(_F32)).astype(_BF16)
        r = _dot(jnp.concatenate([hi, lo], axis=1), u_ref[...])
        return r[:, LANES:], r[:, :LANES]

    def weighted_values(fm_weights, fm_value_refs, rm_weight=None, rm_value_ref=None):
        wb = jnp.concatenate(fm_weights, axis=1).astype(_BF16)
        outs = []
        for p, sl in enumerate(pair_sl):
            rows = slice(2 * tq * p, 2 * tq * (p + 1))
            vals = jnp.concatenate([r[sl, :] for r in fm_value_refs], axis=1).astype(_BF16)
            pv = _dot_nt(wb[rows], vals)
            if rm_weight is not None:
                pv = pv + _dot(rm_weight[rows].astype(_BF16), rm_value_ref[:, sl].astype(_BF16))
            outs.append(jnp.where(even_head, pv[:tq], pv[tq:]))
        return jnp.concatenate(outs, axis=1)

    qs = [stacked_queries(j) for j in subs]
    if own_feature_major:
        z = [scores_fm(qs[j], k_fm[j:j + 3]) for j in subs]
        z_own = [zj[:, 2 * KEY_CHUNK:] for zj in z]
    else:
        z = [scores_fm(qs[0], k_fm)]
        z_own = [jnp.concatenate([_dot_nt(qs[0][p], kd_ref[:, sl].astype(_BF16))
                                  for p, sl in enumerate(pair_sl)], axis=0)]
    q_row = lax.broadcasted_iota(jnp.int32, (N_HEADS * tq, tq), 0) & (tq - 1)
    causal = lax.broadcasted_iota(jnp.int32, (N_HEADS * tq, tq), 1) < q_row
    z_2 = [z[j][:, 0:KEY_CHUNK] + jnp.where(n_past[j] >= 2, 0.0, MASKED_SCORE) for j in subs]
    z_1 = [z[j][:, KEY_CHUNK:2 * KEY_CHUNK] + jnp.where(n_past[j] >= 1, 0.0, MASKED_SCORE) for j in subs]
    z_d = [jnp.where(causal, z_own[j], MASKED_SCORE) for j in subs]
    t_d = [log_terms(zz) for zz in z_d]
    t_1 = [log_terms(zz) for zz in z_1]
    t_2 = [log_terms(zz) for zz in z_2]
    s_d = [suffix_sums(t_d[j][1], ud_ref) for j in subs]
    s_1 = [suffix_sums(t_1[j][1], up_ref) for j in subs]
    s_2 = [suffix_sums(t_2[j][1], up_ref) for j in subs]
    a_d = [jnp.exp(t_d[j][0] + s_d[j][0]) for j in subs]
    a_1 = [jnp.exp(t_1[j][0] + (s_1[j][0] + s_d[j][1])) for j in subs]
    run_1 = [s_d[j][1] + s_1[j][1] for j in subs]
    a_2 = [jnp.exp(t_2[j][0] + (s_2[j][0] + run_1[j])) for j in subs]
    run = [run_1[j] + s_2[j][1] for j in subs]
    for j in subs:
        if own_feature_major:
            out = weighted_values((a_2[j], a_1[j], a_d[j]), v_fm[j:j + 3])
        else:
            out = weighted_values((a_2[j], a_1[j]), v_fm, a_d[j], vd_ref)
        oacc_ref[j * tq:(j + 1) * tq, :] = out
        run_ref[j] = run[j]

    def more(carry):
        c, m = carry
        return jnp.logical_and(c >= 0, m > STICK_LOG_FLOOR)

    def older(j, carry):
        c, _ = carry
        start = pl.multiple_of(c * KEY_CHUNK, KEY_CHUNK)
        k_copy = pltpu.make_async_copy(kany_ref.at[layer, b, :, pl.ds(start, KEY_CHUNK)], kbuf_ref, sem_ref.at[0])
        v_copy = pltpu.make_async_copy(vany_ref.at[layer, b, :, pl.ds(start, KEY_CHUNK)], vbuf_ref, sem_ref.at[1])
        k_copy.start()
        v_copy.start()
        k_copy.wait()
        v_copy.wait()
        lb, l1 = log_terms(scores_fm(stacked_queries(j), (kbuf_ref,)))
        inside, tot = suffix_sums(l1, up_ref)
        run_c = run_ref[j]
        oacc_ref[j * tq:(j + 1) * tq, :] += weighted_values((jnp.exp(lb + (inside + run_c)),), (vbuf_ref,))
        run_c = run_c + tot
        run_ref[j] = run_c
        return c - 1, jnp.max(run_c)

    for j in subs:
        lax.while_loop(more, functools.partial(older, j), (n_past[j] - 3, jnp.max(run[j])))

    o_ref[...] = oacc_ref[...].astype(o_ref.dtype)


def _attention(q, k_own, v_own, k_past, v_past, *, layer, past_is_own):
    batch, t_len, _ = q.shape
    if past_is_own:
        tq = KEY_CHUNK
        n_sub = ATTN_BLOCKS_PER_STEP if t_len % (ATTN_BLOCKS_PER_STEP * tq) == 0 else 1
        past_chunks0, chunks_per_step, n_fm = 0, n_sub, n_sub + 2
    else:
        tq, n_sub = t_len, 1
        assert k_past.shape[-1] % KEY_CHUNK == 0
        past_chunks0, chunks_per_step, n_fm = k_past.shape[-1] // KEY_CHUNK, 0, 2
    assert t_len % (n_sub * tq) == 0 and tq <= KEY_CHUNK and tq & (tq - 1) == 0
    n_steps = t_len // (n_sub * tq)
    assert past_is_own or n_steps == 1

    def newer(n):
        u = lax.broadcasted_iota(jnp.int32, (n, n), 0) > lax.broadcasted_iota(jnp.int32, (n, n), 1)
        once = jnp.concatenate([jnp.ones((n, LANES), _BF16), u.astype(_BF16)], axis=1)
        return jnp.concatenate([once, once], axis=0)
    u_diag, u_past = newer(tq), newer(KEY_CHUNK)

    def fm_chunk(m):
        imap = lambda b, i: (layer, b, 0, jnp.maximum(past_chunks0 + i * chunks_per_step + m - 2, 0))
        return pl.BlockSpec((None, None, ATTN_W, KEY_CHUNK), imap)

    rm_block = pl.BlockSpec((None, n_sub * tq, ATTN_W), lambda b, i: (b, i, 0))
    const = lambda arr: pl.BlockSpec(arr.shape, lambda b, i: (0,) * arr.ndim)
    any_spec = pl.BlockSpec(memory_space=pl.ANY)
    fm_specs = [fm_chunk(m) for m in range(n_fm)]
    args = [q] + [k_past] * n_fm + [v_past] * n_fm
    in_specs = [rm_block] + fm_specs + fm_specs
    if not past_is_own:
        args += [k_own, v_own]
        in_specs += [rm_block, rm_block]
    args += [k_past, v_past, u_diag, u_past]
    in_specs += [any_spec, any_spec, const(u_diag), const(u_past)]
    return pl.pallas_call(
        functools.partial(_attn_kernel, n_sub=n_sub, past_chunks0=past_chunks0, chunks_per_step=chunks_per_step,
                          layer=layer, own_feature_major=past_is_own),
        out_shape=jax.ShapeDtypeStruct((batch, t_len, ATTN_W), _BF16),
        grid=(batch, n_steps),
        in_specs=in_specs,
        out_specs=rm_block,
        scratch_shapes=[pltpu.VMEM((n_sub * tq, ATTN_W), _F32),
                        pltpu.VMEM((n_sub, N_HEADS * tq, LANES), _F32),
                        pltpu.VMEM((ATTN_W, KEY_CHUNK), _F32),
                        pltpu.VMEM((ATTN_W, KEY_CHUNK), _F32),
                        pltpu.SemaphoreType.DMA((2,))],
        compiler_params=pltpu.CompilerParams(dimension_semantics=("arbitrary", "arbitrary"),
                                             vmem_limit_bytes=VMEM_LIMIT_BYTES),
        name="stick_attention",
    )(*args)


def _outffn_kernel(x_ref, c_ref, o_ref, wout_ref, gff_ref, w1_ref, w2_ref, y_ref):
    d_ff = w1_ref.shape[1]
    x1 = x_ref[...] + _dot(c_ref[...], wout_ref[0:CONV_CH, :]) + _dot(o_ref[...], wout_ref[CONV_CH:, :])
    h = x1 * lax.rsqrt(jnp.mean(x1 * x1, axis=-1, keepdims=True) + EPS) * gff_ref[...]
    h = h.astype(_BF16)
    ffn = None
    for cc in range(d_ff // FF_CHUNK):
        f = jnp.maximum(_dot(h, w1_ref[:, cc * FF_CHUNK:(cc + 1) * FF_CHUNK]), 0.0)
        part = _dot((f * f).astype(_BF16), w2_ref[cc * FF_CHUNK:(cc + 1) * FF_CHUNK, :])
        ffn = part if ffn is None else ffn + part
    y_ref[...] = x1 + ffn


def _outffn(x, c, o, wout, gff, w1, w2, *, layer):
    rows, d_model = x.shape
    tb = min(ROW_BLOCK, rows)
    assert rows % tb == 0 and w1.shape[-1] % FF_CHUNK == 0
    rows_spec = lambda w: pl.BlockSpec((tb, w), lambda r: (r, 0))
    const = lambda arr: pl.BlockSpec((None,) + arr.shape[1:], lambda r: (layer,) + (0,) * (arr.ndim - 1),
                                     pipeline_mode=pl.Buffered(1))
    return pl.pallas_call(
        _outffn_kernel,
        out_shape=jax.ShapeDtypeStruct((rows, d_model), _F32),
        grid=(rows // tb,),
        in_specs=[rows_spec(d_model), rows_spec(CONV_CH), rows_spec(ATTN_W),
                  const(wout), const(gff), const(w1), const(w2)],
        out_specs=rows_spec(d_model),
        compiler_params=pltpu.CompilerParams(dimension_semantics=("arbitrary",),
                                             vmem_limit_bytes=VMEM_LIMIT_BYTES),
        name="outproj_ffn",
    )(x, c, o, wout, gff, w1, w2)


def _mix_and_ffn(x, c, o, w, layer):
    batch, t_len, d_model = x.shape
    rows = batch * t_len
    y = _outffn(x.reshape(rows, d_model), c.reshape(rows, CONV_CH), o.reshape(rows, ATTN_W),
                w["wout"], w["gff"], w["w1"], w["w2"], layer=layer)
    return y.reshape(batch, t_len, d_model)


def kernel(x_prompt, x_sample, cache_k, cache_v, state_conv, g_mix, w_in, w_dw, b_dw, g_cn, b_cn, w_pw,
           g_q, g_k, w_out, g_ff, w_ff1, w_ff2):
    depth = w_in.shape[0]
    bp, tp, _ = x_prompt.shape
    bs, ts, _ = x_sample.shape
    n_past = cache_k.shape[2]
    head_id = lax.broadcasted_iota(jnp.int32, (ATTN_W, ATTN_W), 0) // HEAD_DIM
    gmat = (head_id == head_id.T).astype(_BF16)
    zero_hist = jnp.zeros((1, bp, HIST_PAD, CONV_CH), _F32)
    sample_hist = jnp.pad(state_conv, ((0, 0), (0, 0), (HIST_PAD - HIST, 0), (0, 0)))
    row = lambda a: a.reshape(depth, 1, -1)
    per_head = lambda g: jnp.tile(g, (1, N_HEADS))
    win = w_in.astype(_BF16)
    w = dict(gmix=row(g_mix), win=win, w_kv_t=jnp.transpose(win[:, :, 2 * CONV_CH + ATTN_W:], (0, 2, 1)),
             wdw=w_dw, bdw=row(b_dw), gcn=row(g_cn), bcn=row(b_cn), wpw=w_pw.astype(_BF16),
             gq=row(per_head(g_q)), gk=row(per_head(g_k)),
             gk_col=jnp.broadcast_to(per_head(g_k)[:, :, None], (depth, ATTN_W, LANES)), gmat=gmat,
             wout=w_out.astype(_BF16), gff=row(g_ff), w1=w_ff1.astype(_BF16), w2=w_ff2.astype(_BF16))
    to_fm = lambda a: jnp.transpose(a, (0, 1, 3, 4, 2)).reshape(a.shape[0], a.shape[1], ATTN_W, a.shape[2])
    from_fm = lambda a: jnp.transpose(a.reshape(a.shape[0], a.shape[1], N_HEADS, HEAD_DIM, a.shape[3]),
                                      (0, 1, 4, 2, 3))
    cache_k_fm, cache_v_fm = to_fm(cache_k), to_fm(cache_v)

    hp, hs = x_prompt, x_sample
    kv_stack = (None, None)
    conv_p, k_s, v_s, conv_s = [], [], [], []
    for l in range(depth):
        c, q, k_fm, v_fm, cv = _inproj(hp, zero_hist, 0, w, layer=l, n_seg=1, seg_rows=min(ROW_BLOCK, tp),
                                       kv_stack=kv_stack)
        kv_stack = (k_fm, v_fm)
        o = _attention(q, None, None, k_fm, v_fm, layer=l, past_is_own=True)
        hp = _mix_and_ffn(hp, c, o, w, l)
        conv_p.append(cv)
        c, q, k, v, cv = _inproj(hs, sample_hist, l, w, layer=l, n_seg=bs, seg_rows=ts)
        o = _attention(q, k, v, cache_k_fm, cache_v_fm, layer=l, past_is_own=False)
        hs = _mix_and_ffn(hs, c, o, w, l)
        k_s.append(k.reshape(bs, ts, N_HEADS, HEAD_DIM))
        v_s.append(v.reshape(bs, ts, N_HEADS, HEAD_DIM))
        conv_s.append(cv)
    return (hp, hs, from_fm(kv_stack[0]), from_fm(kv_stack[1]), jnp.stack(conv_p),
            jnp.stack(k_s), jnp.stack(v_s), jnp.stack(conv_s))
```

```python
import functools

import jax
import jax.numpy as jnp
from jax import lax
from jax.experimental import pallas as pl
from jax.experimental.pallas import tpu as pltpu

EPS = 1e-6
CONV_WIDTH = 31
HIST = CONV_WIDTH - 1
HIST_PAD = 32
SUBLANES = 8
N_HEADS = 8
HEAD_DIM = 64
ATTN_W = N_HEADS * HEAD_DIM
CONV_CH = 512
LANES = 128
KEY_CHUNK = 128
CONV_ROWS = 64
FF_CHUNK = 1024
ROW_BLOCK = 512
VMEM_LIMIT_BYTES = 56 * 1024 * 1024
ATTN_BLOCKS_PER_STEP = 4
STICK_LOG_FLOOR = -87.5
MASKED_SCORE = -1e30

_F32 = jnp.float32
_BF16 = jnp.bfloat16


def _dot(a, b):
    return jnp.dot(a, b, preferred_element_type=_F32)


def _dot_nt(a, b):
    return lax.dot_general(a, b, (((1,), (1,)), ((), ())), preferred_element_type=_F32)


def _dot_split2(x, m):
    hi = x.astype(_BF16)
    lo = (x - hi.astype(_F32)).astype(_BF16)
    return _dot(hi, m) + _dot(lo, m)


def _inproj_kernel(*refs, n_seg, seg_rows, feature_major_kv, first_layer):
    if feature_major_kv:
        (x_ref, hist_ref, gmix_ref, win_ref, wkv_ref, wdw_ref, bdw_ref, gcn_ref, bcn_ref, wpw_ref,
         gq_ref, gk_ref, gmat_ref, _, _,
         c_ref, q_ref, k_ref, v_ref, nconv_ref, uext_ref, shift_ref, act_ref) = refs
        if first_layer:
            for stack_ref in (k_ref, v_ref):
                if stack_ref.shape[0] > 1:
                    stack_ref[1:] = jnp.zeros((stack_ref.shape[0] - 1,) + stack_ref.shape[1:], _F32)
            k_ref, v_ref = k_ref.at[0], v_ref.at[0]
    else:
        (x_ref, hist_ref, gmix_ref, win_ref, wdw_ref, bdw_ref, gcn_ref, bcn_ref, wpw_ref,
         gq_ref, gk_ref, gmat_ref,
         c_ref, q_ref, k_ref, v_ref, nconv_ref, uext_ref, shift_ref, act_ref) = refs
    t = pl.program_id(1)
    n_t = pl.num_programs(1)
    rows = n_seg * seg_rows
    d_model = x_ref.shape[-1]

    x = x_ref[...].reshape(rows, d_model)
    h = x * lax.rsqrt(jnp.mean(x * x, axis=-1, keepdims=True) + EPS) * gmix_ref[...]
    h = h.astype(_BF16)

    a = _dot(h, win_ref[:, 0:CONV_CH])
    gate = _dot(h, win_ref[:, CONV_CH:2 * CONV_CH])
    u = a * jax.nn.sigmoid(gate)

    @pl.when(t == 0)
    def _():
        uext_ref[:, 0:HIST_PAD, :] = hist_ref[...]

    uext_ref[:, HIST_PAD:HIST_PAD + seg_rows, :] = u.reshape(n_seg, seg_rows, CONV_CH)

    shift_rows = shift_ref.shape[1]
    for s in range(n_seg):
        for b in range(1, SUBLANES):
            shift_ref[b - 1] = uext_ref[s, b:b + shift_rows, :]
        for r in range(seg_rows // CONV_ROWS):
            r0 = r * CONV_ROWS
            acc = jnp.zeros((CONV_ROWS, CONV_CH), _F32)
            for j in range(CONV_WIDTH):
                off = j + (HIST_PAD - HIST)
                b = off % SUBLANES
                lo = r0 + off - b
                if b == 0:
                    tap = uext_ref[s, lo:lo + CONV_ROWS, :]
                else:
                    tap = shift_ref[b - 1, lo:lo + CONV_ROWS, :]
                acc = acc + wdw_ref[j:j + 1, :] * tap
            cv = acc + bdw_ref[...]
            mu = jnp.mean(cv, axis=-1, keepdims=True)
            xc = cv - mu
            y = xc * lax.rsqrt(jnp.mean(xc * xc, axis=-1, keepdims=True) + EPS)
            y = y * gcn_ref[...] + bcn_ref[...]
            y = y * jax.nn.sigmoid(y)
            act_ref[s * seg_rows + r0:s * seg_rows + r0 + CONV_ROWS, :] = y.astype(_BF16)

    c_ref[...] = _dot(act_ref[...], wpw_ref[...]).astype(_BF16).reshape(c_ref.shape)

    @pl.when(t == n_t - 1)
    def _():
        nconv_ref[...] = uext_ref[:, seg_rows + (HIST_PAD - HIST):seg_rows + HIST_PAD, :]

    uext_ref[:, 0:HIST_PAD, :] = uext_ref[:, seg_rows:seg_rows + HIST_PAD, :]

    def head_norm(z, g_ref):
        ssq = _dot_split2(z * z, gmat_ref[...])
        return z * lax.rsqrt(ssq * (1.0 / HEAD_DIM) + EPS) * g_ref[...]

    q = _dot(h, win_ref[:, 2 * CONV_CH:2 * CONV_CH + ATTN_W])
    ssq_q = _dot((q * q).astype(_BF16), gmat_ref[...])
    q = q * lax.rsqrt(ssq_q * (1.0 / HEAD_DIM) + EPS) * gq_ref[...]
    q_ref[...] = (q * (HEAD_DIM ** -0.5)).astype(_BF16).reshape(q_ref.shape)
    if feature_major_kv:
        k_t = _dot_nt(wkv_ref[0:ATTN_W, :], h)
        gain = jnp.concatenate([gk_ref[...]] * (rows // LANES), axis=1)
        for hh in range(N_HEADS):
            rs = slice(hh * HEAD_DIM, (hh + 1) * HEAD_DIM)
            kh = k_t[rs]
            ssq = jnp.sum(kh * kh, axis=0, keepdims=True)
            k_ref[rs, :] = kh * lax.rsqrt(ssq * (1.0 / HEAD_DIM) + EPS) * gain[rs]
        v_ref[...] = _dot_nt(wkv_ref[ATTN_W:, :], h)
    else:
        k = _dot(h, win_ref[:, 2 * CONV_CH + ATTN_W:2 * CONV_CH + 2 * ATTN_W])
        k_ref[...] = head_norm(k, gk_ref).reshape(k_ref.shape)
        v_ref[...] = _dot(h, win_ref[:, 2 * CONV_CH + 2 * ATTN_W:]).reshape(v_ref.shape)


def _inproj(x, hist, hist_layer, w, *, layer, n_seg, seg_rows, kv_stack=None):
    batch, t_len, d_model = x.shape
    depth = w["win"].shape[0]
    assert batch % n_seg == 0 and t_len % seg_rows == 0 and seg_rows % CONV_ROWS == 0
    n_t = t_len // seg_rows
    assert n_seg == 1 or n_t == 1
    feature_major_kv = kv_stack is not None
    rows_spec = lambda width: pl.BlockSpec((n_seg, seg_rows, width), lambda b, t: (b, t, 0))
    shared = lambda arr: pl.BlockSpec(arr.shape, lambda b, t: (0,) * arr.ndim)
    const = lambda arr: pl.BlockSpec((None,) + arr.shape[1:], lambda b, t: (layer,) + (0,) * (arr.ndim - 1))
    hist_spec = pl.BlockSpec((None, n_seg, HIST_PAD, CONV_CH), lambda b, t: (hist_layer, b, 0, 0))
    nconv_spec = pl.BlockSpec((n_seg, HIST, CONV_CH), lambda b, t: (b, 0, 0))
    small = [w["wdw"], w["bdw"], w["gcn"], w["bcn"], w["wpw"]]
    aliases = {}
    if feature_major_kv:
        assert n_seg == 1 and seg_rows % LANES == 0
        args = [x, hist, w["gmix"], w["win"], w["w_kv_t"]] + small + [w["gq"], w["gk_col"], w["gmat"]]
        in_specs = [rows_spec(d_model), hist_spec] + [const(a) for a in args[2:-1]] + [shared(w["gmat"])]
        kv_shape = jax.ShapeDtypeStruct((depth, batch, ATTN_W, t_len), _F32)
        first_layer = kv_stack[0] is None
        if first_layer:
            kv_spec = pl.BlockSpec((depth, None, ATTN_W, seg_rows), lambda b, t: (0, b, 0, t))
            dummy = jnp.zeros((SUBLANES, LANES), _F32)
            args += [dummy, dummy]
            in_specs += [shared(dummy)] * 2
        else:
            kv_spec = pl.BlockSpec((None, None, ATTN_W, seg_rows), lambda b, t: (layer, b, 0, t))
            aliases = {len(args): 2, len(args) + 1: 3}
            args += list(kv_stack)
            in_specs += [pl.BlockSpec(memory_space=pl.ANY)] * 2
    else:
        args = [x, hist, w["gmix"], w["win"]] + small + [w["gq"], w["gk"], w["gmat"]]
        in_specs = [rows_spec(d_model), hist_spec] + [const(a) for a in args[2:-1]] + [shared(w["gmat"])]
        kv_shape = jax.ShapeDtypeStruct((batch, t_len, ATTN_W), _F32)
        kv_spec = rows_spec(ATTN_W)
        first_layer = False
    out_shape = (
        jax.ShapeDtypeStruct((batch, t_len, CONV_CH), _BF16),
        jax.ShapeDtypeStruct((batch, t_len, ATTN_W), _BF16),
        kv_shape, kv_shape,
        jax.ShapeDtypeStruct((batch, HIST, CONV_CH), _F32),
    )
    return pl.pallas_call(
        functools.partial(_inproj_kernel, n_seg=n_seg, seg_rows=seg_rows, feature_major_kv=feature_major_kv,
                          first_layer=first_layer),
        out_shape=out_shape,
        grid=(batch // n_seg, n_t),
        in_specs=in_specs,
        out_specs=(rows_spec(CONV_CH), rows_spec(ATTN_W), kv_spec, kv_spec, nconv_spec),
        scratch_shapes=[pltpu.VMEM((n_seg, seg_rows + HIST_PAD, CONV_CH), _F32),
                        pltpu.VMEM((SUBLANES - 1, seg_rows + HIST_PAD - SUBLANES, CONV_CH), _F32),
                        pltpu.VMEM((n_seg * seg_rows, CONV_CH), _BF16)],
        input_output_aliases=aliases,
        compiler_params=pltpu.CompilerParams(dimension_semantics=("arbitrary", "arbitrary"),
                                             vmem_limit_bytes=VMEM_LIMIT_BYTES),
        name="inproj_conv",
    )(*args)


def _attn_kernel(*refs, n_sub, past_chunks0, chunks_per_step, layer, own_feature_major):
    n_fm = n_sub + 2 if own_feature_major else 2
    q_ref, k_fm, v_fm, rest = refs[0], refs[1:1 + n_fm], refs[1 + n_fm:1 + 2 * n_fm], refs[1 + 2 * n_fm:]
    if not own_feature_major:
        kd_ref, vd_ref, rest = rest[0], rest[1], rest[2:]
    kany_ref, vany_ref, ud_ref, up_ref, o_ref, oacc_ref, run_ref, kbuf_ref, vbuf_ref, sem_ref = rest
    b = pl.program_id(0)
    i = pl.program_id(1)
    tq = q_ref.shape[0] // n_sub
    n_pairs = ATTN_W // LANES
    subs = range(n_sub)
    n_past = [past_chunks0 + i * chunks_per_step + j for j in subs]
    even_head = lax.broadcasted_iota(jnp.int32, (tq, LANES), 1) < HEAD_DIM
    pair_sl = [slice(p * LANES, (p + 1) * LANES) for p in range(n_pairs)]

    def stacked_queries(j):
        qs = []
        for sl in pair_sl:
            qp = q_ref[j * tq:(j + 1) * tq, sl]
            qs.append(jnp.concatenate([jnp.where(even_head, qp, 0.0), jnp.where(even_head, 0.0, qp)], axis=0))
        return qs

    def scores_fm(qs, key_refs):
        out = []
        for p, sl in enumerate(pair_sl):
            keys = jnp.concatenate([r[sl, :] for r in key_refs], axis=1).astype(_BF16)
            out.append(_dot(qs[p], keys))
        return jnp.concatenate(out, axis=0)

    def log_terms(z):
        sp = jnp.log(1.0 + jnp.exp(-jnp.abs(z)))
        neg_part = jnp.minimum(z, 0.0)
        return neg_part - sp, (neg_part - z) - sp

    def suffix_sums(log_1m, u_ref):
        hi = log_1m.astype(_BF16)
        lo = (log_1m - hi.astype(_F32)).astype(_BF16)
        r = _dot(jnp.concatenate([hi, lo], axis=1), u_ref[...])
        return r[:, LANES:], r[:, :LANES]

    def weighted_values(fm_weights, fm_value_refs, rm_weight=None, rm_value_ref=None):
        wb = jnp.concatenate(fm_weights, axis=1).astype(_BF16)
        outs = []
        for p, sl in enumerate(pair_sl):
            rows = slice(2 * tq * p, 2 * tq * (p + 1))
            vals = jnp.concatenate([r[sl, :] for r in fm_value_refs], axis=1).astype(_BF16)
            pv = _dot_nt(wb[rows], vals)
            if rm_weight is not None:
                pv = pv + _dot(rm_weight[rows].astype(_BF16), rm_value_ref[:, sl].astype(_BF16))
            outs.append(jnp.where(even_head, pv[:tq], pv[tq:]))
        return jnp.concatenate(outs, axis=1)

    qs = [stacked_queries(j) for j in subs]
    if own_feature_major:
        z = [scores_fm(qs[j], k_fm[j:j + 3]) for j in subs]
        z_own = [zj[:, 2 * KEY_CHUNK:] for zj in z]
    else:
        z = [scores_fm(qs[0], k_fm)]
        z_own = [jnp.concatenate([_dot_nt(qs[0][p], kd_ref[:, sl].astype(_BF16))
                                  for p, sl in enumerate(pair_sl)], axis=0)]
    q_row = lax.broadcasted_iota(jnp.int32, (N_HEADS * tq, tq), 0) & (tq - 1)
    causal = lax.broadcasted_iota(jnp.int32, (N_HEADS * tq, tq), 1) < q_row
    z_2 = [z[j][:, 0:KEY_CHUNK] + jnp.where(n_past[j] >= 2, 0.0, MASKED_SCORE) for j in subs]
    z_1 = [z[j][:, KEY_CHUNK:2 * KEY_CHUNK] + jnp.where(n_past[j] >= 1, 0.0, MASKED_SCORE) for j in subs]
    z_d = [jnp.where(causal, z_own[j], MASKED_SCORE) for j in subs]
    t_d = [log_terms(zz) for zz in z_d]
    t_1 = [log_terms(zz) for zz in z_1]
    t_2 = [log_terms(zz) for zz in z_2]
    s_d = [suffix_sums(t_d[j][1], ud_ref) for j in subs]
    s_1 = [suffix_sums(t_1[j][1], up_ref) for j in subs]
    s_2 = [suffix_sums(t_2[j][1], up_ref) for j in subs]
    a_d = [jnp.exp(t_d[j][0] + s_d[j][0]) for j in subs]
    a_1 = [jnp.exp(t_1[j][0] + (s_1[j][0] + s_d[j][1])) for j in subs]
    run_1 = [s_d[j][1] + s_1[j][1] for j in subs]
    a_2 = [jnp.exp(t_2[j][0] + (s_2[j][0] + run_1[j])) for j in subs]
    run = [run_1[j] + s_2[j][1] for j in subs]
    for j in subs:
        if own_feature_major:
            out = weighted_values((a_2[j], a_1[j], a_d[j]), v_fm[j:j + 3])
        else:
            out = weighted_values((a_2[j], a_1[j]), v_fm, a_d[j], vd_ref)
        oacc_ref[j * tq:(j + 1) * tq, :] = out
        run_ref[j] = run[j]

    def more(carry):
        c, m = carry
        return jnp.logical_and(c >= 0, m > STICK_LOG_FLOOR)

    def older(j, carry):
        c, _ = carry
        start = pl.multiple_of(c * KEY_CHUNK, KEY_CHUNK)
        k_copy = pltpu.make_async_copy(kany_ref.at[layer, b, :, pl.ds(start, KEY_CHUNK)], kbuf_ref, sem_ref.at[0])
        v_copy = pltpu.make_async_copy(vany_ref.at[layer, b, :, pl.ds(start, KEY_CHUNK)], vbuf_ref, sem_ref.at[1])
        k_copy.start()
        v_copy.start()
        k_copy.wait()
        v_copy.wait()
        lb, l1 = log_terms(scores_fm(stacked_queries(j), (kbuf_ref,)))
        inside, tot = suffix_sums(l1, up_ref)
        run_c = run_ref[j]
        oacc_ref[j * tq:(j + 1) * tq, :] += weighted_values((jnp.exp(lb + (inside + run_c)),), (vbuf_ref,))
        run_c = run_c + tot
        run_ref[j] = run_c
        return c - 1, jnp.max(run_c)

    for j in subs:
        lax.while_loop(more, functools.partial(older, j), (n_past[j] - 3, jnp.max(run[j])))

    o_ref[...] = oacc_ref[...].astype(o_ref.dtype)


def _attention(q, k_own, v_own, k_past, v_past, *, layer, past_is_own):
    batch, t_len, _ = q.shape
    if past_is_own:
        tq = KEY_CHUNK
        n_sub = ATTN_BLOCKS_PER_STEP if t_len % (ATTN_BLOCKS_PER_STEP * tq) == 0 else 1
        past_chunks0, chunks_per_step, n_fm = 0, n_sub, n_sub + 2
    else:
        tq, n_sub = t_len, 1
        assert k_past.shape[-1] % KEY_CHUNK == 0
        past_chunks0, chunks_per_step, n_fm = k_past.shape[-1] // KEY_CHUNK, 0, 2
    assert t_len % (n_sub * tq) == 0 and tq <= KEY_CHUNK and tq & (tq - 1) == 0
    n_steps = t_len // (n_sub * tq)
    assert past_is_own or n_steps == 1

    def newer(n):
        u = lax.broadcasted_iota(jnp.int32, (n, n), 0) > lax.broadcasted_iota(jnp.int32, (n, n), 1)
        once = jnp.concatenate([jnp.ones((n, LANES), _BF16), u.astype(_BF16)], axis=1)
        return jnp.concatenate([once, once], axis=0)
    u_diag, u_past = newer(tq), newer(KEY_CHUNK)

    def fm_chunk(m):
        imap = lambda b, i: (layer, b, 0, jnp.maximum(past_chunks0 + i * chunks_per_step + m - 2, 0))
        return pl.BlockSpec((None, None, ATTN_W, KEY_CHUNK), imap)

    rm_block = pl.BlockSpec((None, n_sub * tq, ATTN_W), lambda b, i: (b, i, 0))
    const = lambda arr: pl.BlockSpec(arr.shape, lambda b, i: (0,) * arr.ndim)
    any_spec = pl.BlockSpec(memory_space=pl.ANY)
    fm_specs = [fm_chunk(m) for m in range(n_fm)]
    args = [q] + [k_past] * n_fm + [v_past] * n_fm
    in_specs = [rm_block] + fm_specs + fm_specs
    if not past_is_own:
        args += [k_own, v_own]
        in_specs += [rm_block, rm_block]
    args += [k_past, v_past, u_diag, u_past]
    in_specs += [any_spec, any_spec, const(u_diag), const(u_past)]
    return pl.pallas_call(
        functools.partial(_attn_kernel, n_sub=n_sub, past_chunks0=past_chunks0, chunks_per_step=chunks_per_step,
                          layer=layer, own_feature_major=past_is_own),
        out_shape=jax.ShapeDtypeStruct((batch, t_len, ATTN_W), _BF16),
        grid=(batch, n_steps),
        in_specs=in_specs,
        out_specs=rm_block,
        scratch_shapes=[pltpu.VMEM((n_sub * tq, ATTN_W), _F32),
                        pltpu.VMEM((n_sub, N_HEADS * tq, LANES), _F32),
                        pltpu.VMEM((ATTN_W, KEY_CHUNK), _F32),
                        pltpu.VMEM((ATTN_W, KEY_CHUNK), _F32),
                        pltpu.SemaphoreType.DMA((2,))],
        compiler_params=pltpu.CompilerParams(dimension_semantics=("arbitrary", "arbitrary"),
                                             vmem_limit_bytes=VMEM_LIMIT_BYTES),
        name="stick_attention",
    )(*args)


def _outffn_kernel(xa_ref, ca_ref, oa_ref, xb_ref, cb_ref, ob_ref, wout_ref, gff_ref, w1_ref, w2_ref,
                   ya_ref, yb_ref, *, n_first):
    d_ff = w1_ref.shape[1]

    def mix_and_ffn(x_ref, c_ref, o_ref, y_ref):
        x1 = x_ref[...] + _dot(c_ref[...], wout_ref[0:CONV_CH, :]) + _dot(o_ref[...], wout_ref[CONV_CH:, :])
        h = x1 * lax.rsqrt(jnp.mean(x1 * x1, axis=-1, keepdims=True) + EPS) * gff_ref[...]
        h = h.astype(_BF16)
        ffn = None
        for cc in range(d_ff // FF_CHUNK):
            f = jnp.maximum(_dot(h, w1_ref[:, cc * FF_CHUNK:(cc + 1) * FF_CHUNK]), 0.0)
            part = _dot((f * f).astype(_BF16), w2_ref[cc * FF_CHUNK:(cc + 1) * FF_CHUNK, :])
            ffn = part if ffn is None else ffn + part
        y_ref[...] = x1 + ffn

    r = pl.program_id(0)

    @pl.when(r < n_first)
    def _():
        mix_and_ffn(xa_ref, ca_ref, oa_ref, ya_ref)

    @pl.when(r >= n_first)
    def _():
        mix_and_ffn(xb_ref, cb_ref, ob_ref, yb_ref)


def _outffn(xa, ca, oa, xb, cb, ob, wout, gff, w1, w2, *, layer):
    d_model = xa.shape[1]
    tb = ROW_BLOCK
    assert xa.shape[0] % tb == 0 and xb.shape[0] % tb == 0 and w1.shape[-1] % FF_CHUNK == 0
    na, nb = xa.shape[0] // tb, xb.shape[0] // tb
    first = lambda width: pl.BlockSpec((tb, width), lambda r: (jnp.minimum(r, na - 1), 0))
    second = lambda width: pl.BlockSpec((tb, width), lambda r: (jnp.maximum(r - na, 0), 0))
    const = lambda arr: pl.BlockSpec((None,) + arr.shape[1:], lambda r: (layer,) + (0,) * (arr.ndim - 1),
                                     pipeline_mode=pl.Buffered(1))
    return pl.pallas_call(
        functools.partial(_outffn_kernel, n_first=na),
        out_shape=(jax.ShapeDtypeStruct(xa.shape, _F32), jax.ShapeDtypeStruct(xb.shape, _F32)),
        grid=(na + nb,),
        in_specs=[first(d_model), first(CONV_CH), first(ATTN_W), second(d_model), second(CONV_CH), second(ATTN_W),
                  const(wout), const(gff), const(w1), const(w2)],
        out_specs=(first(d_model), second(d_model)),
        compiler_params=pltpu.CompilerParams(dimension_semantics=("arbitrary",),
                                             vmem_limit_bytes=VMEM_LIMIT_BYTES),
        name="outproj_ffn",
    )(xa, ca, oa, xb, cb, ob, wout, gff, w1, w2)


def kernel(x_prompt, x_sample, cache_k, cache_v, state_conv, g_mix, w_in, w_dw, b_dw, g_cn, b_cn, w_pw,
           g_q, g_k, w_out, g_ff, w_ff1, w_ff2):
    depth = w_in.shape[0]
    bp, tp, _ = x_prompt.shape
    bs, ts, _ = x_sample.shape
    n_past = cache_k.shape[2]
    head_id = lax.broadcasted_iota(jnp.int32, (ATTN_W, ATTN_W), 0) // HEAD_DIM
    gmat = (head_id == head_id.T).astype(_BF16)
    zero_hist = jnp.zeros((1, bp, HIST_PAD, CONV_CH), _F32)
    sample_hist = jnp.pad(state_conv, ((0, 0), (0, 0), (HIST_PAD - HIST, 0), (0, 0)))
    row = lambda a: a.reshape(depth, 1, -1)
    per_head = lambda g: jnp.tile(g, (1, N_HEADS))
    win = w_in.astype(_BF16)
    w = dict(gmix=row(g_mix), win=win, w_kv_t=jnp.transpose(win[:, :, 2 * CONV_CH + ATTN_W:], (0, 2, 1)),
             wdw=w_dw, bdw=row(b_dw), gcn=row(g_cn), bcn=row(b_cn), wpw=w_pw.astype(_BF16),
             gq=row(per_head(g_q)), gk=row(per_head(g_k)),
             gk_col=jnp.broadcast_to(per_head(g_k)[:, :, None], (depth, ATTN_W, LANES)), gmat=gmat,
             wout=w_out.astype(_BF16), gff=row(g_ff), w1=w_ff1.astype(_BF16), w2=w_ff2.astype(_BF16))
    to_fm = lambda a: jnp.transpose(a, (0, 1, 3, 4, 2)).reshape(a.shape[0], a.shape[1], ATTN_W, a.shape[2])
    from_fm = lambda a: jnp.transpose(a.reshape(a.shape[0], a.shape[1], N_HEADS, HEAD_DIM, a.shape[3]),
                                      (0, 1, 4, 2, 3))
    cache_k_fm, cache_v_fm = to_fm(cache_k), to_fm(cache_v)

    hp, hs = x_prompt, x_sample
    kv_stack = (None, None)
    conv_p, k_s, v_s, conv_s = [], [], [], []
    flat = lambda a: a.reshape(-1, a.shape[-1])
    for l in range(depth):
        cp, q, k_fm, v_fm, cv = _inproj(hp, zero_hist, 0, w, layer=l, n_seg=1, seg_rows=min(ROW_BLOCK, tp),
                                        kv_stack=kv_stack)
        kv_stack = (k_fm, v_fm)
        op = _attention(q, None, None, k_fm, v_fm, layer=l, past_is_own=True)
        conv_p.append(cv)
        cs, q, k, v, cv = _inproj(hs, sample_hist, l, w, layer=l, n_seg=bs, seg_rows=ts)
        os_ = _attention(q, k, v, cache_k_fm, cache_v_fm, layer=l, past_is_own=False)
        k_s.append(k.reshape(bs, ts, N_HEADS, HEAD_DIM))
        v_s.append(v.reshape(bs, ts, N_HEADS, HEAD_DIM))
        conv_s.append(cv)
        yp, ys = _outffn(flat(hp), flat(cp), flat(op), flat(hs), flat(cs), flat(os_),
                         w["wout"], w["gff"], w["w1"], w["w2"], layer=l)
        hp, hs = yp.reshape(hp.shape), ys.reshape(hs.shape)
    return (hp, hs, from_fm(kv_stack[0]), from_fm(kv_stack[1]), jnp.stack(conv_p),
            jnp.stack(k_s), jnp.stack(v_s), jnp.stack(conv_s))
```

```python
import functools

import jax
import jax.numpy as jnp
from jax import lax
from jax.experimental import pallas as pl
from jax.experimental.pallas import tpu as pltpu

EPS = 1e-6
CONV_WIDTH = 31
HIST = CONV_WIDTH - 1
HIST_PAD = 32
SUBLANES = 8
N_HEADS = 8
HEAD_DIM = 64
ATTN_W = N_HEADS * HEAD_DIM
CONV_CH = 512
LANES = 128
KEY_CHUNK = 128
CONV_ROWS = 64
FF_CHUNK = 1024
ROW_BLOCK = 512
WEIGHT_STAGE_ROWS = 256
VMEM_LIMIT_BYTES = 56 * 1024 * 1024
ATTN_BLOCKS_PER_STEP = 4
STICK_LOG_FLOOR = -87.5
MASKED_SCORE = -1e30

_F32 = jnp.float32
_BF16 = jnp.bfloat16


def _dot(a, b):
    return jnp.dot(a, b, preferred_element_type=_F32)


def _dot_nt(a, b):
    return lax.dot_general(a, b, (((1,), (1,)), ((), ())), preferred_element_type=_F32)


def _dot_split2(x, m):
    hi = x.astype(_BF16)
    lo = (x - hi.astype(_F32)).astype(_BF16)
    return _dot(hi, m) + _dot(lo, m)


def _inproj_kernel(*refs, n_seg, seg_rows, feature_major_kv, first_layer):
    if feature_major_kv:
        (x_ref, hist_ref, gmix_ref, win_ref, wkv_ref, wdw_ref, bdw_ref, gcn_ref, bcn_ref, wpw_ref,
         gq_ref, gk_ref, gmat_ref, _, _,
         c_ref, q_ref, k_ref, v_ref, nconv_ref, uext_ref, shift_ref, act_ref) = refs
        if first_layer:
            for stack_ref in (k_ref, v_ref):
                if stack_ref.shape[0] > 1:
                    stack_ref[1:] = jnp.zeros((stack_ref.shape[0] - 1,) + stack_ref.shape[1:], _F32)
            k_ref, v_ref = k_ref.at[0], v_ref.at[0]
    else:
        (x_ref, hist_ref, gmix_ref, win_ref, wdw_ref, bdw_ref, gcn_ref, bcn_ref, wpw_ref,
         gq_ref, gk_ref, gmat_ref,
         c_ref, q_ref, k_ref, v_ref, nconv_ref, uext_ref, shift_ref, act_ref) = refs
    t = pl.program_id(1)
    n_t = pl.num_programs(1)
    rows = n_seg * seg_rows
    d_model = x_ref.shape[-1]

    x = x_ref[...].reshape(rows, d_model)
    h = x * lax.rsqrt(jnp.mean(x * x, axis=-1, keepdims=True) + EPS) * gmix_ref[...]
    h = h.astype(_BF16)

    a = _dot(h, win_ref[:, 0:CONV_CH])
    gate = _dot(h, win_ref[:, CONV_CH:2 * CONV_CH])
    u = a * jax.nn.sigmoid(gate)

    @pl.when(t == 0)
    def _():
        uext_ref[:, 0:HIST_PAD, :] = hist_ref[...]

    uext_ref[:, HIST_PAD:HIST_PAD + seg_rows, :] = u.reshape(n_seg, seg_rows, CONV_CH)

    shift_rows = shift_ref.shape[1]
    for s in range(n_seg):
        for b in range(1, SUBLANES):
            shift_ref[b - 1] = uext_ref[s, b:b + shift_rows, :]
        for r in range(seg_rows // CONV_ROWS):
            r0 = r * CONV_ROWS
            acc = jnp.zeros((CONV_ROWS, CONV_CH), _F32)
            for j in range(CONV_WIDTH):
                off = j + (HIST_PAD - HIST)
                b = off % SUBLANES
                lo = r0 + off - b
                if b == 0:
                    tap = uext_ref[s, lo:lo + CONV_ROWS, :]
                else:
                    tap = shift_ref[b - 1, lo:lo + CONV_ROWS, :]
                acc = acc + wdw_ref[j:j + 1, :] * tap
            cv = acc + bdw_ref[...]
            mu = jnp.mean(cv, axis=-1, keepdims=True)
            xc = cv - mu
            y = xc * lax.rsqrt(jnp.mean(xc * xc, axis=-1, keepdims=True) + EPS)
            y = y * gcn_ref[...] + bcn_ref[...]
            y = y * jax.nn.sigmoid(y)
            act_ref[s * seg_rows + r0:s * seg_rows + r0 + CONV_ROWS, :] = y.astype(_BF16)

    c_ref[...] = _dot(act_ref[...], wpw_ref[...]).astype(_BF16).reshape(c_ref.shape)

    @pl.when(t == n_t - 1)
    def _():
        nconv_ref[...] = uext_ref[:, seg_rows + (HIST_PAD - HIST):seg_rows + HIST_PAD, :]

    uext_ref[:, 0:HIST_PAD, :] = uext_ref[:, seg_rows:seg_rows + HIST_PAD, :]

    def head_norm(z, g_ref):
        ssq = _dot_split2(z * z, gmat_ref[...])
        return z * lax.rsqrt(ssq * (1.0 / HEAD_DIM) + EPS) * g_ref[...]

    q = _dot(h, win_ref[:, 2 * CONV_CH:2 * CONV_CH + ATTN_W])
    ssq_q = _dot((q * q).astype(_BF16), gmat_ref[...])
    q = q * lax.rsqrt(ssq_q * (1.0 / HEAD_DIM) + EPS) * gq_ref[...]
    q_ref[...] = (q * (HEAD_DIM ** -0.5)).astype(_BF16).reshape(q_ref.shape)
    if feature_major_kv:
        k_t = _dot_nt(wkv_ref[0:ATTN_W, :], h)
        gain = jnp.concatenate([gk_ref[...]] * (rows // LANES), axis=1)
        for hh in range(N_HEADS):
            rs = slice(hh * HEAD_DIM, (hh + 1) * HEAD_DIM)
            kh = k_t[rs]
            ssq = jnp.sum(kh * kh, axis=0, keepdims=True)
            k_ref[rs, :] = kh * lax.rsqrt(ssq * (1.0 / HEAD_DIM) + EPS) * gain[rs]
        v_ref[...] = _dot_nt(wkv_ref[ATTN_W:, :], h)
    else:
        k = _dot(h, win_ref[:, 2 * CONV_CH + ATTN_W:2 * CONV_CH + 2 * ATTN_W])
        k_ref[...] = head_norm(k, gk_ref).reshape(k_ref.shape)
        v_ref[...] = _dot(h, win_ref[:, 2 * CONV_CH + 2 * ATTN_W:]).reshape(v_ref.shape)


def _inproj(x, hist, hist_layer, w, *, layer, n_seg, seg_rows, kv_stack=None):
    batch, t_len, d_model = x.shape
    depth = w["win"].shape[0]
    assert batch % n_seg == 0 and t_len % seg_rows == 0 and seg_rows % CONV_ROWS == 0
    n_t = t_len // seg_rows
    assert n_seg == 1 or n_t == 1
    feature_major_kv = kv_stack is not None
    rows_spec = lambda width: pl.BlockSpec((n_seg, seg_rows, width), lambda b, t: (b, t, 0))
    shared = lambda arr: pl.BlockSpec(arr.shape, lambda b, t: (0,) * arr.ndim)
    const = lambda arr: pl.BlockSpec((None,) + arr.shape[1:], lambda b, t: (layer,) + (0,) * (arr.ndim - 1))
    hist_spec = pl.BlockSpec((None, n_seg, HIST_PAD, CONV_CH), lambda b, t: (hist_layer, b, 0, 0))
    nconv_spec = pl.BlockSpec((n_seg, HIST, CONV_CH), lambda b, t: (b, 0, 0))
    small = [w["wdw"], w["bdw"], w["gcn"], w["bcn"], w["wpw"]]
    aliases = {}
    if feature_major_kv:
        assert n_seg == 1 and seg_rows % LANES == 0
        args = [x, hist, w["gmix"], w["win"], w["w_kv_t"]] + small + [w["gq"], w["gk_col"], w["gmat"]]
        in_specs = [rows_spec(d_model), hist_spec] + [const(a) for a in args[2:-1]] + [shared(w["gmat"])]
        kv_shape = jax.ShapeDtypeStruct((depth, batch, ATTN_W, t_len), _F32)
        first_layer = kv_stack[0] is None
        if first_layer:
            kv_spec = pl.BlockSpec((depth, None, ATTN_W, seg_rows), lambda b, t: (0, b, 0, t))
            dummy = jnp.zeros((SUBLANES, LANES), _F32)
            args += [dummy, dummy]
            in_specs += [shared(dummy)] * 2
        else:
            kv_spec = pl.BlockSpec((None, None, ATTN_W, seg_rows), lambda b, t: (layer, b, 0, t))
            aliases = {len(args): 2, len(args) + 1: 3}
            args += list(kv_stack)
            in_specs += [pl.BlockSpec(memory_space=pl.ANY)] * 2
    else:
        args = [x, hist, w["gmix"], w["win"]] + small + [w["gq"], w["gk"], w["gmat"]]
        in_specs = [rows_spec(d_model), hist_spec] + [const(a) for a in args[2:-1]] + [shared(w["gmat"])]
        kv_shape = jax.ShapeDtypeStruct((batch, t_len, ATTN_W), _F32)
        kv_spec = rows_spec(ATTN_W)
        first_layer = False
    out_shape = (
        jax.ShapeDtypeStruct((batch, t_len, CONV_CH), _BF16),
        jax.ShapeDtypeStruct((batch, t_len, ATTN_W), _BF16),
        kv_shape, kv_shape,
        jax.ShapeDtypeStruct((batch, HIST, CONV_CH), _F32),
    )
    return pl.pallas_call(
        functools.partial(_inproj_kernel, n_seg=n_seg, seg_rows=seg_rows, feature_major_kv=feature_major_kv,
                          first_layer=first_layer),
        out_shape=out_shape,
        grid=(batch // n_seg, n_t),
        in_specs=in_specs,
        out_specs=(rows_spec(CONV_CH), rows_spec(ATTN_W), kv_spec, kv_spec, nconv_spec),
        scratch_shapes=[pltpu.VMEM((n_seg, seg_rows + HIST_PAD, CONV_CH), _F32),
                        pltpu.VMEM((SUBLANES - 1, seg_rows + HIST_PAD - SUBLANES, CONV_CH), _F32),
                        pltpu.VMEM((n_seg * seg_rows, CONV_CH), _BF16)],
        input_output_aliases=aliases,
        compiler_params=pltpu.CompilerParams(dimension_semantics=("arbitrary", "arbitrary"),
                                             vmem_limit_bytes=VMEM_LIMIT_BYTES),
        name="inproj_conv",
    )(*args)


def _attn_kernel(*refs, n_sub, past_chunks0, chunks_per_step, layer, own_feature_major):
    n_fm = n_sub + 2 if own_feature_major else 2
    q_ref, k_fm, v_fm, rest = refs[0], refs[1:1 + n_fm], refs[1 + n_fm:1 + 2 * n_fm], refs[1 + 2 * n_fm:]
    if not own_feature_major:
        kd_ref, vd_ref, rest = rest[0], rest[1], rest[2:]
    kany_ref, vany_ref, ud_ref, up_ref, o_ref, oacc_ref, run_ref, kbuf_ref, vbuf_ref, sem_ref = rest
    b = pl.program_id(0)
    i = pl.program_id(1)
    tq = q_ref.shape[0] // n_sub
    n_pairs = ATTN_W // LANES
    subs = range(n_sub)
    n_past = [past_chunks0 + i * chunks_per_step + j for j in subs]
    even_head = lax.broadcasted_iota(jnp.int32, (tq, LANES), 1) < HEAD_DIM
    pair_sl = [slice(p * LANES, (p + 1) * LANES) for p in range(n_pairs)]

    def stacked_queries(j):
        qs = []
        for sl in pair_sl:
            qp = q_ref[j * tq:(j + 1) * tq, sl]
            qs.append(jnp.concatenate([jnp.where(even_head, qp, 0.0), jnp.where(even_head, 0.0, qp)], axis=0))
        return qs

    def scores_fm(qs, key_refs):
        out = []
        for p, sl in enumerate(pair_sl):
            keys = jnp.concatenate([r[sl, :] for r in key_refs], axis=1).astype(_BF16)
            out.append(_dot(qs[p], keys))
        return jnp.concatenate(out, axis=0)

    def log_terms(z):
        sp = jnp.log(1.0 + jnp.exp(-jnp.abs(z)))
        neg_part = jnp.minimum(z, 0.0)
        return neg_part - sp, (neg_part - z) - sp

    def suffix_sums(log_1m, u_ref):
        hi = log_1m.astype(_BF16)
        lo = (log_1m - hi.astype(_F32)).astype(_BF16)
        r = _dot(jnp.concatenate([hi, lo], axis=1), u_ref[...])
        return r[:, LANES:], r[:, :LANES]

    def weighted_values(fm_weights, fm_value_refs, rm_weight=None, rm_value_ref=None):
        wb = jnp.concatenate(fm_weights, axis=1).astype(_BF16)
        outs = []
        for p, sl in enumerate(pair_sl):
            rows = slice(2 * tq * p, 2 * tq * (p + 1))
            vals = jnp.concatenate([r[sl, :] for r in fm_value_refs], axis=1).astype(_BF16)
            pv = _dot_nt(wb[rows], vals)
            if rm_weight is not None:
                pv = pv + _dot(rm_weight[rows].astype(_BF16), rm_value_ref[:, sl].astype(_BF16))
            outs.append(jnp.where(even_head, pv[:tq], pv[tq:]))
        return jnp.concatenate(outs, axis=1)

    qs = [stacked_queries(j) for j in subs]
    if own_feature_major:
        z = [scores_fm(qs[j], k_fm[j:j + 3]) for j in subs]
        z_own = [zj[:, 2 * KEY_CHUNK:] for zj in z]
    else:
        z = [scores_fm(qs[0], k_fm)]
        z_own = [jnp.concatenate([_dot_nt(qs[0][p], kd_ref[:, sl].astype(_BF16))
                                  for p, sl in enumerate(pair_sl)], axis=0)]
    q_row = lax.broadcasted_iota(jnp.int32, (N_HEADS * tq, tq), 0) & (tq - 1)
    causal = lax.broadcasted_iota(jnp.int32, (N_HEADS * tq, tq), 1) < q_row
    z_2 = [z[j][:, 0:KEY_CHUNK] + jnp.where(n_past[j] >= 2, 0.0, MASKED_SCORE) for j in subs]
    z_1 = [z[j][:, KEY_CHUNK:2 * KEY_CHUNK] + jnp.where(n_past[j] >= 1, 0.0, MASKED_SCORE) for j in subs]
    z_d = [jnp.where(causal, z_own[j], MASKED_SCORE) for j in subs]
    t_d = [log_terms(zz) for zz in z_d]
    t_1 = [log_terms(zz) for zz in z_1]
    t_2 = [log_terms(zz) for zz in z_2]
    s_d = [suffix_sums(t_d[j][1], ud_ref) for j in subs]
    s_1 = [suffix_sums(t_1[j][1], up_ref) for j in subs]
    s_2 = [suffix_sums(t_2[j][1], up_ref) for j in subs]
    a_d = [jnp.exp(t_d[j][0] + s_d[j][0]) for j in subs]
    a_1 = [jnp.exp(t_1[j][0] + (s_1[j][0] + s_d[j][1])) for j in subs]
    run_1 = [s_d[j][1] + s_1[j][1] for j in subs]
    a_2 = [jnp.exp(t_2[j][0] + (s_2[j][0] + run_1[j])) for j in subs]
    run = [run_1[j] + s_2[j][1] for j in subs]
    for j in subs:
        if own_feature_major:
            out = weighted_values((a_2[j], a_1[j], a_d[j]), v_fm[j:j + 3])
        else:
            out = weighted_values((a_2[j], a_1[j]), v_fm, a_d[j], vd_ref)
        oacc_ref[j * tq:(j + 1) * tq, :] = out
        run_ref[j] = run[j]

    def more(carry):
        c, m = carry
        return jnp.logical_and(c >= 0, m > STICK_LOG_FLOOR)

    def older(j, carry):
        c, _ = carry
        start = pl.multiple_of(c * KEY_CHUNK, KEY_CHUNK)
        k_copy = pltpu.make_async_copy(kany_ref.at[layer, b, :, pl.ds(start, KEY_CHUNK)], kbuf_ref, sem_ref.at[0])
        v_copy = pltpu.make_async_copy(vany_ref.at[layer, b, :, pl.ds(start, KEY_CHUNK)], vbuf_ref, sem_ref.at[1])
        k_copy.start()
        v_copy.start()
        k_copy.wait()
        v_copy.wait()
        lb, l1 = log_terms(scores_fm(stacked_queries(j), (kbuf_ref,)))
        inside, tot = suffix_sums(l1, up_ref)
        run_c = run_ref[j]
        oacc_ref[j * tq:(j + 1) * tq, :] += weighted_values((jnp.exp(lb + (inside + run_c)),), (vbuf_ref,))
        run_c = run_c + tot
        run_ref[j] = run_c
        return c - 1, jnp.max(run_c)

    for j in subs:
        lax.while_loop(more, functools.partial(older, j), (n_past[j] - 3, jnp.max(run[j])))

    o_ref[...] = oacc_ref[...].astype(o_ref.dtype)


def _attention(q, k_own, v_own, k_past, v_past, *, layer, past_is_own):
    batch, t_len, _ = q.shape
    if past_is_own:
        tq = KEY_CHUNK
        n_sub = ATTN_BLOCKS_PER_STEP if t_len % (ATTN_BLOCKS_PER_STEP * tq) == 0 else 1
        past_chunks0, chunks_per_step, n_fm = 0, n_sub, n_sub + 2
    else:
        tq, n_sub = t_len, 1
        assert k_past.shape[-1] % KEY_CHUNK == 0
        past_chunks0, chunks_per_step, n_fm = k_past.shape[-1] // KEY_CHUNK, 0, 2
    assert t_len % (n_sub * tq) == 0 and tq <= KEY_CHUNK and tq & (tq - 1) == 0
    n_steps = t_len // (n_sub * tq)
    assert past_is_own or n_steps == 1

    def newer(n):
        u = lax.broadcasted_iota(jnp.int32, (n, n), 0) > lax.broadcasted_iota(jnp.int32, (n, n), 1)
        once = jnp.concatenate([jnp.ones((n, LANES), _BF16), u.astype(_BF16)], axis=1)
        return jnp.concatenate([once, once], axis=0)
    u_diag, u_past = newer(tq), newer(KEY_CHUNK)

    def fm_chunk(m):
        imap = lambda b, i: (layer, b, 0, jnp.maximum(past_chunks0 + i * chunks_per_step + m - 2, 0))
        return pl.BlockSpec((None, None, ATTN_W, KEY_CHUNK), imap)

    rm_block = pl.BlockSpec((None, n_sub * tq, ATTN_W), lambda b, i: (b, i, 0))
    const = lambda arr: pl.BlockSpec(arr.shape, lambda b, i: (0,) * arr.ndim)
    any_spec = pl.BlockSpec(memory_space=pl.ANY)
    fm_specs = [fm_chunk(m) for m in range(n_fm)]
    args = [q] + [k_past] * n_fm + [v_past] * n_fm
    in_specs = [rm_block] + fm_specs + fm_specs
    if not past_is_own:
        args += [k_own, v_own]
        in_specs += [rm_block, rm_block]
    args += [k_past, v_past, u_diag, u_past]
    in_specs += [any_spec, any_spec, const(u_diag), const(u_past)]
    return pl.pallas_call(
        functools.partial(_attn_kernel, n_sub=n_sub, past_chunks0=past_chunks0, chunks_per_step=chunks_per_step,
                          layer=layer, own_feature_major=past_is_own),
        out_shape=jax.ShapeDtypeStruct((batch, t_len, ATTN_W), _BF16),
        grid=(batch, n_steps),
        in_specs=in_specs,
        out_specs=rm_block,
        scratch_shapes=[pltpu.VMEM((n_sub * tq, ATTN_W), _F32),
                        pltpu.VMEM((n_sub, N_HEADS * tq, LANES), _F32),
                        pltpu.VMEM((ATTN_W, KEY_CHUNK), _F32),
                        pltpu.VMEM((ATTN_W, KEY_CHUNK), _F32),
                        pltpu.SemaphoreType.DMA((2,))],
        compiler_params=pltpu.CompilerParams(dimension_semantics=("arbitrary", "arbitrary"),
                                             vmem_limit_bytes=VMEM_LIMIT_BYTES),
        name="stick_attention",
    )(*args)


def _outffn_kernel(xa_ref, ca_ref, oa_ref, xb_ref, cb_ref, ob_ref, wout_any, gff_ref, w1_any, w2_any,
                   ya_ref, yb_ref, wout_ref, w1_ref, w2_ref, stage_wide_ref, stage_ref, sem_ref, *, n_first, layer):
    d_ff = w1_ref.shape[1]
    r = pl.program_id(0)

    def load_as_bf16(src_any, dst_ref, buf_ref):
        rows = buf_ref.shape[1]
        n = dst_ref.shape[0] // rows

        def chunk_copy(i):
            return pltpu.make_async_copy(src_any.at[layer, pl.ds(i * rows, rows), :], buf_ref.at[i % 2],
                                         sem_ref.at[i % 2])
        chunk_copy(0).start()
        for i in range(n):
            if i + 1 < n:
                chunk_copy(i + 1).start()
            chunk_copy(i).wait()
            dst_ref[i * rows:(i + 1) * rows, :] = buf_ref[i % 2].astype(_BF16)

    @pl.when(r == 0)
    def _():
        load_as_bf16(wout_any, wout_ref, stage_ref)
        load_as_bf16(w1_any, w1_ref, stage_wide_ref)
        load_as_bf16(w2_any, w2_ref, stage_ref)

    def mix_and_ffn(x_ref, c_ref, o_ref, y_ref):
        x1 = x_ref[...] + _dot(c_ref[...], wout_ref[0:CONV_CH, :]) + _dot(o_ref[...], wout_ref[CONV_CH:, :])
        h = x1 * lax.rsqrt(jnp.mean(x1 * x1, axis=-1, keepdims=True) + EPS) * gff_ref[...]
        h = h.astype(_BF16)
        ffn = None
        for cc in range(d_ff // FF_CHUNK):
            f = jnp.maximum(_dot(h, w1_ref[:, cc * FF_CHUNK:(cc + 1) * FF_CHUNK]), 0.0)
            part = _dot((f * f).astype(_BF16), w2_ref[cc * FF_CHUNK:(cc + 1) * FF_CHUNK, :])
            ffn = part if ffn is None else ffn + part
        y_ref[...] = x1 + ffn

    @pl.when(r < n_first)
    def _():
        mix_and_ffn(xa_ref, ca_ref, oa_ref, ya_ref)

    @pl.when(r >= n_first)
    def _():
        mix_and_ffn(xb_ref, cb_ref, ob_ref, yb_ref)


def _outffn(xa, ca, oa, xb, cb, ob, wout, gff, w1, w2, *, layer):
    d_model = xa.shape[1]
    d_ff = w1.shape[-1]
    tb = ROW_BLOCK
    assert xa.shape[0] % tb == 0 and xb.shape[0] % tb == 0 and d_ff % FF_CHUNK == 0
    assert d_model % WEIGHT_STAGE_ROWS == 0 and d_ff % WEIGHT_STAGE_ROWS == 0
    na, nb = xa.shape[0] // tb, xb.shape[0] // tb
    first = lambda width: pl.BlockSpec((tb, width), lambda r: (jnp.minimum(r, na - 1), 0))
    second = lambda width: pl.BlockSpec((tb, width), lambda r: (jnp.maximum(r - na, 0), 0))
    any_spec = pl.BlockSpec(memory_space=pl.ANY)
    gff_spec = pl.BlockSpec((None,) + gff.shape[1:], lambda r: (layer, 0, 0))
    wide_rows = WEIGHT_STAGE_ROWS * d_model // d_ff
    return pl.pallas_call(
        functools.partial(_outffn_kernel, n_first=na, layer=layer),
        out_shape=(jax.ShapeDtypeStruct(xa.shape, _F32), jax.ShapeDtypeStruct(xb.shape, _F32)),
        grid=(na + nb,),
        in_specs=[first(d_model), first(CONV_CH), first(ATTN_W), second(d_model), second(CONV_CH), second(ATTN_W),
                  any_spec, gff_spec, any_spec, any_spec],
        out_specs=(first(d_model), second(d_model)),
        scratch_shapes=[pltpu.VMEM(wout.shape[1:], _BF16), pltpu.VMEM(w1.shape[1:], _BF16),
                        pltpu.VMEM(w2.shape[1:], _BF16),
                        pltpu.VMEM((2, wide_rows, d_ff), _F32), pltpu.VMEM((2, WEIGHT_STAGE_ROWS, d_model), _F32),
                        pltpu.SemaphoreType.DMA((2,))],
        compiler_params=pltpu.CompilerParams(dimension_semantics=("arbitrary",),
                                             vmem_limit_bytes=VMEM_LIMIT_BYTES),
        name="outproj_ffn",
    )(xa, ca, oa, xb, cb, ob, wout, gff, w1, w2)


def kernel(x_prompt, x_sample, cache_k, cache_v, state_conv, g_mix, w_in, w_dw, b_dw, g_cn, b_cn, w_pw,
           g_q, g_k, w_out, g_ff, w_ff1, w_ff2):
    depth = w_in.shape[0]
    bp, tp, _ = x_prompt.shape
    bs, ts, _ = x_sample.shape
    n_past = cache_k.shape[2]
    head_id = lax.broadcasted_iota(jnp.int32, (ATTN_W, ATTN_W), 0) // HEAD_DIM
    gmat = (head_id == head_id.T).astype(_BF16)
    zero_hist = jnp.zeros((1, bp, HIST_PAD, CONV_CH), _F32)
    sample_hist = jnp.pad(state_conv, ((0, 0), (0, 0), (HIST_PAD - HIST, 0), (0, 0)))
    row = lambda a: a.reshape(depth, 1, -1)
    per_head = lambda g: jnp.tile(g, (1, N_HEADS))
    win = w_in.astype(_BF16)
    w = dict(gmix=row(g_mix), win=win, w_kv_t=jnp.transpose(win[:, :, 2 * CONV_CH + ATTN_W:], (0, 2, 1)),
             wdw=w_dw, bdw=row(b_dw), gcn=row(g_cn), bcn=row(b_cn), wpw=w_pw.astype(_BF16),
             gq=row(per_head(g_q)), gk=row(per_head(g_k)),
             gk_col=jnp.broadcast_to(per_head(g_k)[:, :, None], (depth, ATTN_W, LANES)), gmat=gmat,
             wout=w_out, gff=row(g_ff), w1=w_ff1, w2=w_ff2)
    to_fm = lambda a: jnp.transpose(a, (0, 1, 3, 4, 2)).reshape(a.shape[0], a.shape[1], ATTN_W, a.shape[2])
    from_fm = lambda a: jnp.transpose(a.reshape(a.shape[0], a.shape[1], N_HEADS, HEAD_DIM, a.shape[3]),
                                      (0, 1, 4, 2, 3))
    cache_k_fm, cache_v_fm = to_fm(cache_k), to_fm(cache_v)

    hp, hs = x_prompt, x_sample
    kv_stack = (None, None)
    conv_p, k_s, v_s, conv_s = [], [], [], []
    flat = lambda a: a.reshape(-1, a.shape[-1])
    for l in range(depth):
        cp, q, k_fm, v_fm, cv = _inproj(hp, zero_hist, 0, w, layer=l, n_seg=1, seg_rows=min(ROW_BLOCK, tp),
                                        kv_stack=kv_stack)
        kv_stack = (k_fm, v_fm)
        op = _attention(q, None, None, k_fm, v_fm, layer=l, past_is_own=True)
        conv_p.append(cv)
        cs, q, k, v, cv = _inproj(hs, sample_hist, l, w, layer=l, n_seg=bs, seg_rows=ts)
        os_ = _attention(q, k, v, cache_k_fm, cache_v_fm, layer=l, past_is_own=False)
        k_s.append(k.reshape(bs, ts, N_HEADS, HEAD_DIM))
        v_s.append(v.reshape(bs, ts, N_HEADS, HEAD_DIM))
        conv_s.append(cv)
        yp, ys = _outffn(flat(hp), flat(cp), flat(op), flat(hs), flat(cs), flat(os_),
                         w["wout"], w["gff"], w["w1"], w["w2"], layer=l)
        hp, hs = yp.reshape(hp.shape), ys.reshape(hs.shape)
    return (hp, hs, from_fm(kv_stack[0]), from_fm(kv_stack[1]), jnp.stack(conv_p),
            jnp.stack(k_s), jnp.stack(v_s), jnp.stack(conv_s))
```

```python
import functools

import jax
import jax.numpy as jnp
from jax import lax
from jax.experimental import pallas as pl
from jax.experimental.pallas import tpu as pltpu

EPS = 1e-6
CONV_WIDTH = 31
HIST = CONV_WIDTH - 1
HIST_PAD = 32
SUBLANES = 8
N_HEADS = 8
HEAD_DIM = 64
ATTN_W = N_HEADS * HEAD_DIM
CONV_CH = 512
LANES = 128
KEY_CHUNK = 128
CONV_ROWS = 64
FF_CHUNK = 1024
ROW_BLOCK = 512
WEIGHT_STAGE_ROWS = 256
WEIGHT_STAGE_SLOTS = 4
VMEM_LIMIT_BYTES = 56 * 1024 * 1024
ATTN_BLOCKS_PER_STEP = 4
STICK_LOG_FLOOR = -87.5
MASKED_SCORE = -1e30

_F32 = jnp.float32
_BF16 = jnp.bfloat16


def _dot(a, b):
    return jnp.dot(a, b, preferred_element_type=_F32)


def _dot_nt(a, b):
    return lax.dot_general(a, b, (((1,), (1,)), ((), ())), preferred_element_type=_F32)


def _dot_split2(x, m):
    hi = x.astype(_BF16)
    lo = (x - hi.astype(_F32)).astype(_BF16)
    return _dot(hi, m) + _dot(lo, m)


def _inproj_kernel(*refs, n_seg, seg_rows, feature_major_kv, first_layer):
    if feature_major_kv:
        (x_ref, hist_ref, gmix_ref, win_ref, wkv_ref, wdw_ref, bdw_ref, gcn_ref, bcn_ref, wpw_ref,
         gq_ref, gk_ref, gmat_ref, _, _,
         c_ref, q_ref, k_ref, v_ref, nconv_ref, uext_ref, shift_ref, act_ref) = refs
        if first_layer:
            for stack_ref in (k_ref, v_ref):
                if stack_ref.shape[0] > 1:
                    stack_ref[1:] = jnp.zeros((stack_ref.shape[0] - 1,) + stack_ref.shape[1:], _F32)
            k_ref, v_ref = k_ref.at[0], v_ref.at[0]
    else:
        (x_ref, hist_ref, gmix_ref, win_ref, wdw_ref, bdw_ref, gcn_ref, bcn_ref, wpw_ref,
         gq_ref, gk_ref, gmat_ref,
         c_ref, q_ref, k_ref, v_ref, nconv_ref, uext_ref, shift_ref, act_ref) = refs
    t = pl.program_id(1)
    n_t = pl.num_programs(1)
    rows = n_seg * seg_rows
    d_model = x_ref.shape[-1]

    x = x_ref[...].reshape(rows, d_model)
    h = x * lax.rsqrt(jnp.mean(x * x, axis=-1, keepdims=True) + EPS) * gmix_ref[...]
    h = h.astype(_BF16)

    a = _dot(h, win_ref[:, 0:CONV_CH])
    gate = _dot(h, win_ref[:, CONV_CH:2 * CONV_CH])
    u = a * jax.nn.sigmoid(gate)

    @pl.when(t == 0)
    def _():
        uext_ref[:, 0:HIST_PAD, :] = hist_ref[...]

    uext_ref[:, HIST_PAD:HIST_PAD + seg_rows, :] = u.reshape(n_seg, seg_rows, CONV_CH)

    shift_rows = shift_ref.shape[1]
    for s in range(n_seg):
        for b in range(1, SUBLANES):
            shift_ref[b - 1] = uext_ref[s, b:b + shift_rows, :]
        for r in range(seg_rows // CONV_ROWS):
            r0 = r * CONV_ROWS
            acc = jnp.zeros((CONV_ROWS, CONV_CH), _F32)
            for j in range(CONV_WIDTH):
                off = j + (HIST_PAD - HIST)
                b = off % SUBLANES
                lo = r0 + off - b
                if b == 0:
                    tap = uext_ref[s, lo:lo + CONV_ROWS, :]
                else:
                    tap = shift_ref[b - 1, lo:lo + CONV_ROWS, :]
                acc = acc + wdw_ref[j:j + 1, :] * tap
            cv = acc + bdw_ref[...]
            mu = jnp.mean(cv, axis=-1, keepdims=True)
            xc = cv - mu
            y = xc * lax.rsqrt(jnp.mean(xc * xc, axis=-1, keepdims=True) + EPS)
            y = y * gcn_ref[...] + bcn_ref[...]
            y = y * jax.nn.sigmoid(y)
            act_ref[s * seg_rows + r0:s * seg_rows + r0 + CONV_ROWS, :] = y.astype(_BF16)

    c_ref[...] = _dot(act_ref[...], wpw_ref[...]).astype(_BF16).reshape(c_ref.shape)

    @pl.when(t == n_t - 1)
    def _():
        nconv_ref[...] = uext_ref[:, seg_rows + (HIST_PAD - HIST):seg_rows + HIST_PAD, :]

    uext_ref[:, 0:HIST_PAD, :] = uext_ref[:, seg_rows:seg_rows + HIST_PAD, :]

    def head_norm(z, g_ref):
        ssq = _dot_split2(z * z, gmat_ref[...])
        return z * lax.rsqrt(ssq * (1.0 / HEAD_DIM) + EPS) * g_ref[...]

    q = _dot(h, win_ref[:, 2 * CONV_CH:2 * CONV_CH + ATTN_W])
    ssq_q = _dot((q * q).astype(_BF16), gmat_ref[...])
    q = q * lax.rsqrt(ssq_q * (1.0 / HEAD_DIM) + EPS) * gq_ref[...]
    q_ref[...] = (q * (HEAD_DIM ** -0.5)).astype(_BF16).reshape(q_ref.shape)
    if feature_major_kv:
        k_t = _dot_nt(wkv_ref[0:ATTN_W, :], h)
        gain = jnp.concatenate([gk_ref[...]] * (rows // LANES), axis=1)
        for hh in range(N_HEADS):
            rs = slice(hh * HEAD_DIM, (hh + 1) * HEAD_DIM)
            kh = k_t[rs]
            ssq = jnp.sum(kh * kh, axis=0, keepdims=True)
            k_ref[rs, :] = kh * lax.rsqrt(ssq * (1.0 / HEAD_DIM) + EPS) * gain[rs]
        v_ref[...] = _dot_nt(wkv_ref[ATTN_W:, :], h)
    else:
        k = _dot(h, win_ref[:, 2 * CONV_CH + ATTN_W:2 * CONV_CH + 2 * ATTN_W])
        k_ref[...] = head_norm(k, gk_ref).reshape(k_ref.shape)
        v_ref[...] = _dot(h, win_ref[:, 2 * CONV_CH + 2 * ATTN_W:]).reshape(v_ref.shape)


def _inproj(x, hist, hist_layer, w, *, layer, n_seg, seg_rows, kv_stack=None):
    batch, t_len, d_model = x.shape
    depth = w["win"].shape[0]
    assert batch % n_seg == 0 and t_len % seg_rows == 0 and seg_rows % CONV_ROWS == 0
    n_t = t_len // seg_rows
    assert n_seg == 1 or n_t == 1
    feature_major_kv = kv_stack is not None
    rows_spec = lambda width: pl.BlockSpec((n_seg, seg_rows, width), lambda b, t: (b, t, 0))
    shared = lambda arr: pl.BlockSpec(arr.shape, lambda b, t: (0,) * arr.ndim)
    const = lambda arr: pl.BlockSpec((None,) + arr.shape[1:], lambda b, t: (layer,) + (0,) * (arr.ndim - 1))
    hist_spec = pl.BlockSpec((None, n_seg, HIST_PAD, CONV_CH), lambda b, t: (hist_layer, b, 0, 0))
    nconv_spec = pl.BlockSpec((n_seg, HIST, CONV_CH), lambda b, t: (b, 0, 0))
    small = [w["wdw"], w["bdw"], w["gcn"], w["bcn"], w["wpw"]]
    aliases = {}
    if feature_major_kv:
        assert n_seg == 1 and seg_rows % LANES == 0
        args = [x, hist, w["gmix"], w["win"], w["w_kv_t"]] + small + [w["gq"], w["gk_col"], w["gmat"]]
        in_specs = [rows_spec(d_model), hist_spec] + [const(a) for a in args[2:-1]] + [shared(w["gmat"])]
        kv_shape = jax.ShapeDtypeStruct((depth, batch, ATTN_W, t_len), _F32)
        first_layer = kv_stack[0] is None
        if first_layer:
            kv_spec = pl.BlockSpec((depth, None, ATTN_W, seg_rows), lambda b, t: (0, b, 0, t))
            dummy = jnp.zeros((SUBLANES, LANES), _F32)
            args += [dummy, dummy]
            in_specs += [shared(dummy)] * 2
        else:
            kv_spec = pl.BlockSpec((None, None, ATTN_W, seg_rows), lambda b, t: (layer, b, 0, t))
            aliases = {len(args): 2, len(args) + 1: 3}
            args += list(kv_stack)
            in_specs += [pl.BlockSpec(memory_space=pl.ANY)] * 2
    else:
        args = [x, hist, w["gmix"], w["win"]] + small + [w["gq"], w["gk"], w["gmat"]]
        in_specs = [rows_spec(d_model), hist_spec] + [const(a) for a in args[2:-1]] + [shared(w["gmat"])]
        kv_shape = jax.ShapeDtypeStruct((batch, t_len, ATTN_W), _F32)
        kv_spec = rows_spec(ATTN_W)
        first_layer = False
    out_shape = (
        jax.ShapeDtypeStruct((batch, t_len, CONV_CH), _BF16),
        jax.ShapeDtypeStruct((batch, t_len, ATTN_W), _BF16),
        kv_shape, kv_shape,
        jax.ShapeDtypeStruct((batch, HIST, CONV_CH), _F32),
    )
    return pl.pallas_call(
        functools.partial(_inproj_kernel, n_seg=n_seg, seg_rows=seg_rows, feature_major_kv=feature_major_kv,
                          first_layer=first_layer),
        out_shape=out_shape,
        grid=(batch // n_seg, n_t),
        in_specs=in_specs,
        out_specs=(rows_spec(CONV_CH), rows_spec(ATTN_W), kv_spec, kv_spec, nconv_spec),
        scratch_shapes=[pltpu.VMEM((n_seg, seg_rows + HIST_PAD, CONV_CH), _F32),
                        pltpu.VMEM((SUBLANES - 1, seg_rows + HIST_PAD - SUBLANES, CONV_CH), _F32),
                        pltpu.VMEM((n_seg * seg_rows, CONV_CH), _BF16)],
        input_output_aliases=aliases,
        compiler_params=pltpu.CompilerParams(dimension_semantics=("arbitrary", "arbitrary"),
                                             vmem_limit_bytes=VMEM_LIMIT_BYTES),
        name="inproj_conv",
    )(*args)


def _attn_kernel(*refs, n_sub, past_chunks0, chunks_per_step, layer, own_feature_major):
    n_fm = n_sub + 2 if own_feature_major else 2
    q_ref, k_fm, v_fm, rest = refs[0], refs[1:1 + n_fm], refs[1 + n_fm:1 + 2 * n_fm], refs[1 + 2 * n_fm:]
    if not own_feature_major:
        kd_ref, vd_ref, rest = rest[0], rest[1], rest[2:]
    kany_ref, vany_ref, ud_ref, up_ref, o_ref, oacc_ref, run_ref, kbuf_ref, vbuf_ref, sem_ref = rest
    b = pl.program_id(0)
    i = pl.program_id(1)
    tq = q_ref.shape[0] // n_sub
    n_pairs = ATTN_W // LANES
    subs = range(n_sub)
    n_past = [past_chunks0 + i * chunks_per_step + j for j in subs]
    even_head = lax.broadcasted_iota(jnp.int32, (tq, LANES), 1) < HEAD_DIM
    pair_sl = [slice(p * LANES, (p + 1) * LANES) for p in range(n_pairs)]

    def stacked_queries(j):
        qs = []
        for sl in pair_sl:
            qp = q_ref[j * tq:(j + 1) * tq, sl]
            qs.append(jnp.concatenate([jnp.where(even_head, qp, 0.0), jnp.where(even_head, 0.0, qp)], axis=0))
        return qs

    def scores_fm(qs, key_refs):
        out = []
        for p, sl in enumerate(pair_sl):
            keys = jnp.concatenate([r[sl, :] for r in key_refs], axis=1).astype(_BF16)
            out.append(_dot(qs[p], keys))
        return jnp.concatenate(out, axis=0)

    def log_terms(z):
        sp = jnp.log(1.0 + jnp.exp(-jnp.abs(z)))
        neg_part = jnp.minimum(z, 0.0)
        return neg_part - sp, (neg_part - z) - sp

    def suffix_sums(log_1m, u_ref):
        hi = log_1m.astype(_BF16)
        lo = (log_1m - hi.astype(_F32)).astype(_BF16)
        r = _dot(jnp.concatenate([hi, lo], axis=1), u_ref[...])
        return r[:, LANES:], r[:, :LANES]

    def weighted_values(fm_weights, fm_value_refs, rm_weight=None, rm_value_ref=None):
        wb = jnp.concatenate(fm_weights, axis=1).astype(_BF16)
        outs = []
        for p, sl in enumerate(pair_sl):
            rows = slice(2 * tq * p, 2 * tq * (p + 1))
            vals = jnp.concatenate([r[sl, :] for r in fm_value_refs], axis=1).astype(_BF16)
            pv = _dot_nt(wb[rows], vals)
            if rm_weight is not None:
                pv = pv + _dot(rm_weight[rows].astype(_BF16), rm_value_ref[:, sl].astype(_BF16))
            outs.append(jnp.where(even_head, pv[:tq], pv[tq:]))
        return jnp.concatenate(outs, axis=1)

    qs = [stacked_queries(j) for j in subs]
    if own_feature_major:
        z = [scores_fm(qs[j], k_fm[j:j + 3]) for j in subs]
        z_own = [zj[:, 2 * KEY_CHUNK:] for zj in z]
    else:
        z = [scores_fm(qs[0], k_fm)]
        z_own = [jnp.concatenate([_dot_nt(qs[0][p], kd_ref[:, sl].astype(_BF16))
                                  for p, sl in enumerate(pair_sl)], axis=0)]
    q_row = lax.broadcasted_iota(jnp.int32, (N_HEADS * tq, tq), 0) & (tq - 1)
    causal = lax.broadcasted_iota(jnp.int32, (N_HEADS * tq, tq), 1) < q_row
    z_2 = [z[j][:, 0:KEY_CHUNK] + jnp.where(n_past[j] >= 2, 0.0, MASKED_SCORE) for j in subs]
    z_1 = [z[j][:, KEY_CHUNK:2 * KEY_CHUNK] + jnp.where(n_past[j] >= 1, 0.0, MASKED_SCORE) for j in subs]
    z_d = [jnp.where(causal, z_own[j], MASKED_SCORE) for j in subs]
    t_d = [log_terms(zz) for zz in z_d]
    t_1 = [log_terms(zz) for zz in z_1]
    t_2 = [log_terms(zz) for zz in z_2]
    s_d = [suffix_sums(t_d[j][1], ud_ref) for j in subs]
    s_1 = [suffix_sums(t_1[j][1], up_ref) for j in subs]
    s_2 = [suffix_sums(t_2[j][1], up_ref) for j in subs]
    a_d = [jnp.exp(t_d[j][0] + s_d[j][0]) for j in subs]
    a_1 = [jnp.exp(t_1[j][0] + (s_1[j][0] + s_d[j][1])) for j in subs]
    run_1 = [s_d[j][1] + s_1[j][1] for j in subs]
    a_2 = [jnp.exp(t_2[j][0] + (s_2[j][0] + run_1[j])) for j in subs]
    run = [run_1[j] + s_2[j][1] for j in subs]
    for j in subs:
        if own_feature_major:
            out = weighted_values((a_2[j], a_1[j], a_d[j]), v_fm[j:j + 3])
        else:
            out = weighted_values((a_2[j], a_1[j]), v_fm, a_d[j], vd_ref)
        oacc_ref[j * tq:(j + 1) * tq, :] = out
        run_ref[j] = run[j]

    def more(carry):
        c, m = carry
        return jnp.logical_and(c >= 0, m > STICK_LOG_FLOOR)

    def older(j, carry):
        c, _ = carry
        start = pl.multiple_of(c * KEY_CHUNK, KEY_CHUNK)
        k_copy = pltpu.make_async_copy(kany_ref.at[layer, b, :, pl.ds(start, KEY_CHUNK)], kbuf_ref, sem_ref.at[0])
        v_copy = pltpu.make_async_copy(vany_ref.at[layer, b, :, pl.ds(start, KEY_CHUNK)], vbuf_ref, sem_ref.at[1])
        k_copy.start()
        v_copy.start()
        k_copy.wait()
        v_copy.wait()
        lb, l1 = log_terms(scores_fm(stacked_queries(j), (kbuf_ref,)))
        inside, tot = suffix_sums(l1, up_ref)
        run_c = run_ref[j]
        oacc_ref[j * tq:(j + 1) * tq, :] += weighted_values((jnp.exp(lb + (inside + run_c)),), (vbuf_ref,))
        run_c = run_c + tot
        run_ref[j] = run_c
        return c - 1, jnp.max(run_c)

    for j in subs:
        lax.while_loop(more, functools.partial(older, j), (n_past[j] - 3, jnp.max(run[j])))

    o_ref[...] = oacc_ref[...].astype(o_ref.dtype)


def _attention(q, k_own, v_own, k_past, v_past, *, layer, past_is_own):
    batch, t_len, _ = q.shape
    if past_is_own:
        tq = KEY_CHUNK
        n_sub = ATTN_BLOCKS_PER_STEP if t_len % (ATTN_BLOCKS_PER_STEP * tq) == 0 else 1
        past_chunks0, chunks_per_step, n_fm = 0, n_sub, n_sub + 2
    else:
        tq, n_sub = t_len, 1
        assert k_past.shape[-1] % KEY_CHUNK == 0
        past_chunks0, chunks_per_step, n_fm = k_past.shape[-1] // KEY_CHUNK, 0, 2
    assert t_len % (n_sub * tq) == 0 and tq <= KEY_CHUNK and tq & (tq - 1) == 0
    n_steps = t_len // (n_sub * tq)
    assert past_is_own or n_steps == 1

    def newer(n):
        u = lax.broadcasted_iota(jnp.int32, (n, n), 0) > lax.broadcasted_iota(jnp.int32, (n, n), 1)
        once = jnp.concatenate([jnp.ones((n, LANES), _BF16), u.astype(_BF16)], axis=1)
        return jnp.concatenate([once, once], axis=0)
    u_diag, u_past = newer(tq), newer(KEY_CHUNK)

    def fm_chunk(m):
        imap = lambda b, i: (layer, b, 0, jnp.maximum(past_chunks0 + i * chunks_per_step + m - 2, 0))
        return pl.BlockSpec((None, None, ATTN_W, KEY_CHUNK), imap)

    rm_block = pl.BlockSpec((None, n_sub * tq, ATTN_W), lambda b, i: (b, i, 0))
    const = lambda arr: pl.BlockSpec(arr.shape, lambda b, i: (0,) * arr.ndim)
    any_spec = pl.BlockSpec(memory_space=pl.ANY)
    fm_specs = [fm_chunk(m) for m in range(n_fm)]
    args = [q] + [k_past] * n_fm + [v_past] * n_fm
    in_specs = [rm_block] + fm_specs + fm_specs
    if not past_is_own:
        args += [k_own, v_own]
        in_specs += [rm_block, rm_block]
    args += [k_past, v_past, u_diag, u_past]
    in_specs += [any_spec, any_spec, const(u_diag), const(u_past)]
    return pl.pallas_call(
        functools.partial(_attn_kernel, n_sub=n_sub, past_chunks0=past_chunks0, chunks_per_step=chunks_per_step,
                          layer=layer, own_feature_major=past_is_own),
        out_shape=jax.ShapeDtypeStruct((batch, t_len, ATTN_W), _BF16),
        grid=(batch, n_steps),
        in_specs=in_specs,
        out_specs=rm_block,
        scratch_shapes=[pltpu.VMEM((n_sub * tq, ATTN_W), _F32),
                        pltpu.VMEM((n_sub, N_HEADS * tq, LANES), _F32),
                        pltpu.VMEM((ATTN_W, KEY_CHUNK), _F32),
                        pltpu.VMEM((ATTN_W, KEY_CHUNK), _F32),
                        pltpu.SemaphoreType.DMA((2,))],
        compiler_params=pltpu.CompilerParams(dimension_semantics=("arbitrary", "arbitrary"),
                                             vmem_limit_bytes=VMEM_LIMIT_BYTES),
        name="stick_attention",
    )(*args)


def _outffn_kernel(xa_ref, ca_ref, oa_ref, xb_ref, cb_ref, ob_ref, wout_any, gff_ref, w1_any, w2_any,
                   ya_ref, yb_ref, wout_ref, w1_ref, w2_ref, stage_wide_ref, stage_ref, sem_ref, *, n_first, layer):
    d_ff = w1_ref.shape[1]
    r = pl.program_id(0)

    def load_as_bf16(src_any, dst_ref, buf_ref):
        slots, rows = buf_ref.shape[0], buf_ref.shape[1]
        n = dst_ref.shape[0] // rows

        def chunk_copy(i):
            return pltpu.make_async_copy(src_any.at[layer, pl.ds(i * rows, rows), :], buf_ref.at[i % slots],
                                         sem_ref.at[i % slots])
        for i in range(min(slots - 1, n)):
            chunk_copy(i).start()
        for i in range(n):
            if i + slots - 1 < n:
                chunk_copy(i + slots - 1).start()
            chunk_copy(i).wait()
            dst_ref[i * rows:(i + 1) * rows, :] = buf_ref[i % slots].astype(_BF16)

    @pl.when(r == 0)
    def _():
        load_as_bf16(wout_any, wout_ref, stage_ref)
        load_as_bf16(w1_any, w1_ref, stage_wide_ref)
        load_as_bf16(w2_any, w2_ref, stage_ref)

    def mix_and_ffn(x_ref, c_ref, o_ref, y_ref):
        x1 = x_ref[...] + _dot(c_ref[...], wout_ref[0:CONV_CH, :]) + _dot(o_ref[...], wout_ref[CONV_CH:, :])
        h = x1 * lax.rsqrt(jnp.mean(x1 * x1, axis=-1, keepdims=True) + EPS) * gff_ref[...]
        h = h.astype(_BF16)
        ffn = None
        for cc in range(d_ff // FF_CHUNK):
            f = jnp.maximum(_dot(h, w1_ref[:, cc * FF_CHUNK:(cc + 1) * FF_CHUNK]), 0.0)
            part = _dot((f * f).astype(_BF16), w2_ref[cc * FF_CHUNK:(cc + 1) * FF_CHUNK, :])
            ffn = part if ffn is None else ffn + part
        y_ref[...] = x1 + ffn

    @pl.when(r < n_first)
    def _():
        mix_and_ffn(xa_ref, ca_ref, oa_ref, ya_ref)

    @pl.when(r >= n_first)
    def _():
        mix_and_ffn(xb_ref, cb_ref, ob_ref, yb_ref)


def _outffn(xa, ca, oa, xb, cb, ob, wout, gff, w1, w2, *, layer):
    d_model = xa.shape[1]
    d_ff = w1.shape[-1]
    tb = ROW_BLOCK
    assert xa.shape[0] % tb == 0 and xb.shape[0] % tb == 0 and d_ff % FF_CHUNK == 0
    assert d_model % WEIGHT_STAGE_ROWS == 0 and d_ff % WEIGHT_STAGE_ROWS == 0
    na, nb = xa.shape[0] // tb, xb.shape[0] // tb
    first = lambda width: pl.BlockSpec((tb, width), lambda r: (jnp.minimum(r, na - 1), 0))
    second = lambda width: pl.BlockSpec((tb, width), lambda r: (jnp.maximum(r - na, 0), 0))
    any_spec = pl.BlockSpec(memory_space=pl.ANY)
    gff_spec = pl.BlockSpec((None,) + gff.shape[1:], lambda r: (layer, 0, 0))
    wide_rows = WEIGHT_STAGE_ROWS * d_model // d_ff
    return pl.pallas_call(
        functools.partial(_outffn_kernel, n_first=na, layer=layer),
        out_shape=(jax.ShapeDtypeStruct(xa.shape, _F32), jax.ShapeDtypeStruct(xb.shape, _F32)),
        grid=(na + nb,),
        in_specs=[first(d_model), first(CONV_CH), first(ATTN_W), second(d_model), second(CONV_CH), second(ATTN_W),
                  any_spec, gff_spec, any_spec, any_spec],
        out_specs=(first(d_model), second(d_model)),
        scratch_shapes=[pltpu.VMEM(wout.shape[1:], _BF16), pltpu.VMEM(w1.shape[1:], _BF16),
                        pltpu.VMEM(w2.shape[1:], _BF16),
                        pltpu.VMEM((WEIGHT_STAGE_SLOTS, wide_rows, d_ff), _F32),
                        pltpu.VMEM((WEIGHT_STAGE_SLOTS, WEIGHT_STAGE_ROWS, d_model), _F32),
                        pltpu.SemaphoreType.DMA((WEIGHT_STAGE_SLOTS,))],
        compiler_params=pltpu.CompilerParams(dimension_semantics=("arbitrary",),
                                             vmem_limit_bytes=VMEM_LIMIT_BYTES),
        name="outproj_ffn",
    )(xa, ca, oa, xb, cb, ob, wout, gff, w1, w2)


def kernel(x_prompt, x_sample, cache_k, cache_v, state_conv, g_mix, w_in, w_dw, b_dw, g_cn, b_cn, w_pw,
           g_q, g_k, w_out, g_ff, w_ff1, w_ff2):
    depth = w_in.shape[0]
    bp, tp, _ = x_prompt.shape
    bs, ts, _ = x_sample.shape
    n_past = cache_k.shape[2]
    head_id = lax.broadcasted_iota(jnp.int32, (ATTN_W, ATTN_W), 0) // HEAD_DIM
    gmat = (head_id == head_id.T).astype(_BF16)
    zero_hist = jnp.zeros((1, bp, HIST_PAD, CONV_CH), _F32)
    sample_hist = jnp.pad(state_conv, ((0, 0), (0, 0), (HIST_PAD - HIST, 0), (0, 0)))
    row = lambda a: a.reshape(depth, 1, -1)
    per_head = lambda g: jnp.tile(g, (1, N_HEADS))
    win = w_in.astype(_BF16)
    w = dict(gmix=row(g_mix), win=win, w_kv_t=jnp.transpose(win[:, :, 2 * CONV_CH + ATTN_W:], (0, 2, 1)),
             wdw=w_dw, bdw=row(b_dw), gcn=row(g_cn), bcn=row(b_cn), wpw=w_pw.astype(_BF16),
             gq=row(per_head(g_q)), gk=row(per_head(g_k)),
             gk_col=jnp.broadcast_to(per_head(g_k)[:, :, None], (depth, ATTN_W, LANES)), gmat=gmat,
             wout=w_out, gff=row(g_ff), w1=w_ff1, w2=w_ff2)
    to_fm = lambda a: jnp.transpose(a, (0, 1, 3, 4, 2)).reshape(a.shape[0], a.shape[1], ATTN_W, a.shape[2])
    from_fm = lambda a: jnp.transpose(a.reshape(a.shape[0], a.shape[1], N_HEADS, HEAD_DIM, a.shape[3]),
                                      (0, 1, 4, 2, 3))
    cache_k_fm, cache_v_fm = to_fm(cache_k), to_fm(cache_v)

    hp, hs = x_prompt, x_sample
    kv_stack = (None, None)
    conv_p, k_s, v_s, conv_s = [], [], [], []
    flat = lambda a: a.reshape(-1, a.shape[-1])
    for l in range(depth):
        cp, q, k_fm, v_fm, cv = _inproj(hp, zero_hist, 0, w, layer=l, n_seg=1, seg_rows=min(ROW_BLOCK, tp),
                                        kv_stack=kv_stack)
        kv_stack = (k_fm, v_fm)
        op = _attention(q, None, None, k_fm, v_fm, layer=l, past_is_own=True)
        conv_p.append(cv)
        cs, q, k, v, cv = _inproj(hs, sample_hist, l, w, layer=l, n_seg=bs, seg_rows=ts)
        os_ = _attention(q, k, v, cache_k_fm, cache_v_fm, layer=l, past_is_own=False)
        k_s.append(k.reshape(bs, ts, N_HEADS, HEAD_DIM))
        v_s.append(v.reshape(bs, ts, N_HEADS, HEAD_DIM))
        conv_s.append(cv)
        yp, ys = _outffn(flat(hp), flat(cp), flat(op), flat(hs), flat(cs), flat(os_),
                         w["wout"], w["gff"], w["w1"], w["w2"], layer=l)
        hp, hs = yp.reshape(hp.shape), ys.reshape(hs.shape)
    return (hp, hs, from_fm(kv_stack[0]), from_fm(kv_stack[1]), jnp.stack(conv_p),
            jnp.stack(k_s), jnp.stack(v_s), jnp.stack(conv_s))
```

```python
import functools

import jax
import jax.numpy as jnp
from jax import lax
from jax.experimental import pallas as pl
from jax.experimental.pallas import tpu as pltpu

EPS = 1e-6
CONV_WIDTH = 31
HIST = CONV_WIDTH - 1
HIST_PAD = 32
SUBLANES = 8
N_HEADS = 8
HEAD_DIM = 64
ATTN_W = N_HEADS * HEAD_DIM
CONV_CH = 512
LANES = 128
KEY_CHUNK = 128
CONV_ROWS = 64
FF_CHUNK = 1024
ROW_BLOCK = 512
WEIGHT_STAGE_ROWS = 256
WEIGHT_STAGE_SLOTS = 4
VMEM_LIMIT_BYTES = 56 * 1024 * 1024
ATTN_BLOCKS_PER_STEP = 4
STICK_LOG_FLOOR = -87.5
MASKED_SCORE = -1e30

_F32 = jnp.float32
_BF16 = jnp.bfloat16


def _dot(a, b):
    return jnp.dot(a, b, preferred_element_type=_F32)


def _dot_nt(a, b):
    return lax.dot_general(a, b, (((1,), (1,)), ((), ())), preferred_element_type=_F32)


def _dot_split2(x, m):
    hi = x.astype(_BF16)
    lo = (x - hi.astype(_F32)).astype(_BF16)
    return _dot(hi, m) + _dot(lo, m)


def _inproj_kernel(*refs, n_seg, seg_rows, feature_major_kv, first_layer):
    if feature_major_kv:
        (x_ref, hist_ref, gmix_ref, win_ref, wkv_ref, wdw_ref, bdw_ref, gcn_ref, bcn_ref, wpw_ref,
         gq_ref, gk_ref, gmat_ref, _, _,
         c_ref, q_ref, k_ref, v_ref, nconv_ref, uext_ref, cs_ref, act_ref) = refs
        if first_layer:
            for stack_ref in (k_ref, v_ref):
                if stack_ref.shape[0] > 1:
                    stack_ref[1:] = jnp.zeros((stack_ref.shape[0] - 1,) + stack_ref.shape[1:], _F32)
            k_ref, v_ref = k_ref.at[0], v_ref.at[0]
    else:
        (x_ref, hist_ref, gmix_ref, win_ref, wdw_ref, bdw_ref, gcn_ref, bcn_ref, wpw_ref,
         gq_ref, gk_ref, gmat_ref,
         c_ref, q_ref, k_ref, v_ref, nconv_ref, uext_ref, cs_ref, act_ref) = refs
    t = pl.program_id(1)
    n_t = pl.num_programs(1)
    rows = n_seg * seg_rows
    d_model = x_ref.shape[-1]

    x = x_ref[...].reshape(rows, d_model)
    h = x * lax.rsqrt(jnp.mean(x * x, axis=-1, keepdims=True) + EPS) * gmix_ref[...]
    h = h.astype(_BF16)

    a = _dot(h, win_ref[:, 0:CONV_CH])
    gate = _dot(h, win_ref[:, CONV_CH:2 * CONV_CH])
    u = a * jax.nn.sigmoid(gate)

    lane_sl = [slice(g * LANES, (g + 1) * LANES) for g in range(CONV_CH // LANES)]
    u3 = u.reshape(n_seg, seg_rows, CONV_CH)

    @pl.when(t == 0)
    def _():
        for g, ls in enumerate(lane_sl):
            uext_ref[:, g, 0:HIST_PAD, :] = hist_ref[:, :, ls]

    for g, ls in enumerate(lane_sl):
        uext_ref[:, g, HIST_PAD:HIST_PAD + seg_rows, :] = u3[:, :, ls]

    half = CONV_ROWS // 2
    for s in range(n_seg):
        for r in range(seg_rows // CONV_ROWS):
            r0 = r * CONV_ROWS
            acc_even = jnp.zeros((half, CONV_CH), _F32)
            acc_odd = jnp.zeros((half, CONV_CH), _F32)
            for m in range(CONV_WIDTH + 1):
                start = r0 + m + (HIST_PAD - HIST)
                window = jnp.concatenate(
                    [uext_ref[s, g, pl.ds(start, half, stride=2), :] for g in range(len(lane_sl))], axis=1)
                if m < CONV_WIDTH:
                    acc_even = acc_even + wdw_ref[m:m + 1, :] * window
                if m >= 1:
                    acc_odd = acc_odd + wdw_ref[m - 1:m, :] * window
            for par, acc in enumerate((acc_even, acc_odd)):
                cv = acc + bdw_ref[...]
                mu = jnp.mean(cv, axis=-1, keepdims=True)
                xc = cv - mu
                y = xc * lax.rsqrt(jnp.mean(xc * xc, axis=-1, keepdims=True) + EPS)
                y = y * gcn_ref[...] + bcn_ref[...]
                y = y * jax.nn.sigmoid(y)
                base = s * seg_rows + r0 + par * half
                act_ref[base:base + half, :] = y.astype(_BF16)

    c_perm = _dot(act_ref[...], wpw_ref[...])
    for s in range(n_seg):
        for r in range(seg_rows // CONV_ROWS):
            r0 = r * CONV_ROWS
            base = s * seg_rows + r0
            for par in range(2):
                rows_par = c_perm[base + par * half:base + (par + 1) * half]
                for g, ls in enumerate(lane_sl):
                    cs_ref[s, g, pl.ds(r0 + par, half, stride=2), :] = rows_par[:, ls]
    for g, ls in enumerate(lane_sl):
        c_ref[:, :, ls] = cs_ref[:, g, :, :].astype(_BF16)

    @pl.when(t == n_t - 1)
    def _():
        for g, ls in enumerate(lane_sl):
            nconv_ref[:, :, ls] = uext_ref[:, g, seg_rows + (HIST_PAD - HIST):seg_rows + HIST_PAD, :]

    uext_ref[:, :, 0:HIST_PAD, :] = uext_ref[:, :, seg_rows:seg_rows + HIST_PAD, :]

    def head_norm(z, g_ref):
        ssq = _dot_split2(z * z, gmat_ref[...])
        return z * lax.rsqrt(ssq * (1.0 / HEAD_DIM) + EPS) * g_ref[...]

    q = _dot(h, win_ref[:, 2 * CONV_CH:2 * CONV_CH + ATTN_W])
    ssq_q = _dot((q * q).astype(_BF16), gmat_ref[...])
    q = q * lax.rsqrt(ssq_q * (1.0 / HEAD_DIM) + EPS) * gq_ref[...]
    q_ref[...] = (q * (HEAD_DIM ** -0.5)).astype(_BF16).reshape(q_ref.shape)
    if feature_major_kv:
        k_t = _dot_nt(wkv_ref[0:ATTN_W, :], h)
        gain = jnp.concatenate([gk_ref[...]] * (rows // LANES), axis=1)
        for hh in range(N_HEADS):
            rs = slice(hh * HEAD_DIM, (hh + 1) * HEAD_DIM)
            kh = k_t[rs]
            ssq = jnp.sum(kh * kh, axis=0, keepdims=True)
            k_ref[rs, :] = kh * lax.rsqrt(ssq * (1.0 / HEAD_DIM) + EPS) * gain[rs]
        v_ref[...] = _dot_nt(wkv_ref[ATTN_W:, :], h)
    else:
        k = _dot(h, win_ref[:, 2 * CONV_CH + ATTN_W:2 * CONV_CH + 2 * ATTN_W])
        k_ref[...] = head_norm(k, gk_ref).reshape(k_ref.shape)
        v_ref[...] = _dot(h, win_ref[:, 2 * CONV_CH + 2 * ATTN_W:]).reshape(v_ref.shape)


def _inproj(x, hist, hist_layer, w, *, layer, n_seg, seg_rows, kv_stack=None):
    batch, t_len, d_model = x.shape
    depth = w["win"].shape[0]
    assert batch % n_seg == 0 and t_len % seg_rows == 0 and seg_rows % CONV_ROWS == 0
    n_t = t_len // seg_rows
    assert n_seg == 1 or n_t == 1
    feature_major_kv = kv_stack is not None
    rows_spec = lambda width: pl.BlockSpec((n_seg, seg_rows, width), lambda b, t: (b, t, 0))
    shared = lambda arr: pl.BlockSpec(arr.shape, lambda b, t: (0,) * arr.ndim)
    const = lambda arr: pl.BlockSpec((None,) + arr.shape[1:], lambda b, t: (layer,) + (0,) * (arr.ndim - 1))
    hist_spec = pl.BlockSpec((None, n_seg, HIST_PAD, CONV_CH), lambda b, t: (hist_layer, b, 0, 0))
    nconv_spec = pl.BlockSpec((n_seg, HIST, CONV_CH), lambda b, t: (b, 0, 0))
    small = [w["wdw"], w["bdw"], w["gcn"], w["bcn"], w["wpw"]]
    aliases = {}
    if feature_major_kv:
        assert n_seg == 1 and seg_rows % LANES == 0
        args = [x, hist, w["gmix"], w["win"], w["w_kv_t"]] + small + [w["gq"], w["gk_col"], w["gmat"]]
        in_specs = [rows_spec(d_model), hist_spec] + [const(a) for a in args[2:-1]] + [shared(w["gmat"])]
        kv_shape = jax.ShapeDtypeStruct((depth, batch, ATTN_W, t_len), _F32)
        first_layer = kv_stack[0] is None
        if first_layer:
            kv_spec = pl.BlockSpec((depth, None, ATTN_W, seg_rows), lambda b, t: (0, b, 0, t))
            dummy = jnp.zeros((SUBLANES, LANES), _F32)
            args += [dummy, dummy]
            in_specs += [shared(dummy)] * 2
        else:
            kv_spec = pl.BlockSpec((None, None, ATTN_W, seg_rows), lambda b, t: (layer, b, 0, t))
            aliases = {len(args): 2, len(args) + 1: 3}
            args += list(kv_stack)
            in_specs += [pl.BlockSpec(memory_space=pl.ANY)] * 2
    else:
        args = [x, hist, w["gmix"], w["win"]] + small + [w["gq"], w["gk"], w["gmat"]]
        in_specs = [rows_spec(d_model), hist_spec] + [const(a) for a in args[2:-1]] + [shared(w["gmat"])]
        kv_shape = jax.ShapeDtypeStruct((batch, t_len, ATTN_W), _F32)
        kv_spec = rows_spec(ATTN_W)
        first_layer = False
    out_shape = (
        jax.ShapeDtypeStruct((batch, t_len, CONV_CH), _BF16),
        jax.ShapeDtypeStruct((batch, t_len, ATTN_W), _BF16),
        kv_shape, kv_shape,
        jax.ShapeDtypeStruct((batch, HIST, CONV_CH), _F32),
    )
    return pl.pallas_call(
        functools.partial(_inproj_kernel, n_seg=n_seg, seg_rows=seg_rows, feature_major_kv=feature_major_kv,
                          first_layer=first_layer),
        out_shape=out_shape,
        grid=(batch // n_seg, n_t),
        in_specs=in_specs,
        out_specs=(rows_spec(CONV_CH), rows_spec(ATTN_W), kv_spec, kv_spec, nconv_spec),
        scratch_shapes=[pltpu.VMEM((n_seg, CONV_CH // LANES, seg_rows + HIST_PAD, LANES), _F32),
                        pltpu.VMEM((n_seg, CONV_CH // LANES, seg_rows, LANES), _F32),
                        pltpu.VMEM((n_seg * seg_rows, CONV_CH), _BF16)],
        input_output_aliases=aliases,
        compiler_params=pltpu.CompilerParams(dimension_semantics=("arbitrary", "arbitrary"),
                                             vmem_limit_bytes=VMEM_LIMIT_BYTES),
        name="inproj_conv",
    )(*args)


def _attn_kernel(*refs, n_sub, past_chunks0, chunks_per_step, layer, own_feature_major):
    n_fm = n_sub + 2 if own_feature_major else 2
    q_ref, k_fm, v_fm, rest = refs[0], refs[1:1 + n_fm], refs[1 + n_fm:1 + 2 * n_fm], refs[1 + 2 * n_fm:]
    if not own_feature_major:
        kd_ref, vd_ref, rest = rest[0], rest[1], rest[2:]
    kany_ref, vany_ref, ud_ref, up_ref, o_ref, oacc_ref, run_ref, kbuf_ref, vbuf_ref, sem_ref = rest
    b = pl.program_id(0)
    i = pl.program_id(1)
    tq = q_ref.shape[0] // n_sub
    n_pairs = ATTN_W // LANES
    subs = range(n_sub)
    n_past = [past_chunks0 + i * chunks_per_step + j for j in subs]
    even_head = lax.broadcasted_iota(jnp.int32, (tq, LANES), 1) < HEAD_DIM
    pair_sl = [slice(p * LANES, (p + 1) * LANES) for p in range(n_pairs)]

    def stacked_queries(j):
        qs = []
        for sl in pair_sl:
            qp = q_ref[j * tq:(j + 1) * tq, sl]
            qs.append(jnp.concatenate([jnp.where(even_head, qp, 0.0), jnp.where(even_head, 0.0, qp)], axis=0))
        return qs

    def scores_fm(qs, key_refs):
        out = []
        for p, sl in enumerate(pair_sl):
            keys = jnp.concatenate([r[sl, :] for r in key_refs], axis=1).astype(_BF16)
            out.append(_dot(qs[p], keys))
        return jnp.concatenate(out, axis=0)

    def log_terms(z):
        sp = jnp.log(1.0 + jnp.exp(-jnp.abs(z)))
        neg_part = jnp.minimum(z, 0.0)
        return neg_part - sp, (neg_part - z) - sp

    def suffix_sums(log_1m, u_ref):
        hi = log_1m.astype(_BF16)
        lo = (log_1m - hi.astype(_F32)).astype(_BF16)
        r = _dot(jnp.concatenate([hi, lo], axis=1), u_ref[...])
        return r[:, LANES:], r[:, :LANES]

    def weighted_values(fm_weights, fm_value_refs, rm_weight=None, rm_value_ref=None):
        wb = jnp.concatenate(fm_weights, axis=1).astype(_BF16)
        outs = []
        for p, sl in enumerate(pair_sl):
            rows = slice(2 * tq * p, 2 * tq * (p + 1))
            vals = jnp.concatenate([r[sl, :] for r in fm_value_refs], axis=1).astype(_BF16)
            pv = _dot_nt(wb[rows], vals)
            if rm_weight is not None:
                pv = pv + _dot(rm_weight[rows].astype(_BF16), rm_value_ref[:, sl].astype(_BF16))
            outs.append(jnp.where(even_head, pv[:tq], pv[tq:]))
        return jnp.concatenate(outs, axis=1)

    qs = [stacked_queries(j) for j in subs]
    if own_feature_major:
        z = [scores_fm(qs[j], k_fm[j:j + 3]) for j in subs]
        z_own = [zj[:, 2 * KEY_CHUNK:] for zj in z]
    else:
        z = [scores_fm(qs[0], k_fm)]
        z_own = [jnp.concatenate([_dot_nt(qs[0][p], kd_ref[:, sl].astype(_BF16))
                                  for p, sl in enumerate(pair_sl)], axis=0)]
    q_row = lax.broadcasted_iota(jnp.int32, (N_HEADS * tq, tq), 0) & (tq - 1)
    causal = lax.broadcasted_iota(jnp.int32, (N_HEADS * tq, tq), 1) < q_row
    z_2 = [z[j][:, 0:KEY_CHUNK] + jnp.where(n_past[j] >= 2, 0.0, MASKED_SCORE) for j in subs]
    z_1 = [z[j][:, KEY_CHUNK:2 * KEY_CHUNK] + jnp.where(n_past[j] >= 1, 0.0, MASKED_SCORE) for j in subs]
    z_d = [jnp.where(causal, z_own[j], MASKED_SCORE) for j in subs]
    t_d = [log_terms(zz) for zz in z_d]
    t_1 = [log_terms(zz) for zz in z_1]
    t_2 = [log_terms(zz) for zz in z_2]
    s_d = [suffix_sums(t_d[j][1], ud_ref) for j in subs]
    s_1 = [suffix_sums(t_1[j][1], up_ref) for j in subs]
    s_2 = [suffix_sums(t_2[j][1], up_ref) for j in subs]
    a_d = [jnp.exp(t_d[j][0] + s_d[j][0]) for j in subs]
    a_1 = [jnp.exp(t_1[j][0] + (s_1[j][0] + s_d[j][1])) for j in subs]
    run_1 = [s_d[j][1] + s_1[j][1] for j in subs]
    a_2 = [jnp.exp(t_2[j][0] + (s_2[j][0] + run_1[j])) for j in subs]
    run = [run_1[j] + s_2[j][1] for j in subs]
    for j in subs:
        if own_feature_major:
            out = weighted_values((a_2[j], a_1[j], a_d[j]), v_fm[j:j + 3])
        else:
            out = weighted_values((a_2[j], a_1[j]), v_fm, a_d[j], vd_ref)
        oacc_ref[j * tq:(j + 1) * tq, :] = out
        run_ref[j] = run[j]

    def more(carry):
        c, m = carry
        return jnp.logical_and(c >= 0, m > STICK_LOG_FLOOR)

    def older(j, carry):
        c, _ = carry
        start = pl.multiple_of(c * KEY_CHUNK, KEY_CHUNK)
        k_copy = pltpu.make_async_copy(kany_ref.at[layer, b, :, pl.ds(start, KEY_CHUNK)], kbuf_ref, sem_ref.at[0])
        v_copy = pltpu.make_async_copy(vany_ref.at[layer, b, :, pl.ds(start, KEY_CHUNK)], vbuf_ref, sem_ref.at[1])
        k_copy.start()
        v_copy.start()
        k_copy.wait()
        v_copy.wait()
        lb, l1 = log_terms(scores_fm(stacked_queries(j), (kbuf_ref,)))
        inside, tot = suffix_sums(l1, up_ref)
        run_c = run_ref[j]
        oacc_ref[j * tq:(j + 1) * tq, :] += weighted_values((jnp.exp(lb + (inside + run_c)),), (vbuf_ref,))
        run_c = run_c + tot
        run_ref[j] = run_c
        return c - 1, jnp.max(run_c)

    for j in subs:
        lax.while_loop(more, functools.partial(older, j), (n_past[j] - 3, jnp.max(run[j])))

    o_ref[...] = oacc_ref[...].astype(o_ref.dtype)


def _attention(q, k_own, v_own, k_past, v_past, *, layer, past_is_own):
    batch, t_len, _ = q.shape
    if past_is_own:
        tq = KEY_CHUNK
        n_sub = ATTN_BLOCKS_PER_STEP if t_len % (ATTN_BLOCKS_PER_STEP * tq) == 0 else 1
        past_chunks0, chunks_per_step, n_fm = 0, n_sub, n_sub + 2
    else:
        tq, n_sub = t_len, 1
        assert k_past.shape[-1] % KEY_CHUNK == 0
        past_chunks0, chunks_per_step, n_fm = k_past.shape[-1] // KEY_CHUNK, 0, 2
    assert t_len % (n_sub * tq) == 0 and tq <= KEY_CHUNK and tq & (tq - 1) == 0
    n_steps = t_len // (n_sub * tq)
    assert past_is_own or n_steps == 1

    def newer(n):
        u = lax.broadcasted_iota(jnp.int32, (n, n), 0) > lax.broadcasted_iota(jnp.int32, (n, n), 1)
        once = jnp.concatenate([jnp.ones((n, LANES), _BF16), u.astype(_BF16)], axis=1)
        return jnp.concatenate([once, once], axis=0)
    u_diag, u_past = newer(tq), newer(KEY_CHUNK)

    def fm_chunk(m):
        imap = lambda b, i: (layer, b, 0, jnp.maximum(past_chunks0 + i * chunks_per_step + m - 2, 0))
        return pl.BlockSpec((None, None, ATTN_W, KEY_CHUNK), imap)

    rm_block = pl.BlockSpec((None, n_sub * tq, ATTN_W), lambda b, i: (b, i, 0))
    const = lambda arr: pl.BlockSpec(arr.shape, lambda b, i: (0,) * arr.ndim)
    any_spec = pl.BlockSpec(memory_space=pl.ANY)
    fm_specs = [fm_chunk(m) for m in range(n_fm)]
    args = [q] + [k_past] * n_fm + [v_past] * n_fm
    in_specs = [rm_block] + fm_specs + fm_specs
    if not past_is_own:
        args += [k_own, v_own]
        in_specs += [rm_block, rm_block]
    args += [k_past, v_past, u_diag, u_past]
    in_specs += [any_spec, any_spec, const(u_diag), const(u_past)]
    return pl.pallas_call(
        functools.partial(_attn_kernel, n_sub=n_sub, past_chunks0=past_chunks0, chunks_per_step=chunks_per_step,
                          layer=layer, own_feature_major=past_is_own),
        out_shape=jax.ShapeDtypeStruct((batch, t_len, ATTN_W), _BF16),
        grid=(batch, n_steps),
        in_specs=in_specs,
        out_specs=rm_block,
        scratch_shapes=[pltpu.VMEM((n_sub * tq, ATTN_W), _F32),
                        pltpu.VMEM((n_sub, N_HEADS * tq, LANES), _F32),
                        pltpu.VMEM((ATTN_W, KEY_CHUNK), _F32),
                        pltpu.VMEM((ATTN_W, KEY_CHUNK), _F32),
                        pltpu.SemaphoreType.DMA((2,))],
        compiler_params=pltpu.CompilerParams(dimension_semantics=("arbitrary", "arbitrary"),
                                             vmem_limit_bytes=VMEM_LIMIT_BYTES),
        name="stick_attention",
    )(*args)


def _outffn_kernel(xa_ref, ca_ref, oa_ref, xb_ref, cb_ref, ob_ref, wout_any, gff_ref, w1_any, w2_any,
                   ya_ref, yb_ref, wout_ref, w1_ref, w2_ref, stage_wide_ref, stage_ref, sem_ref, *, n_first, layer):
    d_ff = w1_ref.shape[1]
    r = pl.program_id(0)

    def load_as_bf16(src_any, dst_ref, buf_ref):
        slots, rows = buf_ref.shape[0], buf_ref.shape[1]
        n = dst_ref.shape[0] // rows

        def chunk_copy(i):
            return pltpu.make_async_copy(src_any.at[layer, pl.ds(i * rows, rows), :], buf_ref.at[i % slots],
                                         sem_ref.at[i % slots])
        for i in range(min(slots - 1, n)):
            chunk_copy(i).start()
        for i in range(n):
            if i + slots - 1 < n:
                chunk_copy(i + slots - 1).start()
            chunk_copy(i).wait()
            dst_ref[i * rows:(i + 1) * rows, :] = buf_ref[i % slots].astype(_BF16)

    @pl.when(r == 0)
    def _():
        load_as_bf16(wout_any, wout_ref, stage_ref)
        load_as_bf16(w1_any, w1_ref, stage_wide_ref)
        load_as_bf16(w2_any, w2_ref, stage_ref)

    def mix_and_ffn(x_ref, c_ref, o_ref, y_ref):
        x1 = x_ref[...] + _dot(c_ref[...], wout_ref[0:CONV_CH, :]) + _dot(o_ref[...], wout_ref[CONV_CH:, :])
        h = x1 * lax.rsqrt(jnp.mean(x1 * x1, axis=-1, keepdims=True) + EPS) * gff_ref[...]
        h = h.astype(_BF16)
        ffn = None
        for cc in range(d_ff // FF_CHUNK):
            f = jnp.maximum(_dot(h, w1_ref[:, cc * FF_CHUNK:(cc + 1) * FF_CHUNK]), 0.0)
            part = _dot((f * f).astype(_BF16), w2_ref[cc * FF_CHUNK:(cc + 1) * FF_CHUNK, :])
            ffn = part if ffn is None else ffn + part
        y_ref[...] = x1 + ffn

    @pl.when(r < n_first)
    def _():
        mix_and_ffn(xa_ref, ca_ref, oa_ref, ya_ref)

    @pl.when(r >= n_first)
    def _():
        mix_and_ffn(xb_ref, cb_ref, ob_ref, yb_ref)


def _outffn(xa, ca, oa, xb, cb, ob, wout, gff, w1, w2, *, layer):
    d_model = xa.shape[1]
    d_ff = w1.shape[-1]
    tb = ROW_BLOCK
    assert xa.shape[0] % tb == 0 and xb.shape[0] % tb == 0 and d_ff % FF_CHUNK == 0
    assert d_model % WEIGHT_STAGE_ROWS == 0 and d_ff % WEIGHT_STAGE_ROWS == 0
    na, nb = xa.shape[0] // tb, xb.shape[0] // tb
    first = lambda width: pl.BlockSpec((tb, width), lambda r: (jnp.minimum(r, na - 1), 0))
    second = lambda width: pl.BlockSpec((tb, width), lambda r: (jnp.maximum(r - na, 0), 0))
    any_spec = pl.BlockSpec(memory_space=pl.ANY)
    gff_spec = pl.BlockSpec((None,) + gff.shape[1:], lambda r: (layer, 0, 0))
    wide_rows = WEIGHT_STAGE_ROWS * d_model // d_ff
    return pl.pallas_call(
        functools.partial(_outffn_kernel, n_first=na, layer=layer),
        out_shape=(jax.ShapeDtypeStruct(xa.shape, _F32), jax.ShapeDtypeStruct(xb.shape, _F32)),
        grid=(na + nb,),
        in_specs=[first(d_model), first(CONV_CH), first(ATTN_W), second(d_model), second(CONV_CH), second(ATTN_W),
                  any_spec, gff_spec, any_spec, any_spec],
        out_specs=(first(d_model), second(d_model)),
        scratch_shapes=[pltpu.VMEM(wout.shape[1:], _BF16), pltpu.VMEM(w1.shape[1:], _BF16),
                        pltpu.VMEM(w2.shape[1:], _BF16),
                        pltpu.VMEM((WEIGHT_STAGE_SLOTS, wide_rows, d_ff), _F32),
                        pltpu.VMEM((WEIGHT_STAGE_SLOTS, WEIGHT_STAGE_ROWS, d_model), _F32),
                        pltpu.SemaphoreType.DMA((WEIGHT_STAGE_SLOTS,))],
        compiler_params=pltpu.CompilerParams(dimension_semantics=("arbitrary",),
                                             vmem_limit_bytes=VMEM_LIMIT_BYTES),
        name="outproj_ffn",
    )(xa, ca, oa, xb, cb, ob, wout, gff, w1, w2)


def kernel(x_prompt, x_sample, cache_k, cache_v, state_conv, g_mix, w_in, w_dw, b_dw, g_cn, b_cn, w_pw,
           g_q, g_k, w_out, g_ff, w_ff1, w_ff2):
    depth = w_in.shape[0]
    bp, tp, _ = x_prompt.shape
    bs, ts, _ = x_sample.shape
    n_past = cache_k.shape[2]
    head_id = lax.broadcasted_iota(jnp.int32, (ATTN_W, ATTN_W), 0) // HEAD_DIM
    gmat = (head_id == head_id.T).astype(_BF16)
    zero_hist = jnp.zeros((1, bp, HIST_PAD, CONV_CH), _F32)
    sample_hist = jnp.pad(state_conv, ((0, 0), (0, 0), (HIST_PAD - HIST, 0), (0, 0)))
    row = lambda a: a.reshape(depth, 1, -1)
    per_head = lambda g: jnp.tile(g, (1, N_HEADS))
    win = w_in.astype(_BF16)
    w = dict(gmix=row(g_mix), win=win, w_kv_t=jnp.transpose(win[:, :, 2 * CONV_CH + ATTN_W:], (0, 2, 1)),
             wdw=w_dw, bdw=row(b_dw), gcn=row(g_cn), bcn=row(b_cn), wpw=w_pw.astype(_BF16),
             gq=row(per_head(g_q)), gk=row(per_head(g_k)),
             gk_col=jnp.broadcast_to(per_head(g_k)[:, :, None], (depth, ATTN_W, LANES)), gmat=gmat,
             wout=w_out, gff=row(g_ff), w1=w_ff1, w2=w_ff2)
    to_fm = lambda a: jnp.transpose(a, (0, 1, 3, 4, 2)).reshape(a.shape[0], a.shape[1], ATTN_W, a.shape[2])
    from_fm = lambda a: jnp.transpose(a.reshape(a.shape[0], a.shape[1], N_HEADS, HEAD_DIM, a.shape[3]),
                                      (0, 1, 4, 2, 3))
    cache_k_fm, cache_v_fm = to_fm(cache_k), to_fm(cache_v)

    hp, hs = x_prompt, x_sample
    kv_stack = (None, None)
    conv_p, k_s, v_s, conv_s = [], [], [], []
    flat = lambda a: a.reshape(-1, a.shape[-1])
    for l in range(depth):
        cp, q, k_fm, v_fm, cv = _inproj(hp, zero_hist, 0, w, layer=l, n_seg=1, seg_rows=min(ROW_BLOCK, tp),
                                        kv_stack=kv_stack)
        kv_stack = (k_fm, v_fm)
        op = _attention(q, None, None, k_fm, v_fm, layer=l, past_is_own=True)
        conv_p.append(cv)
        cs, q, k, v, cv = _inproj(hs, sample_hist, l, w, layer=l, n_seg=bs, seg_rows=ts)
        os_ = _attention(q, k, v, cache_k_fm, cache_v_fm, layer=l, past_is_own=False)
        k_s.append(k.reshape(bs, ts, N_HEADS, HEAD_DIM))
        v_s.append(v.reshape(bs, ts, N_HEADS, HEAD_DIM))
        conv_s.append(cv)
        yp, ys = _outffn(flat(hp), flat(cp), flat(op), flat(hs), flat(cs), flat(os_),
                         w["wout"], w["gff"], w["w1"], w["w2"], layer=l)
        hp, hs = yp.reshape(hp.shape), ys.reshape(hs.shape)
    return (hp, hs, from_fm(kv_stack[0]), from_fm(kv_stack[1]), jnp.stack(conv_p),
            jnp.stack(k_s), jnp.stack(v_s), jnp.stack(conv_s))
```

```python
import functools

import jax
import jax.numpy as jnp
from jax import lax
from jax.experimental import pallas as pl
from jax.experimental.pallas import tpu as pltpu

EPS = 1e-6
CONV_WIDTH = 31
HIST = CONV_WIDTH - 1
HIST_PAD = 32
SUBLANES = 8
N_HEADS = 8
HEAD_DIM = 64
ATTN_W = N_HEADS * HEAD_DIM
CONV_CH = 512
LANES = 128
KEY_CHUNK = 128
CONV_ROWS = 64
FF_CHUNK = 1024
ROW_BLOCK = 512
WEIGHT_STAGE_ROWS = 256
WEIGHT_STAGE_SLOTS = 8
VMEM_LIMIT_BYTES = 56 * 1024 * 1024
ATTN_BLOCKS_PER_STEP = 4
STICK_LOG_FLOOR = -87.5
MASKED_SCORE = -1e30

_F32 = jnp.float32
_BF16 = jnp.bfloat16


def _dot(a, b):
    return jnp.dot(a, b, preferred_element_type=_F32)


def _dot_nt(a, b):
    return lax.dot_general(a, b, (((1,), (1,)), ((), ())), preferred_element_type=_F32)


def _dot_split2(x, m):
    hi = x.astype(_BF16)
    lo = (x - hi.astype(_F32)).astype(_BF16)
    return _dot(hi, m) + _dot(lo, m)


def _inproj_kernel(*refs, n_seg, seg_rows, feature_major_kv, first_layer):
    if feature_major_kv:
        (x_ref, hist_ref, gmix_ref, win_ref, wkv_ref, wdw_ref, bdw_ref, gcn_ref, bcn_ref, wpw_ref,
         gq_ref, gk_ref, gmat_ref, _, _,
         c_ref, q_ref, k_ref, v_ref, nconv_ref, uext_ref, cs_ref, act_ref) = refs
        if first_layer:
            for stack_ref in (k_ref, v_ref):
                if stack_ref.shape[0] > 1:
                    stack_ref[1:] = jnp.zeros((stack_ref.shape[0] - 1,) + stack_ref.shape[1:], _F32)
            k_ref, v_ref = k_ref.at[0], v_ref.at[0]
    else:
        (x_ref, hist_ref, gmix_ref, win_ref, wdw_ref, bdw_ref, gcn_ref, bcn_ref, wpw_ref,
         gq_ref, gk_ref, gmat_ref,
         c_ref, q_ref, k_ref, v_ref, nconv_ref, uext_ref, cs_ref, act_ref) = refs
    t = pl.program_id(1)
    n_t = pl.num_programs(1)
    rows = n_seg * seg_rows
    d_model = x_ref.shape[-1]

    x = x_ref[...].reshape(rows, d_model)
    h = x * lax.rsqrt(jnp.mean(x * x, axis=-1, keepdims=True) + EPS) * gmix_ref[...]
    h = h.astype(_BF16)

    a = _dot(h, win_ref[:, 0:CONV_CH])
    gate = _dot(h, win_ref[:, CONV_CH:2 * CONV_CH])
    u = a * jax.nn.sigmoid(gate)

    lane_sl = [slice(g * LANES, (g + 1) * LANES) for g in range(CONV_CH // LANES)]
    u3 = u.reshape(n_seg, seg_rows, CONV_CH)

    @pl.when(t == 0)
    def _():
        for g, ls in enumerate(lane_sl):
            uext_ref[:, g, 0:HIST_PAD, :] = hist_ref[:, :, ls]

    for g, ls in enumerate(lane_sl):
        uext_ref[:, g, HIST_PAD:HIST_PAD + seg_rows, :] = u3[:, :, ls]

    half = CONV_ROWS // 2
    for s in range(n_seg):
        for r in range(seg_rows // CONV_ROWS):
            r0 = r * CONV_ROWS
            acc_even = jnp.zeros((half, CONV_CH), _F32)
            acc_odd = jnp.zeros((half, CONV_CH), _F32)
            for m in range(CONV_WIDTH + 1):
                start = r0 + m + (HIST_PAD - HIST)
                window = jnp.concatenate(
                    [uext_ref[s, g, pl.ds(start, half, stride=2), :] for g in range(len(lane_sl))], axis=1)
                if m < CONV_WIDTH:
                    acc_even = acc_even + wdw_ref[m:m + 1, :] * window
                if m >= 1:
                    acc_odd = acc_odd + wdw_ref[m - 1:m, :] * window
            for par, acc in enumerate((acc_even, acc_odd)):
                cv = acc + bdw_ref[...]
                mu = jnp.mean(cv, axis=-1, keepdims=True)
                xc = cv - mu
                y = xc * lax.rsqrt(jnp.mean(xc * xc, axis=-1, keepdims=True) + EPS)
                y = y * gcn_ref[...] + bcn_ref[...]
                y = y * jax.nn.sigmoid(y)
                base = s * seg_rows + r0 + par * half
                act_ref[base:base + half, :] = y.astype(_BF16)

    c_perm = _dot(act_ref[...], wpw_ref[...])
    for s in range(n_seg):
        for r in range(seg_rows // CONV_ROWS):
            r0 = r * CONV_ROWS
            base = s * seg_rows + r0
            for par in range(2):
                rows_par = c_perm[base + par * half:base + (par + 1) * half]
                for g, ls in enumerate(lane_sl):
                    cs_ref[s, g, pl.ds(r0 + par, half, stride=2), :] = rows_par[:, ls]
    for g, ls in enumerate(lane_sl):
        c_ref[:, :, ls] = cs_ref[:, g, :, :].astype(_BF16)

    @pl.when(t == n_t - 1)
    def _():
        for g, ls in enumerate(lane_sl):
            nconv_ref[:, :, ls] = uext_ref[:, g, seg_rows + (HIST_PAD - HIST):seg_rows + HIST_PAD, :]

    uext_ref[:, :, 0:HIST_PAD, :] = uext_ref[:, :, seg_rows:seg_rows + HIST_PAD, :]

    def head_norm(z, g_ref):
        ssq = _dot_split2(z * z, gmat_ref[...])
        return z * lax.rsqrt(ssq * (1.0 / HEAD_DIM) + EPS) * g_ref[...]

    q = _dot(h, win_ref[:, 2 * CONV_CH:2 * CONV_CH + ATTN_W])
    ssq_q = _dot((q * q).astype(_BF16), gmat_ref[...])
    q = q * lax.rsqrt(ssq_q * (1.0 / HEAD_DIM) + EPS) * gq_ref[...]
    q_ref[...] = (q * (HEAD_DIM ** -0.5)).astype(_BF16).reshape(q_ref.shape)
    if feature_major_kv:
        k_t = _dot_nt(wkv_ref[0:ATTN_W, :], h)
        gain = jnp.concatenate([gk_ref[...]] * (rows // LANES), axis=1)
        for hh in range(N_HEADS):
            rs = slice(hh * HEAD_DIM, (hh + 1) * HEAD_DIM)
            kh = k_t[rs]
            ssq = jnp.sum(kh * kh, axis=0, keepdims=True)
            k_ref[rs, :] = kh * lax.rsqrt(ssq * (1.0 / HEAD_DIM) + EPS) * gain[rs]
        v_ref[...] = _dot_nt(wkv_ref[ATTN_W:, :], h)
    else:
        k = _dot(h, win_ref[:, 2 * CONV_CH + ATTN_W:2 * CONV_CH + 2 * ATTN_W])
        k_ref[...] = head_norm(k, gk_ref).reshape(k_ref.shape)
        v_ref[...] = _dot(h, win_ref[:, 2 * CONV_CH + 2 * ATTN_W:]).reshape(v_ref.shape)


def _inproj(x, hist, hist_layer, w, *, layer, n_seg, seg_rows, kv_stack=None):
    batch, t_len, d_model = x.shape
    depth = w["win"].shape[0]
    assert batch % n_seg == 0 and t_len % seg_rows == 0 and seg_rows % CONV_ROWS == 0
    n_t = t_len // seg_rows
    assert n_seg == 1 or n_t == 1
    feature_major_kv = kv_stack is not None
    rows_spec = lambda width: pl.BlockSpec((n_seg, seg_rows, width), lambda b, t: (b, t, 0))
    shared = lambda arr: pl.BlockSpec(arr.shape, lambda b, t: (0,) * arr.ndim)
    const = lambda arr: pl.BlockSpec((None,) + arr.shape[1:], lambda b, t: (layer,) + (0,) * (arr.ndim - 1))
    hist_spec = pl.BlockSpec((None, n_seg, HIST_PAD, CONV_CH), lambda b, t: (hist_layer, b, 0, 0))
    nconv_spec = pl.BlockSpec((n_seg, HIST, CONV_CH), lambda b, t: (b, 0, 0))
    small = [w["wdw"], w["bdw"], w["gcn"], w["bcn"], w["wpw"]]
    aliases = {}
    if feature_major_kv:
        assert n_seg == 1 and seg_rows % LANES == 0
        args = [x, hist, w["gmix"], w["win"], w["w_kv_t"]] + small + [w["gq"], w["gk_col"], w["gmat"]]
        in_specs = [rows_spec(d_model), hist_spec] + [const(a) for a in args[2:-1]] + [shared(w["gmat"])]
        kv_shape = jax.ShapeDtypeStruct((depth, batch, ATTN_W, t_len), _F32)
        first_layer = kv_stack[0] is None
        if first_layer:
            kv_spec = pl.BlockSpec((depth, None, ATTN_W, seg_rows), lambda b, t: (0, b, 0, t))
            dummy = jnp.zeros((SUBLANES, LANES), _F32)
            args += [dummy, dummy]
            in_specs += [shared(dummy)] * 2
        else:
            kv_spec = pl.BlockSpec((None, None, ATTN_W, seg_rows), lambda b, t: (layer, b, 0, t))
            aliases = {len(args): 2, len(args) + 1: 3}
            args += list(kv_stack)
            in_specs += [pl.BlockSpec(memory_space=pl.ANY)] * 2
    else:
        args = [x, hist, w["gmix"], w["win"]] + small + [w["gq"], w["gk"], w["gmat"]]
        in_specs = [rows_spec(d_model), hist_spec] + [const(a) for a in args[2:-1]] + [shared(w["gmat"])]
        kv_shape = jax.ShapeDtypeStruct((batch, t_len, ATTN_W), _F32)
        kv_spec = rows_spec(ATTN_W)
        first_layer = False
    out_shape = (
        jax.ShapeDtypeStruct((batch, t_len, CONV_CH), _BF16),
        jax.ShapeDtypeStruct((batch, t_len, ATTN_W), _BF16),
        kv_shape, kv_shape,
        jax.ShapeDtypeStruct((batch, HIST, CONV_CH), _F32),
    )
    return pl.pallas_call(
        functools.partial(_inproj_kernel, n_seg=n_seg, seg_rows=seg_rows, feature_major_kv=feature_major_kv,
                          first_layer=first_layer),
        out_shape=out_shape,
        grid=(batch // n_seg, n_t),
        in_specs=in_specs,
        out_specs=(rows_spec(CONV_CH), rows_spec(ATTN_W), kv_spec, kv_spec, nconv_spec),
        scratch_shapes=[pltpu.VMEM((n_seg, CONV_CH // LANES, seg_rows + HIST_PAD, LANES), _F32),
                        pltpu.VMEM((n_seg, CONV_CH // LANES, seg_rows, LANES), _F32),
                        pltpu.VMEM((n_seg * seg_rows, CONV_CH), _BF16)],
        input_output_aliases=aliases,
        compiler_params=pltpu.CompilerParams(dimension_semantics=("arbitrary", "arbitrary"),
                                             vmem_limit_bytes=VMEM_LIMIT_BYTES),
        name="inproj_conv",
    )(*args)


def _attn_kernel(*refs, n_sub, past_chunks0, chunks_per_step, layer, own_feature_major):
    n_fm = n_sub + 2 if own_feature_major else 2
    q_ref, k_fm, v_fm, rest = refs[0], refs[1:1 + n_fm], refs[1 + n_fm:1 + 2 * n_fm], refs[1 + 2 * n_fm:]
    if not own_feature_major:
        kd_ref, vd_ref, rest = rest[0], rest[1], rest[2:]
    kany_ref, vany_ref, ud_ref, up_ref, o_ref, oacc_ref, run_ref, kbuf_ref, vbuf_ref, sem_ref = rest
    b = pl.program_id(0)
    i = pl.program_id(1)
    tq = q_ref.shape[0] // n_sub
    n_pairs = ATTN_W // LANES
    subs = range(n_sub)
    n_past = [past_chunks0 + i * chunks_per_step + j for j in subs]
    even_head = lax.broadcasted_iota(jnp.int32, (tq, LANES), 1) < HEAD_DIM
    pair_sl = [slice(p * LANES, (p + 1) * LANES) for p in range(n_pairs)]

    def stacked_queries(j):
        qs = []
        for sl in pair_sl:
            qp = q_ref[j * tq:(j + 1) * tq, sl]
            qs.append(jnp.concatenate([jnp.where(even_head, qp, 0.0), jnp.where(even_head, 0.0, qp)], axis=0))
        return qs

    def scores_fm(qs, key_refs):
        out = []
        for p, sl in enumerate(pair_sl):
            keys = jnp.concatenate([r[sl, :] for r in key_refs], axis=1).astype(_BF16)
            out.append(_dot(qs[p], keys))
        return jnp.concatenate(out, axis=0)

    def log_terms(z):
        sp = jnp.log(1.0 + jnp.exp(-jnp.abs(z)))
        neg_part = jnp.minimum(z, 0.0)
        return neg_part - sp, (neg_part - z) - sp

    def suffix_sums(log_1m, u_ref):
        hi = log_1m.astype(_BF16)
        lo = (log_1m - hi.astype(_F32)).astype(_BF16)
        r = _dot(jnp.concatenate([hi, lo], axis=1), u_ref[...])
        return r[:, LANES:], r[:, :LANES]

    def weighted_values(fm_weights, fm_value_refs, rm_weight=None, rm_value_ref=None):
        wb = jnp.concatenate(fm_weights, axis=1).astype(_BF16)
        outs = []
        for p, sl in enumerate(pair_sl):
            rows = slice(2 * tq * p, 2 * tq * (p + 1))
            vals = jnp.concatenate([r[sl, :] for r in fm_value_refs], axis=1).astype(_BF16)
            pv = _dot_nt(wb[rows], vals)
            if rm_weight is not None:
                pv = pv + _dot(rm_weight[rows].astype(_BF16), rm_value_ref[:, sl].astype(_BF16))
            outs.append(jnp.where(even_head, pv[:tq], pv[tq:]))
        return jnp.concatenate(outs, axis=1)

    qs = [stacked_queries(j) for j in subs]
    if own_feature_major:
        z = [scores_fm(qs[j], k_fm[j:j + 3]) for j in subs]
        z_own = [zj[:, 2 * KEY_CHUNK:] for zj in z]
    else:
        z = [scores_fm(qs[0], k_fm)]
        z_own = [jnp.concatenate([_dot_nt(qs[0][p], kd_ref[:, sl].astype(_BF16))
                                  for p, sl in enumerate(pair_sl)], axis=0)]
    q_row = lax.broadcasted_iota(jnp.int32, (N_HEADS * tq, tq), 0) & (tq - 1)
    causal = lax.broadcasted_iota(jnp.int32, (N_HEADS * tq, tq), 1) < q_row
    z_2 = [z[j][:, 0:KEY_CHUNK] + jnp.where(n_past[j] >= 2, 0.0, MASKED_SCORE) for j in subs]
    z_1 = [z[j][:, KEY_CHUNK:2 * KEY_CHUNK] + jnp.where(n_past[j] >= 1, 0.0, MASKED_SCORE) for j in subs]
    z_d = [jnp.where(causal, z_own[j], MASKED_SCORE) for j in subs]
    t_d = [log_terms(zz) for zz in z_d]
    t_1 = [log_terms(zz) for zz in z_1]
    t_2 = [log_terms(zz) for zz in z_2]
    s_d = [suffix_sums(t_d[j][1], ud_ref) for j in subs]
    s_1 = [suffix_sums(t_1[j][1], up_ref) for j in subs]
    s_2 = [suffix_sums(t_2[j][1], up_ref) for j in subs]
    a_d = [jnp.exp(t_d[j][0] + s_d[j][0]) for j in subs]
    a_1 = [jnp.exp(t_1[j][0] + (s_1[j][0] + s_d[j][1])) for j in subs]
    run_1 = [s_d[j][1] + s_1[j][1] for j in subs]
    a_2 = [jnp.exp(t_2[j][0] + (s_2[j][0] + run_1[j])) for j in subs]
    run = [run_1[j] + s_2[j][1] for j in subs]
    for j in subs:
        if own_feature_major:
            out = weighted_values((a_2[j], a_1[j], a_d[j]), v_fm[j:j + 3])
        else:
            out = weighted_values((a_2[j], a_1[j]), v_fm, a_d[j], vd_ref)
        oacc_ref[j * tq:(j + 1) * tq, :] = out
        run_ref[j] = run[j]

    def more(carry):
        c, m = carry
        return jnp.logical_and(c >= 0, m > STICK_LOG_FLOOR)

    def older(j, carry):
        c, _ = carry
        start = pl.multiple_of(c * KEY_CHUNK, KEY_CHUNK)
        k_copy = pltpu.make_async_copy(kany_ref.at[layer, b, :, pl.ds(start, KEY_CHUNK)], kbuf_ref, sem_ref.at[0])
        v_copy = pltpu.make_async_copy(vany_ref.at[layer, b, :, pl.ds(start, KEY_CHUNK)], vbuf_ref, sem_ref.at[1])
        k_copy.start()
        v_copy.start()
        k_copy.wait()
        v_copy.wait()
        lb, l1 = log_terms(scores_fm(stacked_queries(j), (kbuf_ref,)))
        inside, tot = suffix_sums(l1, up_ref)
        run_c = run_ref[j]
        oacc_ref[j * tq:(j + 1) * tq, :] += weighted_values((jnp.exp(lb + (inside + run_c)),), (vbuf_ref,))
        run_c = run_c + tot
        run_ref[j] = run_c
        return c - 1, jnp.max(run_c)

    for j in subs:
        lax.while_loop(more, functools.partial(older, j), (n_past[j] - 3, jnp.max(run[j])))

    o_ref[...] = oacc_ref[...].astype(o_ref.dtype)


def _attention(q, k_own, v_own, k_past, v_past, *, layer, past_is_own):
    batch, t_len, _ = q.shape
    if past_is_own:
        tq = KEY_CHUNK
        n_sub = ATTN_BLOCKS_PER_STEP if t_len % (ATTN_BLOCKS_PER_STEP * tq) == 0 else 1
        past_chunks0, chunks_per_step, n_fm = 0, n_sub, n_sub + 2
    else:
        tq, n_sub = t_len, 1
        assert k_past.shape[-1] % KEY_CHUNK == 0
        past_chunks0, chunks_per_step, n_fm = k_past.shape[-1] // KEY_CHUNK, 0, 2
    assert t_len % (n_sub * tq) == 0 and tq <= KEY_CHUNK and tq & (tq - 1) == 0
    n_steps = t_len // (n_sub * tq)
    assert past_is_own or n_steps == 1

    def newer(n):
        u = lax.broadcasted_iota(jnp.int32, (n, n), 0) > lax.broadcasted_iota(jnp.int32, (n, n), 1)
        once = jnp.concatenate([jnp.ones((n, LANES), _BF16), u.astype(_BF16)], axis=1)
        return jnp.concatenate([once, once], axis=0)
    u_diag, u_past = newer(tq), newer(KEY_CHUNK)

    def fm_chunk(m):
        imap = lambda b, i: (layer, b, 0, jnp.maximum(past_chunks0 + i * chunks_per_step + m - 2, 0))
        return pl.BlockSpec((None, None, ATTN_W, KEY_CHUNK), imap)

    rm_block = pl.BlockSpec((None, n_sub * tq, ATTN_W), lambda b, i: (b, i, 0))
    const = lambda arr: pl.BlockSpec(arr.shape, lambda b, i: (0,) * arr.ndim)
    any_spec = pl.BlockSpec(memory_space=pl.ANY)
    fm_specs = [fm_chunk(m) for m in range(n_fm)]
    args = [q] + [k_past] * n_fm + [v_past] * n_fm
    in_specs = [rm_block] + fm_specs + fm_specs
    if not past_is_own:
        args += [k_own, v_own]
        in_specs += [rm_block, rm_block]
    args += [k_past, v_past, u_diag, u_past]
    in_specs += [any_spec, any_spec, const(u_diag), const(u_past)]
    return pl.pallas_call(
        functools.partial(_attn_kernel, n_sub=n_sub, past_chunks0=past_chunks0, chunks_per_step=chunks_per_step,
                          layer=layer, own_feature_major=past_is_own),
        out_shape=jax.ShapeDtypeStruct((batch, t_len, ATTN_W), _BF16),
        grid=(batch, n_steps),
        in_specs=in_specs,
        out_specs=rm_block,
        scratch_shapes=[pltpu.VMEM((n_sub * tq, ATTN_W), _F32),
                        pltpu.VMEM((n_sub, N_HEADS * tq, LANES), _F32),
                        pltpu.VMEM((ATTN_W, KEY_CHUNK), _F32),
                        pltpu.VMEM((ATTN_W, KEY_CHUNK), _F32),
                        pltpu.SemaphoreType.DMA((2,))],
        compiler_params=pltpu.CompilerParams(dimension_semantics=("arbitrary", "arbitrary"),
                                             vmem_limit_bytes=VMEM_LIMIT_BYTES),
        name="stick_attention",
    )(*args)


def _outffn_kernel(xa_ref, ca_ref, oa_ref, xb_ref, cb_ref, ob_ref, wout_any, gff_ref, w1_any, w2_any,
                   ya_ref, yb_ref, wout_ref, w1_ref, w2_ref, stage_ref, sem_ref, *, n_first, layer):
    d_ff = w1_ref.shape[1]
    r = pl.program_id(0)

    @pl.when(r == 0)
    def _():
        slots, rows, cols = stage_ref.shape
        tiles = [(src, dst, i, j)
                 for src, dst in ((wout_any, wout_ref), (w1_any, w1_ref), (w2_any, w2_ref))
                 for i in range(dst.shape[0] // rows) for j in range(dst.shape[1] // cols)]

        def tile_copy(n):
            src, _, i, j = tiles[n]
            return pltpu.make_async_copy(src.at[layer, pl.ds(i * rows, rows), pl.ds(j * cols, cols)],
                                         stage_ref.at[n % slots], sem_ref.at[n % slots])
        for n in range(min(slots - 1, len(tiles))):
            tile_copy(n).start()
        for n, (_, dst, i, j) in enumerate(tiles):
            if n + slots - 1 < len(tiles):
                tile_copy(n + slots - 1).start()
            tile_copy(n).wait()
            dst[i * rows:(i + 1) * rows, j * cols:(j + 1) * cols] = stage_ref[n % slots].astype(_BF16)

    def mix_and_ffn(x_ref, c_ref, o_ref, y_ref):
        x1 = x_ref[...] + _dot(c_ref[...], wout_ref[0:CONV_CH, :]) + _dot(o_ref[...], wout_ref[CONV_CH:, :])
        h = x1 * lax.rsqrt(jnp.mean(x1 * x1, axis=-1, keepdims=True) + EPS) * gff_ref[...]
        h = h.astype(_BF16)
        ffn = None
        for cc in range(d_ff // FF_CHUNK):
            f = jnp.maximum(_dot(h, w1_ref[:, cc * FF_CHUNK:(cc + 1) * FF_CHUNK]), 0.0)
            part = _dot((f * f).astype(_BF16), w2_ref[cc * FF_CHUNK:(cc + 1) * FF_CHUNK, :])
            ffn = part if ffn is None else ffn + part
        y_ref[...] = x1 + ffn

    @pl.when(r < n_first)
    def _():
        mix_and_ffn(xa_ref, ca_ref, oa_ref, ya_ref)

    @pl.when(r >= n_first)
    def _():
        mix_and_ffn(xb_ref, cb_ref, ob_ref, yb_ref)


def _outffn(xa, ca, oa, xb, cb, ob, wout, gff, w1, w2, *, layer):
    d_model = xa.shape[1]
    d_ff = w1.shape[-1]
    tb = ROW_BLOCK
    assert xa.shape[0] % tb == 0 and xb.shape[0] % tb == 0 and d_ff % FF_CHUNK == 0
    assert d_model % WEIGHT_STAGE_ROWS == 0 and d_ff % WEIGHT_STAGE_ROWS == 0
    na, nb = xa.shape[0] // tb, xb.shape[0] // tb
    first = lambda width: pl.BlockSpec((tb, width), lambda r: (jnp.minimum(r, na - 1), 0))
    second = lambda width: pl.BlockSpec((tb, width), lambda r: (jnp.maximum(r - na, 0), 0))
    any_spec = pl.BlockSpec(memory_space=pl.ANY)
    gff_spec = pl.BlockSpec((None,) + gff.shape[1:], lambda r: (layer, 0, 0))
    return pl.pallas_call(
        functools.partial(_outffn_kernel, n_first=na, layer=layer),
        out_shape=(jax.ShapeDtypeStruct(xa.shape, _F32), jax.ShapeDtypeStruct(xb.shape, _F32)),
        grid=(na + nb,),
        in_specs=[first(d_model), first(CONV_CH), first(ATTN_W), second(d_model), second(CONV_CH), second(ATTN_W),
                  any_spec, gff_spec, any_spec, any_spec],
        out_specs=(first(d_model), second(d_model)),
        scratch_shapes=[pltpu.VMEM(wout.shape[1:], _BF16), pltpu.VMEM(w1.shape[1:], _BF16),
                        pltpu.VMEM(w2.shape[1:], _BF16),
                        pltpu.VMEM((WEIGHT_STAGE_SLOTS, WEIGHT_STAGE_ROWS, d_model), _F32),
                        pltpu.SemaphoreType.DMA((WEIGHT_STAGE_SLOTS,))],
        compiler_params=pltpu.CompilerParams(dimension_semantics=("arbitrary",),
                                             vmem_limit_bytes=VMEM_LIMIT_BYTES),
        name="outproj_ffn",
    )(xa, ca, oa, xb, cb, ob, wout, gff, w1, w2)


def kernel(x_prompt, x_sample, cache_k, cache_v, state_conv, g_mix, w_in, w_dw, b_dw, g_cn, b_cn, w_pw,
           g_q, g_k, w_out, g_ff, w_ff1, w_ff2):
    depth = w_in.shape[0]
    bp, tp, _ = x_prompt.shape
    bs, ts, _ = x_sample.shape
    n_past = cache_k.shape[2]
    head_id = lax.broadcasted_iota(jnp.int32, (ATTN_W, ATTN_W), 0) // HEAD_DIM
    gmat = (head_id == head_id.T).astype(_BF16)
    zero_hist = jnp.zeros((1, bp, HIST_PAD, CONV_CH), _F32)
    sample_hist = jnp.pad(state_conv, ((0, 0), (0, 0), (HIST_PAD - HIST, 0), (0, 0)))
    row = lambda a: a.reshape(depth, 1, -1)
    per_head = lambda g: jnp.tile(g, (1, N_HEADS))
    win = w_in.astype(_BF16)
    w = dict(gmix=row(g_mix), win=win, w_kv_t=jnp.transpose(win[:, :, 2 * CONV_CH + ATTN_W:], (0, 2, 1)),
             wdw=w_dw, bdw=row(b_dw), gcn=row(g_cn), bcn=row(b_cn), wpw=w_pw.astype(_BF16),
             gq=row(per_head(g_q)), gk=row(per_head(g_k)),
             gk_col=jnp.broadcast_to(per_head(g_k)[:, :, None], (depth, ATTN_W, LANES)), gmat=gmat,
             wout=w_out, gff=row(g_ff), w1=w_ff1, w2=w_ff2)
    to_fm = lambda a: jnp.transpose(a, (0, 1, 3, 4, 2)).reshape(a.shape[0], a.shape[1], ATTN_W, a.shape[2])
    from_fm = lambda a: jnp.transpose(a.reshape(a.shape[0], a.shape[1], N_HEADS, HEAD_DIM, a.shape[3]),
                                      (0, 1, 4, 2, 3))
    cache_k_fm, cache_v_fm = to_fm(cache_k), to_fm(cache_v)

    hp, hs = x_prompt, x_sample
    kv_stack = (None, None)
    conv_p, k_s, v_s, conv_s = [], [], [], []
    flat = lambda a: a.reshape(-1, a.shape[-1])
    for l in range(depth):
        cp, q, k_fm, v_fm, cv = _inproj(hp, zero_hist, 0, w, layer=l, n_seg=1, seg_rows=min(ROW_BLOCK, tp),
                                        kv_stack=kv_stack)
        kv_stack = (k_fm, v_fm)
        op = _attention(q, None, None, k_fm, v_fm, layer=l, past_is_own=True)
        conv_p.append(cv)
        cs, q, k, v, cv = _inproj(hs, sample_hist, l, w, layer=l, n_seg=bs, seg_rows=ts)
        os_ = _attention(q, k, v, cache_k_fm, cache_v_fm, layer=l, past_is_own=False)
        k_s.append(k.reshape(bs, ts, N_HEADS, HEAD_DIM))
        v_s.append(v.reshape(bs, ts, N_HEADS, HEAD_DIM))
        conv_s.append(cv)
        yp, ys = _outffn(flat(hp), flat(cp), flat(op), flat(hs), flat(cs), flat(os_),
                         w["wout"], w["gff"], w["w1"], w["w2"], layer=l)
        hp, hs = yp.reshape(hp.shape), ys.reshape(hs.shape)
    return (hp, hs, from_fm(kv_stack[0]), from_fm(kv_stack[1]), jnp.stack(conv_p),
            jnp.stack(k_s), jnp.stack(v_s), jnp.stack(conv_s))
```

```python
import functools

import jax
import jax.numpy as jnp
from jax import lax
from jax.experimental import pallas as pl
from jax.experimental.pallas import tpu as pltpu

EPS = 1e-6
CONV_WIDTH = 31
HIST = CONV_WIDTH - 1
HIST_PAD = 32
SUBLANES = 8
N_HEADS = 8
HEAD_DIM = 64
ATTN_W = N_HEADS * HEAD_DIM
CONV_CH = 512
LANES = 128
KEY_CHUNK = 128
CONV_ROWS = 64
FF_CHUNK = 1024
ROW_BLOCK = 512
WEIGHT_STAGE_ROWS = 256
WEIGHT_STAGE_SLOTS = 8
VMEM_LIMIT_BYTES = 56 * 1024 * 1024
ATTN_BLOCKS_PER_STEP = 4
STICK_LOG_FLOOR = -87.5
MASKED_SCORE = -1e30

_F32 = jnp.float32
_BF16 = jnp.bfloat16


def _dot(a, b):
    return jnp.dot(a, b, preferred_element_type=_F32)


def _dot_nt(a, b):
    return lax.dot_general(a, b, (((1,), (1,)), ((), ())), preferred_element_type=_F32)


def _dot_split2(x, m):
    hi = x.astype(_BF16)
    lo = (x - hi.astype(_F32)).astype(_BF16)
    return _dot(hi, m) + _dot(lo, m)


def _inproj_kernel(*refs, n_seg, seg_rows, feature_major_kv, first_layer):
    if feature_major_kv:
        (x_ref, hist_ref, gmix_ref, win_ref, wkv_ref, wdw_ref, bdw_ref, gcn_ref, bcn_ref, wpw_ref,
         gq_ref, gk_ref, gmat_ref, _, _,
         c_ref, q_ref, k_ref, v_ref, nconv_ref, uext_ref, cs_ref, act_ref) = refs
        if first_layer:
            for stack_ref in (k_ref, v_ref):
                if stack_ref.shape[0] > 1:
                    stack_ref[1:] = jnp.zeros((stack_ref.shape[0] - 1,) + stack_ref.shape[1:], _F32)
            k_ref, v_ref = k_ref.at[0], v_ref.at[0]
    else:
        (x_ref, hist_ref, gmix_ref, win_ref, wdw_ref, bdw_ref, gcn_ref, bcn_ref, wpw_ref,
         gq_ref, gk_ref, gmat_ref,
         c_ref, q_ref, k_ref, v_ref, nconv_ref, uext_ref, cs_ref, act_ref) = refs
    t = pl.program_id(1)
    n_t = pl.num_programs(1)
    rows = n_seg * seg_rows
    d_model = x_ref.shape[-1]

    x = x_ref[...].reshape(rows, d_model)
    h = x * lax.rsqrt(jnp.mean(x * x, axis=-1, keepdims=True) + EPS) * gmix_ref[...]
    h = h.astype(_BF16)

    a = _dot(h, win_ref[:, 0:CONV_CH])
    gate = _dot(h, win_ref[:, CONV_CH:2 * CONV_CH])
    u = a * jax.nn.sigmoid(gate)

    lane_sl = [slice(g * LANES, (g + 1) * LANES) for g in range(CONV_CH // LANES)]
    u3 = u.reshape(n_seg, seg_rows, CONV_CH)

    @pl.when(t == 0)
    def _():
        for g, ls in enumerate(lane_sl):
            uext_ref[:, g, 0:HIST_PAD, :] = hist_ref[:, :, ls]

    for g, ls in enumerate(lane_sl):
        uext_ref[:, g, HIST_PAD:HIST_PAD + seg_rows, :] = u3[:, :, ls]

    half = CONV_ROWS // 2
    for s in range(n_seg):
        for r in range(seg_rows // CONV_ROWS):
            r0 = r * CONV_ROWS
            acc_even = jnp.zeros((half, CONV_CH), _F32)
            acc_odd = jnp.zeros((half, CONV_CH), _F32)
            for m in range(CONV_WIDTH + 1):
                start = r0 + m + (HIST_PAD - HIST)
                window = jnp.concatenate(
                    [uext_ref[s, g, pl.ds(start, half, stride=2), :] for g in range(len(lane_sl))], axis=1)
                if m < CONV_WIDTH:
                    acc_even = acc_even + wdw_ref[m:m + 1, :] * window
                if m >= 1:
                    acc_odd = acc_odd + wdw_ref[m - 1:m, :] * window
            for par, acc in enumerate((acc_even, acc_odd)):
                cv = acc + bdw_ref[...]
                mu = jnp.mean(cv, axis=-1, keepdims=True)
                xc = cv - mu
                y = xc * lax.rsqrt(jnp.mean(xc * xc, axis=-1, keepdims=True) + EPS)
                y = y * gcn_ref[...] + bcn_ref[...]
                y = y * jax.nn.sigmoid(y)
                base = s * seg_rows + r0 + par * half
                act_ref[base:base + half, :] = y.astype(_BF16)

    c_perm = _dot(act_ref[...], wpw_ref[...])
    for s in range(n_seg):
        for r in range(seg_rows // CONV_ROWS):
            r0 = r * CONV_ROWS
            base = s * seg_rows + r0
            for par in range(2):
                rows_par = c_perm[base + par * half:base + (par + 1) * half]
                for g, ls in enumerate(lane_sl):
                    cs_ref[s, g, pl.ds(r0 + par, half, stride=2), :] = rows_par[:, ls]
    for g, ls in enumerate(lane_sl):
        c_ref[:, :, ls] = cs_ref[:, g, :, :].astype(_BF16)

    @pl.when(t == n_t - 1)
    def _():
        for g, ls in enumerate(lane_sl):
            nconv_ref[:, :, ls] = uext_ref[:, g, seg_rows + (HIST_PAD - HIST):seg_rows + HIST_PAD, :]

    uext_ref[:, :, 0:HIST_PAD, :] = uext_ref[:, :, seg_rows:seg_rows + HIST_PAD, :]

    def head_norm(z, g_ref):
        ssq = _dot_split2(z * z, gmat_ref[...])
        return z * lax.rsqrt(ssq * (1.0 / HEAD_DIM) + EPS) * g_ref[...]

    q = _dot(h, win_ref[:, 2 * CONV_CH:2 * CONV_CH + ATTN_W])
    ssq_q = _dot((q * q).astype(_BF16), gmat_ref[...])
    q = q * lax.rsqrt(ssq_q * (1.0 / HEAD_DIM) + EPS) * gq_ref[...]
    q_ref[...] = (q * (HEAD_DIM ** -0.5)).astype(_BF16).reshape(q_ref.shape)
    if feature_major_kv:
        k_t = _dot_nt(wkv_ref[0:ATTN_W, :], h)
        gain = jnp.concatenate([gk_ref[...]] * (rows // LANES), axis=1)
        for hh in range(N_HEADS):
            rs = slice(hh * HEAD_DIM, (hh + 1) * HEAD_DIM)
            kh = k_t[rs]
            ssq = jnp.sum(kh * kh, axis=0, keepdims=True)
            k_ref[rs, :] = kh * lax.rsqrt(ssq * (1.0 / HEAD_DIM) + EPS) * gain[rs]
        v_ref[...] = _dot_nt(wkv_ref[ATTN_W:, :], h)
    else:
        k = _dot(h, win_ref[:, 2 * CONV_CH + ATTN_W:2 * CONV_CH + 2 * ATTN_W])
        k_ref[...] = head_norm(k, gk_ref).reshape(k_ref.shape)
        v_ref[...] = _dot(h, win_ref[:, 2 * CONV_CH + 2 * ATTN_W:]).reshape(v_ref.shape)


def _inproj(x, hist, hist_layer, w, *, layer, n_seg, seg_rows, kv_stack=None):
    batch, t_len, d_model = x.shape
    depth = w["win"].shape[0]
    assert batch % n_seg == 0 and t_len % seg_rows == 0 and seg_rows % CONV_ROWS == 0
    n_t = t_len // seg_rows
    assert n_seg == 1 or n_t == 1
    feature_major_kv = kv_stack is not None
    rows_spec = lambda width: pl.BlockSpec((n_seg, seg_rows, width), lambda b, t: (b, t, 0))
    shared = lambda arr: pl.BlockSpec(arr.shape, lambda b, t: (0,) * arr.ndim)
    const = lambda arr: pl.BlockSpec((None,) + arr.shape[1:], lambda b, t: (layer,) + (0,) * (arr.ndim - 1))
    hist_spec = pl.BlockSpec((None, n_seg, HIST_PAD, CONV_CH), lambda b, t: (hist_layer, b, 0, 0))
    nconv_spec = pl.BlockSpec((n_seg, HIST, CONV_CH), lambda b, t: (b, 0, 0))
    small = [w["wdw"], w["bdw"], w["gcn"], w["bcn"], w["wpw"]]
    aliases = {}
    if feature_major_kv:
        assert n_seg == 1 and seg_rows % LANES == 0
        args = [x, hist, w["gmix"], w["win"], w["w_kv_t"]] + small + [w["gq"], w["gk_col"], w["gmat"]]
        in_specs = [rows_spec(d_model), hist_spec] + [const(a) for a in args[2:-1]] + [shared(w["gmat"])]
        kv_shape = jax.ShapeDtypeStruct((depth, batch, ATTN_W, t_len), _F32)
        first_layer = kv_stack[0] is None
        if first_layer:
            kv_spec = pl.BlockSpec((depth, None, ATTN_W, seg_rows), lambda b, t: (0, b, 0, t))
            dummy = jnp.zeros((SUBLANES, LANES), _F32)
            args += [dummy, dummy]
            in_specs += [shared(dummy)] * 2
        else:
            kv_spec = pl.BlockSpec((None, None, ATTN_W, seg_rows), lambda b, t: (layer, b, 0, t))
            aliases = {len(args): 2, len(args) + 1: 3}
            args += list(kv_stack)
            in_specs += [pl.BlockSpec(memory_space=pl.ANY)] * 2
    else:
        args = [x, hist, w["gmix"], w["win"]] + small + [w["gq"], w["gk"], w["gmat"]]
        in_specs = [rows_spec(d_model), hist_spec] + [const(a) for a in args[2:-1]] + [shared(w["gmat"])]
        kv_shape = jax.ShapeDtypeStruct((batch, t_len, ATTN_W), _F32)
        kv_spec = rows_spec(ATTN_W)
        first_layer = False
    out_shape = (
        jax.ShapeDtypeStruct((batch, t_len, CONV_CH), _BF16),
        jax.ShapeDtypeStruct((batch, t_len, ATTN_W), _BF16),
        kv_shape, kv_shape,
        jax.ShapeDtypeStruct((batch, HIST, CONV_CH), _F32),
    )
    return pl.pallas_call(
        functools.partial(_inproj_kernel, n_seg=n_seg, seg_rows=seg_rows, feature_major_kv=feature_major_kv,
                          first_layer=first_layer),
        out_shape=out_shape,
        grid=(batch // n_seg, n_t),
        in_specs=in_specs,
        out_specs=(rows_spec(CONV_CH), rows_spec(ATTN_W), kv_spec, kv_spec, nconv_spec),
        scratch_shapes=[pltpu.VMEM((n_seg, CONV_CH // LANES, seg_rows + HIST_PAD, LANES), _F32),
                        pltpu.VMEM((n_seg, CONV_CH // LANES, seg_rows, LANES), _F32),
                        pltpu.VMEM((n_seg * seg_rows, CONV_CH), _BF16)],
        input_output_aliases=aliases,
        compiler_params=pltpu.CompilerParams(dimension_semantics=("arbitrary", "arbitrary"),
                                             vmem_limit_bytes=VMEM_LIMIT_BYTES),
        name="inproj_conv",
    )(*args)


def _attn_kernel(*refs, n_sub, past_chunks0, chunks_per_step, layer, own_feature_major):
    n_fm = n_sub + 2 if own_feature_major else 2
    q_ref, k_fm, v_fm, rest = refs[0], refs[1:1 + n_fm], refs[1 + n_fm:1 + 2 * n_fm], refs[1 + 2 * n_fm:]
    if not own_feature_major:
        kd_ref, vd_ref, rest = rest[0], rest[1], rest[2:]
    kany_ref, vany_ref, ud_ref, up_ref, o_ref, oacc_ref, run_ref, kbuf_ref, vbuf_ref, sem_ref = rest
    b = pl.program_id(0)
    i = pl.program_id(1)
    tq = q_ref.shape[0] // n_sub
    n_pairs = ATTN_W // LANES
    subs = range(n_sub)
    n_past = [past_chunks0 + i * chunks_per_step + j for j in subs]
    even_head = lax.broadcasted_iota(jnp.int32, (tq, LANES), 1) < HEAD_DIM
    pair_sl = [slice(p * LANES, (p + 1) * LANES) for p in range(n_pairs)]

    def stacked_queries(j):
        qs = []
        for sl in pair_sl:
            qp = q_ref[j * tq:(j + 1) * tq, sl]
            qs.append(jnp.concatenate([jnp.where(even_head, qp, 0.0), jnp.where(even_head, 0.0, qp)], axis=0))
        return qs

    def scores_fm(qs, key_refs):
        out = []
        for p, sl in enumerate(pair_sl):
            keys = jnp.concatenate([r[sl, :] for r in key_refs], axis=1).astype(_BF16)
            out.append(_dot(qs[p], keys))
        return jnp.concatenate(out, axis=0)

    def log_terms(z):
        sp = jnp.log(1.0 + jnp.exp(-jnp.abs(z)))
        neg_part = jnp.minimum(z, 0.0)
        return neg_part - sp, (neg_part - z) - sp

    def suffix_sums(log_1m, u_ref):
        hi = log_1m.astype(_BF16)
        lo = (log_1m - hi.astype(_F32)).astype(_BF16)
        r = _dot(jnp.concatenate([hi, lo], axis=1), u_ref[...])
        return r[:, LANES:], r[:, :LANES]

    def weighted_values(fm_weights, fm_value_refs, rm_weight=None, rm_value_ref=None):
        wb = jnp.concatenate(fm_weights, axis=1).astype(_BF16)
        outs = []
        for p, sl in enumerate(pair_sl):
            rows = slice(2 * tq * p, 2 * tq * (p + 1))
            vals = jnp.concatenate([r[sl, :] for r in fm_value_refs], axis=1).astype(_BF16)
            pv = _dot_nt(wb[rows], vals)
            if rm_weight is not None:
                pv = pv + _dot(rm_weight[rows].astype(_BF16), rm_value_ref[:, sl].astype(_BF16))
            outs.append(jnp.where(even_head, pv[:tq], pv[tq:]))
        return jnp.concatenate(outs, axis=1)

    qs = [stacked_queries(j) for j in subs]
    if own_feature_major:
        z = [scores_fm(qs[j], k_fm[j:j + 3]) for j in subs]
        z_own = [zj[:, 2 * KEY_CHUNK:] for zj in z]
    else:
        z = [scores_fm(qs[0], k_fm)]
        z_own = [jnp.concatenate([_dot_nt(qs[0][p], kd_ref[:, sl].astype(_BF16))
                                  for p, sl in enumerate(pair_sl)], axis=0)]
    q_row = lax.broadcasted_iota(jnp.int32, (N_HEADS * tq, tq), 0) & (tq - 1)
    causal = lax.broadcasted_iota(jnp.int32, (N_HEADS * tq, tq), 1) < q_row
    def past_scores(j, back):
        zc = z[j][:, (2 - back) * KEY_CHUNK:(3 - back) * KEY_CHUNK]
        if past_chunks0 + j >= back:
            return zc
        return zc + jnp.where(n_past[j] >= back, 0.0, MASKED_SCORE)

    z_2 = [past_scores(j, 2) for j in subs]
    z_1 = [past_scores(j, 1) for j in subs]
    z_d = [jnp.where(causal, z_own[j], MASKED_SCORE) for j in subs]
    t_d = [log_terms(zz) for zz in z_d]
    t_1 = [log_terms(zz) for zz in z_1]
    t_2 = [log_terms(zz) for zz in z_2]
    s_d = [suffix_sums(t_d[j][1], ud_ref) for j in subs]
    s_1 = [suffix_sums(t_1[j][1], up_ref) for j in subs]
    s_2 = [suffix_sums(t_2[j][1], up_ref) for j in subs]
    a_d = [jnp.exp(t_d[j][0] + s_d[j][0]) for j in subs]
    a_1 = [jnp.exp(t_1[j][0] + (s_1[j][0] + s_d[j][1])) for j in subs]
    run_1 = [s_d[j][1] + s_1[j][1] for j in subs]
    a_2 = [jnp.exp(t_2[j][0] + (s_2[j][0] + run_1[j])) for j in subs]
    run = [run_1[j] + s_2[j][1] for j in subs]
    for j in subs:
        if own_feature_major:
            out = weighted_values((a_2[j], a_1[j], a_d[j]), v_fm[j:j + 3])
        else:
            out = weighted_values((a_2[j], a_1[j]), v_fm, a_d[j], vd_ref)
        oacc_ref[j * tq:(j + 1) * tq, :] = out
        run_ref[j] = run[j]

    def more(carry):
        c, m = carry
        return jnp.logical_and(c >= 0, m > STICK_LOG_FLOOR)

    def older(j, carry):
        c, _ = carry
        start = pl.multiple_of(c * KEY_CHUNK, KEY_CHUNK)
        k_copy = pltpu.make_async_copy(kany_ref.at[layer, b, :, pl.ds(start, KEY_CHUNK)], kbuf_ref, sem_ref.at[0])
        v_copy = pltpu.make_async_copy(vany_ref.at[layer, b, :, pl.ds(start, KEY_CHUNK)], vbuf_ref, sem_ref.at[1])
        k_copy.start()
        v_copy.start()
        k_copy.wait()
        v_copy.wait()
        lb, l1 = log_terms(scores_fm(stacked_queries(j), (kbuf_ref,)))
        inside, tot = suffix_sums(l1, up_ref)
        run_c = run_ref[j]
        oacc_ref[j * tq:(j + 1) * tq, :] += weighted_values((jnp.exp(lb + (inside + run_c)),), (vbuf_ref,))
        run_c = run_c + tot
        run_ref[j] = run_c
        return c - 1, jnp.max(run_c)

    for j in subs:
        lax.while_loop(more, functools.partial(older, j), (n_past[j] - 3, jnp.max(run[j])))

    o_ref[...] = oacc_ref[...].astype(o_ref.dtype)


def _attention(q, k_own, v_own, k_past, v_past, *, layer, past_is_own):
    batch, t_len, _ = q.shape
    if past_is_own:
        tq = KEY_CHUNK
        n_sub = ATTN_BLOCKS_PER_STEP if t_len % (ATTN_BLOCKS_PER_STEP * tq) == 0 else 1
        past_chunks0, chunks_per_step, n_fm = 0, n_sub, n_sub + 2
    else:
        tq, n_sub = t_len, 1
        assert k_past.shape[-1] % KEY_CHUNK == 0
        past_chunks0, chunks_per_step, n_fm = k_past.shape[-1] // KEY_CHUNK, 0, 2
    assert t_len % (n_sub * tq) == 0 and tq <= KEY_CHUNK and tq & (tq - 1) == 0
    n_steps = t_len // (n_sub * tq)
    assert past_is_own or n_steps == 1

    def newer(n):
        u = lax.broadcasted_iota(jnp.int32, (n, n), 0) > lax.broadcasted_iota(jnp.int32, (n, n), 1)
        once = jnp.concatenate([jnp.ones((n, LANES), _BF16), u.astype(_BF16)], axis=1)
        return jnp.concatenate([once, once], axis=0)
    u_diag, u_past = newer(tq), newer(KEY_CHUNK)

    def fm_chunk(m):
        imap = lambda b, i: (layer, b, 0, jnp.maximum(past_chunks0 + i * chunks_per_step + m - 2, 0))
        return pl.BlockSpec((None, None, ATTN_W, KEY_CHUNK), imap)

    rm_block = pl.BlockSpec((None, n_sub * tq, ATTN_W), lambda b, i: (b, i, 0))
    const = lambda arr: pl.BlockSpec(arr.shape, lambda b, i: (0,) * arr.ndim)
    any_spec = pl.BlockSpec(memory_space=pl.ANY)
    fm_specs = [fm_chunk(m) for m in range(n_fm)]
    args = [q] + [k_past] * n_fm + [v_past] * n_fm
    in_specs = [rm_block] + fm_specs + fm_specs
    if not past_is_own:
        args += [k_own, v_own]
        in_specs += [rm_block, rm_block]
    args += [k_past, v_past, u_diag, u_past]
    in_specs += [any_spec, any_spec, const(u_diag), const(u_past)]
    return pl.pallas_call(
        functools.partial(_attn_kernel, n_sub=n_sub, past_chunks0=past_chunks0, chunks_per_step=chunks_per_step,
                          layer=layer, own_feature_major=past_is_own),
        out_shape=jax.ShapeDtypeStruct((batch, t_len, ATTN_W), _BF16),
        grid=(batch, n_steps),
        in_specs=in_specs,
        out_specs=rm_block,
        scratch_shapes=[pltpu.VMEM((n_sub * tq, ATTN_W), _F32),
                        pltpu.VMEM((n_sub, N_HEADS * tq, LANES), _F32),
                        pltpu.VMEM((ATTN_W, KEY_CHUNK), _F32),
                        pltpu.VMEM((ATTN_W, KEY_CHUNK), _F32),
                        pltpu.SemaphoreType.DMA((2,))],
        compiler_params=pltpu.CompilerParams(dimension_semantics=("arbitrary", "arbitrary"),
                                             vmem_limit_bytes=VMEM_LIMIT_BYTES),
        name="stick_attention",
    )(*args)


def _outffn_kernel(xa_ref, ca_ref, oa_ref, xb_ref, cb_ref, ob_ref, wout_any, gff_ref, w1_any, w2_any,
                   ya_ref, yb_ref, wout_ref, w1_ref, w2_ref, stage_ref, sem_ref, *, n_first, layer):
    d_ff = w1_ref.shape[1]
    r = pl.program_id(0)

    @pl.when(r == 0)
    def _():
        slots, rows, cols = stage_ref.shape
        tiles = [(src, dst, i, j)
                 for src, dst in ((wout_any, wout_ref), (w1_any, w1_ref), (w2_any, w2_ref))
                 for i in range(dst.shape[0] // rows) for j in range(dst.shape[1] // cols)]

        def tile_copy(n):
            src, _, i, j = tiles[n]
            return pltpu.make_async_copy(src.at[layer, pl.ds(i * rows, rows), pl.ds(j * cols, cols)],
                                         stage_ref.at[n % slots], sem_ref.at[n % slots])
        for n in range(min(slots - 1, len(tiles))):
            tile_copy(n).start()
        for n, (_, dst, i, j) in enumerate(tiles):
            if n + slots - 1 < len(tiles):
                tile_copy(n + slots - 1).start()
            tile_copy(n).wait()
            dst[i * rows:(i + 1) * rows, j * cols:(j + 1) * cols] = stage_ref[n % slots].astype(_BF16)

    def mix_and_ffn(x_ref, c_ref, o_ref, y_ref):
        x1 = x_ref[...] + _dot(c_ref[...], wout_ref[0:CONV_CH, :]) + _dot(o_ref[...], wout_ref[CONV_CH:, :])
        h = x1 * lax.rsqrt(jnp.mean(x1 * x1, axis=-1, keepdims=True) + EPS) * gff_ref[...]
        h = h.astype(_BF16)
        ffn = None
        for cc in range(d_ff // FF_CHUNK):
            f = jnp.maximum(_dot(h, w1_ref[:, cc * FF_CHUNK:(cc + 1) * FF_CHUNK]), 0.0)
            part = _dot((f * f).astype(_BF16), w2_ref[cc * FF_CHUNK:(cc + 1) * FF_CHUNK, :])
            ffn = part if ffn is None else ffn + part
        y_ref[...] = x1 + ffn

    @pl.when(r < n_first)
    def _():
        mix_and_ffn(xa_ref, ca_ref, oa_ref, ya_ref)

    @pl.when(r >= n_first)
    def _():
        mix_and_ffn(xb_ref, cb_ref, ob_ref, yb_ref)


def _outffn(xa, ca, oa, xb, cb, ob, wout, gff, w1, w2, *, layer):
    d_model = xa.shape[1]
    d_ff = w1.shape[-1]
    tb = ROW_BLOCK
    assert xa.shape[0] % tb == 0 and xb.shape[0] % tb == 0 and d_ff % FF_CHUNK == 0
    assert d_model % WEIGHT_STAGE_ROWS == 0 and d_ff % WEIGHT_STAGE_ROWS == 0
    na, nb = xa.shape[0] // tb, xb.shape[0] // tb
    first = lambda width: pl.BlockSpec((tb, width), lambda r: (jnp.minimum(r, na - 1), 0))
    second = lambda width: pl.BlockSpec((tb, width), lambda r: (jnp.maximum(r - na, 0), 0))
    any_spec = pl.BlockSpec(memory_space=pl.ANY)
    gff_spec = pl.BlockSpec((None,) + gff.shape[1:], lambda r: (layer, 0, 0))
    return pl.pallas_call(
        functools.partial(_outffn_kernel, n_first=na, layer=layer),
        out_shape=(jax.ShapeDtypeStruct(xa.shape, _F32), jax.ShapeDtypeStruct(xb.shape, _F32)),
        grid=(na + nb,),
        in_specs=[first(d_model), first(CONV_CH), first(ATTN_W), second(d_model), second(CONV_CH), second(ATTN_W),
                  any_spec, gff_spec, any_spec, any_spec],
        out_specs=(first(d_model), second(d_model)),
        scratch_shapes=[pltpu.VMEM(wout.shape[1:], _BF16), pltpu.VMEM(w1.shape[1:], _BF16),
                        pltpu.VMEM(w2.shape[1:], _BF16),
                        pltpu.VMEM((WEIGHT_STAGE_SLOTS, WEIGHT_STAGE_ROWS, d_model), _F32),
                        pltpu.SemaphoreType.DMA((WEIGHT_STAGE_SLOTS,))],
        compiler_params=pltpu.CompilerParams(dimension_semantics=("arbitrary",),
                                             vmem_limit_bytes=VMEM_LIMIT_BYTES),
        name="outproj_ffn",
    )(xa, ca, oa, xb, cb, ob, wout, gff, w1, w2)


def kernel(x_prompt, x_sample, cache_k, cache_v, state_conv, g_mix, w_in, w_dw, b_dw, g_cn, b_cn, w_pw,
           g_q, g_k, w_out, g_ff, w_ff1, w_ff2):
    depth = w_in.shape[0]
    bp, tp, _ = x_prompt.shape
    bs, ts, _ = x_sample.shape
    n_past = cache_k.shape[2]
    head_id = lax.broadcasted_iota(jnp.int32, (ATTN_W, ATTN_W), 0) // HEAD_DIM
    gmat = (head_id == head_id.T).astype(_BF16)
    zero_hist = jnp.zeros((1, bp, HIST_PAD, CONV_CH), _F32)
    sample_hist = jnp.pad(state_conv, ((0, 0), (0, 0), (HIST_PAD - HIST, 0), (0, 0)))
    row = lambda a: a.reshape(depth, 1, -1)
    per_head = lambda g: jnp.tile(g, (1, N_HEADS))
    win = w_in.astype(_BF16)
    w = dict(gmix=row(g_mix), win=win, w_kv_t=jnp.transpose(win[:, :, 2 * CONV_CH + ATTN_W:], (0, 2, 1)),
             wdw=w_dw, bdw=row(b_dw), gcn=row(g_cn), bcn=row(b_cn), wpw=w_pw.astype(_BF16),
             gq=row(per_head(g_q)), gk=row(per_head(g_k)),
             gk_col=jnp.broadcast_to(per_head(g_k)[:, :, None], (depth, ATTN_W, LANES)), gmat=gmat,
             wout=w_out, gff=row(g_ff), w1=w_ff1, w2=w_ff2)
    to_fm = lambda a: jnp.transpose(a, (0, 1, 3, 4, 2)).reshape(a.shape[0], a.shape[1], ATTN_W, a.shape[2])
    from_fm = lambda a: jnp.transpose(a.reshape(a.shape[0], a.shape[1], N_HEADS, HEAD_DIM, a.shape[3]),
                                      (0, 1, 4, 2, 3))
    cache_k_fm, cache_v_fm = to_fm(cache_k), to_fm(cache_v)

    hp, hs = x_prompt, x_sample
    kv_stack = (None, None)
    conv_p, k_s, v_s, conv_s = [], [], [], []
    flat = lambda a: a.reshape(-1, a.shape[-1])
    for l in range(depth):
        cp, q, k_fm, v_fm, cv = _inproj(hp, zero_hist, 0, w, layer=l, n_seg=1, seg_rows=min(ROW_BLOCK, tp),
                                        kv_stack=kv_stack)
        kv_stack = (k_fm, v_fm)
        op = _attention(q, None, None, k_fm, v_fm, layer=l, past_is_own=True)
        conv_p.append(cv)
        cs, q, k, v, cv = _inproj(hs, sample_hist, l, w, layer=l, n_seg=bs, seg_rows=ts)
        os_ = _attention(q, k, v, cache_k_fm, cache_v_fm, layer=l, past_is_own=False)
        k_s.append(k.reshape(bs, ts, N_HEADS, HEAD_DIM))
        v_s.append(v.reshape(bs, ts, N_HEADS, HEAD_DIM))
        conv_s.append(cv)
        yp, ys = _outffn(flat(hp), flat(cp), flat(op), flat(hs), flat(cs), flat(os_),
                         w["wout"], w["gff"], w["w1"], w["w2"], layer=l)
        hp, hs = yp.reshape(hp.shape), ys.reshape(hs.shape)
    return (hp, hs, from_fm(kv_stack[0]), from_fm(kv_stack[1]), jnp.stack(conv_p),
            jnp.stack(k_s), jnp.stack(v_s), jnp.stack(conv_s))
```

```python
import functools

import jax
import jax.numpy as jnp
from jax import lax
from jax.experimental import pallas as pl
from jax.experimental.pallas import tpu as pltpu

EPS = 1e-6
CONV_WIDTH = 31
HIST = CONV_WIDTH - 1
HIST_PAD = 32
SUBLANES = 8
N_HEADS = 8
HEAD_DIM = 64
ATTN_W = N_HEADS * HEAD_DIM
CONV_CH = 512
LANES = 128
KEY_CHUNK = 128
CONV_ROWS = 64
FF_CHUNK = 1024
ROW_BLOCK = 512
WEIGHT_STAGE_ROWS = 256
WEIGHT_STAGE_SLOTS = 8
VMEM_LIMIT_BYTES = 56 * 1024 * 1024
ATTN_BLOCKS_PER_STEP = 4
STICK_LOG_FLOOR = -87.5
MASKED_SCORE = -1e30

_F32 = jnp.float32
_BF16 = jnp.bfloat16


def _dot(a, b):
    return jnp.dot(a, b, preferred_element_type=_F32)


def _dot_nt(a, b):
    return lax.dot_general(a, b, (((1,), (1,)), ((), ())), preferred_element_type=_F32)


def _dot_split2(x, m):
    hi = x.astype(_BF16)
    lo = (x - hi.astype(_F32)).astype(_BF16)
    return _dot(hi, m) + _dot(lo, m)


def _inproj_kernel(*refs, n_seg, seg_rows, feature_major_kv, first_layer):
    if feature_major_kv:
        (x_ref, hist_ref, gmix_ref, win_ref, wkv_ref, wdw_ref, bdw_ref, gcn_ref, bcn_ref, wpw_ref,
         gq_ref, gk_ref, gmat_ref, _, _,
         c_ref, q_ref, k_ref, v_ref, nconv_ref, uext_ref, cs_ref, act_ref) = refs
        if first_layer:
            for stack_ref in (k_ref, v_ref):
                if stack_ref.shape[0] > 1:
                    stack_ref[1:] = jnp.zeros((stack_ref.shape[0] - 1,) + stack_ref.shape[1:], _F32)
            k_ref, v_ref = k_ref.at[0], v_ref.at[0]
    else:
        (x_ref, hist_ref, gmix_ref, win_ref, wdw_ref, bdw_ref, gcn_ref, bcn_ref, wpw_ref,
         gq_ref, gk_ref, gmat_ref,
         c_ref, q_ref, k_ref, v_ref, nconv_ref, uext_ref, cs_ref, act_ref) = refs
    t = pl.program_id(1)
    n_t = pl.num_programs(1)
    rows = n_seg * seg_rows
    d_model = x_ref.shape[-1]

    x = x_ref[...].reshape(rows, d_model)
    h = x * lax.rsqrt(jnp.mean(x * x, axis=-1, keepdims=True) + EPS) * gmix_ref[...]
    h = h.astype(_BF16)

    a = _dot(h, win_ref[:, 0:CONV_CH])
    gate = _dot(h, win_ref[:, CONV_CH:2 * CONV_CH])
    u = a * jax.nn.sigmoid(gate)

    lane_sl = [slice(g * LANES, (g + 1) * LANES) for g in range(CONV_CH // LANES)]
    u3 = u.reshape(n_seg, seg_rows, CONV_CH)

    @pl.when(t == 0)
    def _():
        for g, ls in enumerate(lane_sl):
            uext_ref[:, g, 0:HIST_PAD, :] = hist_ref[:, :, ls]

    for g, ls in enumerate(lane_sl):
        uext_ref[:, g, HIST_PAD:HIST_PAD + seg_rows, :] = u3[:, :, ls]

    half = CONV_ROWS // 2
    for s in range(n_seg):
        for r in range(seg_rows // CONV_ROWS):
            r0 = r * CONV_ROWS
            acc_even = jnp.zeros((half // SUBLANES, SUBLANES, CONV_CH), _F32)
            acc_odd = jnp.zeros((half // SUBLANES, SUBLANES, CONV_CH), _F32)
            for m in range(CONV_WIDTH + 1):
                start = r0 + m + (HIST_PAD - HIST)
                window = jnp.concatenate(
                    [uext_ref[s, g, pl.ds(start, half, stride=2), :] for g in range(len(lane_sl))], axis=1)
                window = window.reshape(half // SUBLANES, SUBLANES, CONV_CH)
                if m < CONV_WIDTH:
                    acc_even = acc_even + wdw_ref[m] * window
                if m >= 1:
                    acc_odd = acc_odd + wdw_ref[m - 1] * window
            acc_even = acc_even.reshape(half, CONV_CH)
            acc_odd = acc_odd.reshape(half, CONV_CH)
            for par, acc in enumerate((acc_even, acc_odd)):
                cv = acc + bdw_ref[...]
                mu = jnp.mean(cv, axis=-1, keepdims=True)
                xc = cv - mu
                y = xc * lax.rsqrt(jnp.mean(xc * xc, axis=-1, keepdims=True) + EPS)
                y = y * gcn_ref[...] + bcn_ref[...]
                y = y * jax.nn.sigmoid(y)
                base = s * seg_rows + r0 + par * half
                act_ref[base:base + half, :] = y.astype(_BF16)

    c_perm = _dot(act_ref[...], wpw_ref[...])
    for s in range(n_seg):
        for r in range(seg_rows // CONV_ROWS):
            r0 = r * CONV_ROWS
            base = s * seg_rows + r0
            for par in range(2):
                rows_par = c_perm[base + par * half:base + (par + 1) * half]
                for g, ls in enumerate(lane_sl):
                    cs_ref[s, g, pl.ds(r0 + par, half, stride=2), :] = rows_par[:, ls]
    for g, ls in enumerate(lane_sl):
        c_ref[:, :, ls] = cs_ref[:, g, :, :].astype(_BF16)

    @pl.when(t == n_t - 1)
    def _():
        for g, ls in enumerate(lane_sl):
            nconv_ref[:, :, ls] = uext_ref[:, g, seg_rows + (HIST_PAD - HIST):seg_rows + HIST_PAD, :]

    uext_ref[:, :, 0:HIST_PAD, :] = uext_ref[:, :, seg_rows:seg_rows + HIST_PAD, :]

    def head_norm(z, g_ref):
        ssq = _dot_split2(z * z, gmat_ref[...])
        return z * lax.rsqrt(ssq * (1.0 / HEAD_DIM) + EPS) * g_ref[...]

    q = _dot(h, win_ref[:, 2 * CONV_CH:2 * CONV_CH + ATTN_W])
    ssq_q = _dot((q * q).astype(_BF16), gmat_ref[...])
    q = q * lax.rsqrt(ssq_q * (1.0 / HEAD_DIM) + EPS) * gq_ref[...]
    q_ref[...] = (q * (HEAD_DIM ** -0.5)).astype(_BF16).reshape(q_ref.shape)
    if feature_major_kv:
        k_t = _dot_nt(wkv_ref[0:ATTN_W, :], h)
        gain = jnp.concatenate([gk_ref[...]] * (rows // LANES), axis=1)
        for hh in range(N_HEADS):
            rs = slice(hh * HEAD_DIM, (hh + 1) * HEAD_DIM)
            kh = k_t[rs]
            ssq = jnp.sum(kh * kh, axis=0, keepdims=True)
            k_ref[rs, :] = kh * lax.rsqrt(ssq * (1.0 / HEAD_DIM) + EPS) * gain[rs]
        v_ref[...] = _dot_nt(wkv_ref[ATTN_W:, :], h)
    else:
        k = _dot(h, win_ref[:, 2 * CONV_CH + ATTN_W:2 * CONV_CH + 2 * ATTN_W])
        k_ref[...] = head_norm(k, gk_ref).reshape(k_ref.shape)
        v_ref[...] = _dot(h, win_ref[:, 2 * CONV_CH + 2 * ATTN_W:]).reshape(v_ref.shape)


def _inproj(x, hist, hist_layer, w, *, layer, n_seg, seg_rows, kv_stack=None):
    batch, t_len, d_model = x.shape
    depth = w["win"].shape[0]
    assert batch % n_seg == 0 and t_len % seg_rows == 0 and seg_rows % CONV_ROWS == 0
    n_t = t_len // seg_rows
    assert n_seg == 1 or n_t == 1
    feature_major_kv = kv_stack is not None
    rows_spec = lambda width: pl.BlockSpec((n_seg, seg_rows, width), lambda b, t: (b, t, 0))
    shared = lambda arr: pl.BlockSpec(arr.shape, lambda b, t: (0,) * arr.ndim)
    const = lambda arr: pl.BlockSpec((None,) + arr.shape[1:], lambda b, t: (layer,) + (0,) * (arr.ndim - 1))
    hist_spec = pl.BlockSpec((None, n_seg, HIST_PAD, CONV_CH), lambda b, t: (hist_layer, b, 0, 0))
    nconv_spec = pl.BlockSpec((n_seg, HIST, CONV_CH), lambda b, t: (b, 0, 0))
    small = [w["wdw"], w["bdw"], w["gcn"], w["bcn"], w["wpw"]]
    aliases = {}
    if feature_major_kv:
        assert n_seg == 1 and seg_rows % LANES == 0
        args = [x, hist, w["gmix"], w["win"], w["w_kv_t"]] + small + [w["gq"], w["gk_col"], w["gmat"]]
        in_specs = [rows_spec(d_model), hist_spec] + [const(a) for a in args[2:-1]] + [shared(w["gmat"])]
        kv_shape = jax.ShapeDtypeStruct((depth, batch, ATTN_W, t_len), _F32)
        first_layer = kv_stack[0] is None
        if first_layer:
            kv_spec = pl.BlockSpec((depth, None, ATTN_W, seg_rows), lambda b, t: (0, b, 0, t))
            dummy = jnp.zeros((SUBLANES, LANES), _F32)
            args += [dummy, dummy]
            in_specs += [shared(dummy)] * 2
        else:
            kv_spec = pl.BlockSpec((None, None, ATTN_W, seg_rows), lambda b, t: (layer, b, 0, t))
            aliases = {len(args): 2, len(args) + 1: 3}
            args += list(kv_stack)
            in_specs += [pl.BlockSpec(memory_space=pl.ANY)] * 2
    else:
        args = [x, hist, w["gmix"], w["win"]] + small + [w["gq"], w["gk"], w["gmat"]]
        in_specs = [rows_spec(d_model), hist_spec] + [const(a) for a in args[2:-1]] + [shared(w["gmat"])]
        kv_shape = jax.ShapeDtypeStruct((batch, t_len, ATTN_W), _F32)
        kv_spec = rows_spec(ATTN_W)
        first_layer = False
    out_shape = (
        jax.ShapeDtypeStruct((batch, t_len, CONV_CH), _BF16),
        jax.ShapeDtypeStruct((batch, t_len, ATTN_W), _BF16),
        kv_shape, kv_shape,
        jax.ShapeDtypeStruct((batch, HIST, CONV_CH), _F32),
    )
    return pl.pallas_call(
        functools.partial(_inproj_kernel, n_seg=n_seg, seg_rows=seg_rows, feature_major_kv=feature_major_kv,
                          first_layer=first_layer),
        out_shape=out_shape,
        grid=(batch // n_seg, n_t),
        in_specs=in_specs,
        out_specs=(rows_spec(CONV_CH), rows_spec(ATTN_W), kv_spec, kv_spec, nconv_spec),
        scratch_shapes=[pltpu.VMEM((n_seg, CONV_CH // LANES, seg_rows + HIST_PAD, LANES), _F32),
                        pltpu.VMEM((n_seg, CONV_CH // LANES, seg_rows, LANES), _F32),
                        pltpu.VMEM((n_seg * seg_rows, CONV_CH), _BF16)],
        input_output_aliases=aliases,
        compiler_params=pltpu.CompilerParams(dimension_semantics=("arbitrary", "arbitrary"),
                                             vmem_limit_bytes=VMEM_LIMIT_BYTES),
        name="inproj_conv",
    )(*args)


def _attn_kernel(*refs, n_sub, past_chunks0, chunks_per_step, layer, own_feature_major):
    n_fm = n_sub + 2 if own_feature_major else 2
    q_ref, k_fm, v_fm, rest = refs[0], refs[1:1 + n_fm], refs[1 + n_fm:1 + 2 * n_fm], refs[1 + 2 * n_fm:]
    if not own_feature_major:
        kd_ref, vd_ref, rest = rest[0], rest[1], rest[2:]
    kany_ref, vany_ref, ud_ref, up_ref, o_ref, oacc_ref, run_ref, kbuf_ref, vbuf_ref, sem_ref = rest
    b = pl.program_id(0)
    i = pl.program_id(1)
    tq = q_ref.shape[0] // n_sub
    n_pairs = ATTN_W // LANES
    subs = range(n_sub)
    n_past = [past_chunks0 + i * chunks_per_step + j for j in subs]
    even_head = lax.broadcasted_iota(jnp.int32, (tq, LANES), 1) < HEAD_DIM
    pair_sl = [slice(p * LANES, (p + 1) * LANES) for p in range(n_pairs)]

    def stacked_queries(j):
        qs = []
        for sl in pair_sl:
            qp = q_ref[j * tq:(j + 1) * tq, sl]
            qs.append(jnp.concatenate([jnp.where(even_head, qp, 0.0), jnp.where(even_head, 0.0, qp)], axis=0))
        return qs

    def scores_fm(qs, key_refs):
        out = []
        for p, sl in enumerate(pair_sl):
            keys = jnp.concatenate([r[sl, :] for r in key_refs], axis=1).astype(_BF16)
            out.append(_dot(qs[p], keys))
        return jnp.concatenate(out, axis=0)

    def log_terms(z):
        sp = jnp.log(1.0 + jnp.exp(-jnp.abs(z)))
        neg_part = jnp.minimum(z, 0.0)
        return neg_part - sp, (neg_part - z) - sp

    def suffix_sums(log_1m, u_ref):
        hi = log_1m.astype(_BF16)
        lo = (log_1m - hi.astype(_F32)).astype(_BF16)
        r = _dot(jnp.concatenate([hi, lo], axis=1), u_ref[...])
        return r[:, LANES:], r[:, :LANES]

    def weighted_values(fm_weights, fm_value_refs, rm_weight=None, rm_value_ref=None):
        wb = jnp.concatenate(fm_weights, axis=1).astype(_BF16)
        outs = []
        for p, sl in enumerate(pair_sl):
            rows = slice(2 * tq * p, 2 * tq * (p + 1))
            vals = jnp.concatenate([r[sl, :] for r in fm_value_refs], axis=1).astype(_BF16)
            pv = _dot_nt(wb[rows], vals)
            if rm_weight is not None:
                pv = pv + _dot(rm_weight[rows].astype(_BF16), rm_value_ref[:, sl].astype(_BF16))
            outs.append(jnp.where(even_head, pv[:tq], pv[tq:]))
        return jnp.concatenate(outs, axis=1)

    qs = [stacked_queries(j) for j in subs]
    if own_feature_major:
        z = [scores_fm(qs[j], k_fm[j:j + 3]) for j in subs]
        z_own = [zj[:, 2 * KEY_CHUNK:] for zj in z]
    else:
        z = [scores_fm(qs[0], k_fm)]
        z_own = [jnp.concatenate([_dot_nt(qs[0][p], kd_ref[:, sl].astype(_BF16))
                                  for p, sl in enumerate(pair_sl)], axis=0)]
    q_row = lax.broadcasted_iota(jnp.int32, (N_HEADS * tq, tq), 0) & (tq - 1)
    causal = lax.broadcasted_iota(jnp.int32, (N_HEADS * tq, tq), 1) < q_row
    def past_scores(j, back):
        zc = z[j][:, (2 - back) * KEY_CHUNK:(3 - back) * KEY_CHUNK]
        if past_chunks0 + j >= back:
            return zc
        return zc + jnp.where(n_past[j] >= back, 0.0, MASKED_SCORE)

    z_2 = [past_scores(j, 2) for j in subs]
    z_1 = [past_scores(j, 1) for j in subs]
    z_d = [jnp.where(causal, z_own[j], MASKED_SCORE) for j in subs]
    t_d = [log_terms(zz) for zz in z_d]
    t_1 = [log_terms(zz) for zz in z_1]
    t_2 = [log_terms(zz) for zz in z_2]
    s_d = [suffix_sums(t_d[j][1], ud_ref) for j in subs]
    s_1 = [suffix_sums(t_1[j][1], up_ref) for j in subs]
    s_2 = [suffix_sums(t_2[j][1], up_ref) for j in subs]
    a_d = [jnp.exp(t_d[j][0] + s_d[j][0]) for j in subs]
    a_1 = [jnp.exp(t_1[j][0] + (s_1[j][0] + s_d[j][1])) for j in subs]
    run_1 = [s_d[j][1] + s_1[j][1] for j in subs]
    a_2 = [jnp.exp(t_2[j][0] + (s_2[j][0] + run_1[j])) for j in subs]
    run = [run_1[j] + s_2[j][1] for j in subs]
    for j in subs:
        if own_feature_major:
            out = weighted_values((a_2[j], a_1[j], a_d[j]), v_fm[j:j + 3])
        else:
            out = weighted_values((a_2[j], a_1[j]), v_fm, a_d[j], vd_ref)
        oacc_ref[j * tq:(j + 1) * tq, :] = out
        run_ref[j] = run[j]

    def more(carry):
        c, m = carry
        return jnp.logical_and(c >= 0, m > STICK_LOG_FLOOR)

    def older(j, carry):
        c, _ = carry
        start = pl.multiple_of(c * KEY_CHUNK, KEY_CHUNK)
        k_copy = pltpu.make_async_copy(kany_ref.at[layer, b, :, pl.ds(start, KEY_CHUNK)], kbuf_ref, sem_ref.at[0])
        v_copy = pltpu.make_async_copy(vany_ref.at[layer, b, :, pl.ds(start, KEY_CHUNK)], vbuf_ref, sem_ref.at[1])
        k_copy.start()
        v_copy.start()
        k_copy.wait()
        v_copy.wait()
        lb, l1 = log_terms(scores_fm(stacked_queries(j), (kbuf_ref,)))
        inside, tot = suffix_sums(l1, up_ref)
        run_c = run_ref[j]
        oacc_ref[j * tq:(j + 1) * tq, :] += weighted_values((jnp.exp(lb + (inside + run_c)),), (vbuf_ref,))
        run_c = run_c + tot
        run_ref[j] = run_c
        return c - 1, jnp.max(run_c)

    for j in subs:
        lax.while_loop(more, functools.partial(older, j), (n_past[j] - 3, jnp.max(run[j])))

    o_ref[...] = oacc_ref[...].astype(o_ref.dtype)


def _attention(q, k_own, v_own, k_past, v_past, *, layer, past_is_own):
    batch, t_len, _ = q.shape
    if past_is_own:
        tq = KEY_CHUNK
        n_sub = ATTN_BLOCKS_PER_STEP if t_len % (ATTN_BLOCKS_PER_STEP * tq) == 0 else 1
        past_chunks0, chunks_per_step, n_fm = 0, n_sub, n_sub + 2
    else:
        tq, n_sub = t_len, 1
        assert k_past.shape[-1] % KEY_CHUNK == 0
        past_chunks0, chunks_per_step, n_fm = k_past.shape[-1] // KEY_CHUNK, 0, 2
    assert t_len % (n_sub * tq) == 0 and tq <= KEY_CHUNK and tq & (tq - 1) == 0
    n_steps = t_len // (n_sub * tq)
    assert past_is_own or n_steps == 1

    def newer(n):
        u = lax.broadcasted_iota(jnp.int32, (n, n), 0) > lax.broadcasted_iota(jnp.int32, (n, n), 1)
        once = jnp.concatenate([jnp.ones((n, LANES), _BF16), u.astype(_BF16)], axis=1)
        return jnp.concatenate([once, once], axis=0)
    u_diag, u_past = newer(tq), newer(KEY_CHUNK)

    def fm_chunk(m):
        imap = lambda b, i: (layer, b, 0, jnp.maximum(past_chunks0 + i * chunks_per_step + m - 2, 0))
        return pl.BlockSpec((None, None, ATTN_W, KEY_CHUNK), imap)

    rm_block = pl.BlockSpec((None, n_sub * tq, ATTN_W), lambda b, i: (b, i, 0))
    const = lambda arr: pl.BlockSpec(arr.shape, lambda b, i: (0,) * arr.ndim)
    any_spec = pl.BlockSpec(memory_space=pl.ANY)
    fm_specs = [fm_chunk(m) for m in range(n_fm)]
    args = [q] + [k_past] * n_fm + [v_past] * n_fm
    in_specs = [rm_block] + fm_specs + fm_specs
    if not past_is_own:
        args += [k_own, v_own]
        in_specs += [rm_block, rm_block]
    args += [k_past, v_past, u_diag, u_past]
    in_specs += [any_spec, any_spec, const(u_diag), const(u_past)]
    return pl.pallas_call(
        functools.partial(_attn_kernel, n_sub=n_sub, past_chunks0=past_chunks0, chunks_per_step=chunks_per_step,
                          layer=layer, own_feature_major=past_is_own),
        out_shape=jax.ShapeDtypeStruct((batch, t_len, ATTN_W), _BF16),
        grid=(batch, n_steps),
        in_specs=in_specs,
        out_specs=rm_block,
        scratch_shapes=[pltpu.VMEM((n_sub * tq, ATTN_W), _F32),
                        pltpu.VMEM((n_sub, N_HEADS * tq, LANES), _F32),
                        pltpu.VMEM((ATTN_W, KEY_CHUNK), _F32),
                        pltpu.VMEM((ATTN_W, KEY_CHUNK), _F32),
                        pltpu.SemaphoreType.DMA((2,))],
        compiler_params=pltpu.CompilerParams(dimension_semantics=("arbitrary", "arbitrary"),
                                             vmem_limit_bytes=VMEM_LIMIT_BYTES),
        name="stick_attention",
    )(*args)


def _outffn_kernel(xa_ref, ca_ref, oa_ref, xb_ref, cb_ref, ob_ref, wout_any, gff_ref, w1_any, w2_any,
                   ya_ref, yb_ref, wout_ref, w1_ref, w2_ref, stage_ref, sem_ref, *, n_first, layer):
    d_ff = w1_ref.shape[1]
    r = pl.program_id(0)

    @pl.when(r == 0)
    def _():
        slots, rows, cols = stage_ref.shape
        tiles = [(src, dst, i, j)
                 for src, dst in ((wout_any, wout_ref), (w1_any, w1_ref), (w2_any, w2_ref))
                 for i in range(dst.shape[0] // rows) for j in range(dst.shape[1] // cols)]

        def tile_copy(n):
            src, _, i, j = tiles[n]
            return pltpu.make_async_copy(src.at[layer, pl.ds(i * rows, rows), pl.ds(j * cols, cols)],
                                         stage_ref.at[n % slots], sem_ref.at[n % slots])
        for n in range(min(slots - 1, len(tiles))):
            tile_copy(n).start()
        for n, (_, dst, i, j) in enumerate(tiles):
            if n + slots - 1 < len(tiles):
                tile_copy(n + slots - 1).start()
            tile_copy(n).wait()
            dst[i * rows:(i + 1) * rows, j * cols:(j + 1) * cols] = stage_ref[n % slots].astype(_BF16)

    def mix_and_ffn(x_ref, c_ref, o_ref, y_ref):
        x1 = x_ref[...] + _dot(c_ref[...], wout_ref[0:CONV_CH, :]) + _dot(o_ref[...], wout_ref[CONV_CH:, :])
        h = x1 * lax.rsqrt(jnp.mean(x1 * x1, axis=-1, keepdims=True) + EPS) * gff_ref[...]
        h = h.astype(_BF16)
        ffn = None
        for cc in range(d_ff // FF_CHUNK):
            f = jnp.maximum(_dot(h, w1_ref[:, cc * FF_CHUNK:(cc + 1) * FF_CHUNK]), 0.0)
            part = _dot((f * f).astype(_BF16), w2_ref[cc * FF_CHUNK:(cc + 1) * FF_CHUNK, :])
            ffn = part if ffn is None else ffn + part
        y_ref[...] = x1 + ffn

    @pl.when(r < n_first)
    def _():
        mix_and_ffn(xa_ref, ca_ref, oa_ref, ya_ref)

    @pl.when(r >= n_first)
    def _():
        mix_and_ffn(xb_ref, cb_ref, ob_ref, yb_ref)


def _outffn(xa, ca, oa, xb, cb, ob, wout, gff, w1, w2, *, layer):
    d_model = xa.shape[1]
    d_ff = w1.shape[-1]
    tb = ROW_BLOCK
    assert xa.shape[0] % tb == 0 and xb.shape[0] % tb == 0 and d_ff % FF_CHUNK == 0
    assert d_model % WEIGHT_STAGE_ROWS == 0 and d_ff % WEIGHT_STAGE_ROWS == 0
    na, nb = xa.shape[0] // tb, xb.shape[0] // tb
    first = lambda width: pl.BlockSpec((tb, width), lambda r: (jnp.minimum(r, na - 1), 0))
    second = lambda width: pl.BlockSpec((tb, width), lambda r: (jnp.maximum(r - na, 0), 0))
    any_spec = pl.BlockSpec(memory_space=pl.ANY)
    gff_spec = pl.BlockSpec((None,) + gff.shape[1:], lambda r: (layer, 0, 0))
    return pl.pallas_call(
        functools.partial(_outffn_kernel, n_first=na, layer=layer),
        out_shape=(jax.ShapeDtypeStruct(xa.shape, _F32), jax.ShapeDtypeStruct(xb.shape, _F32)),
        grid=(na + nb,),
        in_specs=[first(d_model), first(CONV_CH), first(ATTN_W), second(d_model), second(CONV_CH), second(ATTN_W),
                  any_spec, gff_spec, any_spec, any_spec],
        out_specs=(first(d_model), second(d_model)),
        scratch_shapes=[pltpu.VMEM(wout.shape[1:], _BF16), pltpu.VMEM(w1.shape[1:], _BF16),
                        pltpu.VMEM(w2.shape[1:], _BF16),
                        pltpu.VMEM((WEIGHT_STAGE_SLOTS, WEIGHT_STAGE_ROWS, d_model), _F32),
                        pltpu.SemaphoreType.DMA((WEIGHT_STAGE_SLOTS,))],
        compiler_params=pltpu.CompilerParams(dimension_semantics=("arbitrary",),
                                             vmem_limit_bytes=VMEM_LIMIT_BYTES),
        name="outproj_ffn",
    )(xa, ca, oa, xb, cb, ob, wout, gff, w1, w2)


def kernel(x_prompt, x_sample, cache_k, cache_v, state_conv, g_mix, w_in, w_dw, b_dw, g_cn, b_cn, w_pw,
           g_q, g_k, w_out, g_ff, w_ff1, w_ff2):
    depth = w_in.shape[0]
    bp, tp, _ = x_prompt.shape
    bs, ts, _ = x_sample.shape
    n_past = cache_k.shape[2]
    head_id = lax.broadcasted_iota(jnp.int32, (ATTN_W, ATTN_W), 0) // HEAD_DIM
    gmat = (head_id == head_id.T).astype(_BF16)
    zero_hist = jnp.zeros((1, bp, HIST_PAD, CONV_CH), _F32)
    sample_hist = jnp.pad(state_conv, ((0, 0), (0, 0), (HIST_PAD - HIST, 0), (0, 0)))
    row = lambda a: a.reshape(depth, 1, -1)
    per_head = lambda g: jnp.tile(g, (1, N_HEADS))
    win = w_in.astype(_BF16)
    w = dict(gmix=row(g_mix), win=win, w_kv_t=jnp.transpose(win[:, :, 2 * CONV_CH + ATTN_W:], (0, 2, 1)),
             wdw=jnp.broadcast_to(w_dw[:, :, None, :], w_dw.shape[:2] + (SUBLANES, CONV_CH)), bdw=row(b_dw), gcn=row(g_cn), bcn=row(b_cn), wpw=w_pw.astype(_BF16),
             gq=row(per_head(g_q)), gk=row(per_head(g_k)),
             gk_col=jnp.broadcast_to(per_head(g_k)[:, :, None], (depth, ATTN_W, LANES)), gmat=gmat,
             wout=w_out, gff=row(g_ff), w1=w_ff1, w2=w_ff2)
    to_fm = lambda a: jnp.transpose(a, (0, 1, 3, 4, 2)).reshape(a.shape[0], a.shape[1], ATTN_W, a.shape[2])
    from_fm = lambda a: jnp.transpose(a.reshape(a.shape[0], a.shape[1], N_HEADS, HEAD_DIM, a.shape[3]),
                                      (0, 1, 4, 2, 3))
    cache_k_fm, cache_v_fm = to_fm(cache_k), to_fm(cache_v)

    hp, hs = x_prompt, x_sample
    kv_stack = (None, None)
    conv_p, k_s, v_s, conv_s = [], [], [], []
    flat = lambda a: a.reshape(-1, a.shape[-1])
    for l in range(depth):
        cp, q, k_fm, v_fm, cv = _inproj(hp, zero_hist, 0, w, layer=l, n_seg=1, seg_rows=min(ROW_BLOCK, tp),
                                        kv_stack=kv_stack)
        kv_stack = (k_fm, v_fm)
        op = _attention(q, None, None, k_fm, v_fm, layer=l, past_is_own=True)
        conv_p.append(cv)
        cs, q, k, v, cv = _inproj(hs, sample_hist, l, w, layer=l, n_seg=bs, seg_rows=ts)
        os_ = _attention(q, k, v, cache_k_fm, cache_v_fm, layer=l, past_is_own=False)
        k_s.append(k.reshape(bs, ts, N_HEADS, HEAD_DIM))
        v_s.append(v.reshape(bs, ts, N_HEADS, HEAD_DIM))
        conv_s.append(cv)
        yp, ys = _outffn(flat(hp), flat(cp), flat(op), flat(hs), flat(cs), flat(os_),
                         w["wout"], w["gff"], w["w1"], w["w2"], layer=l)
        hp, hs = yp.reshape(hp.shape), ys.reshape(hs.shape)
    return (hp, hs, from_fm(kv_stack[0]), from_fm(kv_stack[1]), jnp.stack(conv_p),
            jnp.stack(k_s), jnp.stack(v_s), jnp.stack(conv_s))
```

```python
import functools

import jax
import jax.numpy as jnp
from jax import lax
from jax.experimental import pallas as pl
from jax.experimental.pallas import tpu as pltpu

EPS = 1e-6
CONV_WIDTH = 31
HIST = CONV_WIDTH - 1
HIST_PAD = 32
SUBLANES = 8
N_HEADS = 8
HEAD_DIM = 64
ATTN_W = N_HEADS * HEAD_DIM
CONV_CH = 512
LANES = 128
KEY_CHUNK = 128
CONV_ROWS = 64
FF_CHUNK = 1024
ROW_BLOCK = 512
WEIGHT_STAGE_ROWS = 256
WEIGHT_STAGE_SLOTS = 8
VMEM_LIMIT_BYTES = 56 * 1024 * 1024
ATTN_BLOCKS_PER_STEP = 4
STICK_LOG_FLOOR = -87.5
MASKED_SCORE = -1e30

_F32 = jnp.float32
_BF16 = jnp.bfloat16


def _dot(a, b):
    return jnp.dot(a, b, preferred_element_type=_F32)


def _dot_nt(a, b):
    return lax.dot_general(a, b, (((1,), (1,)), ((), ())), preferred_element_type=_F32)


def _dot_split2(x, m):
    hi = x.astype(_BF16)
    lo = (x - hi.astype(_F32)).astype(_BF16)
    return _dot(hi, m) + _dot(lo, m)


def _inproj_kernel(*refs, n_seg, seg_rows, feature_major_kv, first_layer):
    if feature_major_kv:
        (x_ref, hist_ref, gmix_ref, win_ref, wkv_ref, wdw_ref, bdw_ref, gcn_ref, bcn_ref, wpw_ref,
         gq_ref, gk_ref, gmat_ref, _, _,
         c_ref, q_ref, k_ref, v_ref, nconv_ref, uext_ref, cs_ref, act_ref) = refs
        if first_layer:
            for stack_ref in (k_ref, v_ref):
                if stack_ref.shape[0] > 1:
                    stack_ref[1:] = jnp.zeros((stack_ref.shape[0] - 1,) + stack_ref.shape[1:], _F32)
            k_ref, v_ref = k_ref.at[0], v_ref.at[0]
    else:
        (x_ref, hist_ref, gmix_ref, win_ref, wdw_ref, bdw_ref, gcn_ref, bcn_ref, wpw_ref,
         gq_ref, gk_ref, gmat_ref,
         c_ref, q_ref, k_ref, v_ref, nconv_ref, uext_ref, cs_ref, act_ref) = refs
    t = pl.program_id(1)
    n_t = pl.num_programs(1)
    rows = n_seg * seg_rows
    d_model = x_ref.shape[-1]

    x = x_ref[...].reshape(rows, d_model)
    h = x * lax.rsqrt(jnp.mean(x * x, axis=-1, keepdims=True) + EPS) * gmix_ref[...]
    h = h.astype(_BF16)

    a = _dot(h, win_ref[:, 0:CONV_CH])
    gate = _dot(h, win_ref[:, CONV_CH:2 * CONV_CH])
    u = a * jax.nn.sigmoid(gate)

    def head_norm(z, g_ref):
        ssq = _dot_split2(z * z, gmat_ref[...])
        return z * lax.rsqrt(ssq * (1.0 / HEAD_DIM) + EPS) * g_ref[...]

    q = _dot(h, win_ref[:, 2 * CONV_CH:2 * CONV_CH + ATTN_W])
    ssq_q = _dot((q * q).astype(_BF16), gmat_ref[...])
    q = q * lax.rsqrt(ssq_q * (1.0 / HEAD_DIM) + EPS) * gq_ref[...]
    q_ref[...] = (q * (HEAD_DIM ** -0.5)).astype(_BF16).reshape(q_ref.shape)
    if feature_major_kv:
        k_t = _dot_nt(wkv_ref[0:ATTN_W, :], h)
        gain = jnp.concatenate([gk_ref[...]] * (rows // LANES), axis=1)
        for hh in range(N_HEADS):
            rs = slice(hh * HEAD_DIM, (hh + 1) * HEAD_DIM)
            kh = k_t[rs]
            ssq = jnp.sum(kh * kh, axis=0, keepdims=True)
            k_ref[rs, :] = kh * lax.rsqrt(ssq * (1.0 / HEAD_DIM) + EPS) * gain[rs]
        v_ref[...] = _dot_nt(wkv_ref[ATTN_W:, :], h)
    else:
        k = _dot(h, win_ref[:, 2 * CONV_CH + ATTN_W:2 * CONV_CH + 2 * ATTN_W])
        k_ref[...] = head_norm(k, gk_ref).reshape(k_ref.shape)
        v_ref[...] = _dot(h, win_ref[:, 2 * CONV_CH + 2 * ATTN_W:]).reshape(v_ref.shape)

    lane_sl = [slice(g * LANES, (g + 1) * LANES) for g in range(CONV_CH // LANES)]
    u3 = u.reshape(n_seg, seg_rows, CONV_CH)

    @pl.when(t == 0)
    def _():
        for g, ls in enumerate(lane_sl):
            uext_ref[:, g, 0:HIST_PAD, :] = hist_ref[:, :, ls]

    for g, ls in enumerate(lane_sl):
        uext_ref[:, g, HIST_PAD:HIST_PAD + seg_rows, :] = u3[:, :, ls]

    half = CONV_ROWS // 2
    for s in range(n_seg):
        for r in range(seg_rows // CONV_ROWS):
            r0 = r * CONV_ROWS
            acc_even = jnp.zeros((half // SUBLANES, SUBLANES, CONV_CH), _F32)
            acc_odd = jnp.zeros((half // SUBLANES, SUBLANES, CONV_CH), _F32)
            for m in range(CONV_WIDTH + 1):
                start = r0 + m + (HIST_PAD - HIST)
                window = jnp.concatenate(
                    [uext_ref[s, g, pl.ds(start, half, stride=2), :] for g in range(len(lane_sl))], axis=1)
                window = window.reshape(half // SUBLANES, SUBLANES, CONV_CH)
                if m < CONV_WIDTH:
                    acc_even = acc_even + wdw_ref[m] * window
                if m >= 1:
                    acc_odd = acc_odd + wdw_ref[m - 1] * window
            acc_even = acc_even.reshape(half, CONV_CH)
            acc_odd = acc_odd.reshape(half, CONV_CH)
            for par, acc in enumerate((acc_even, acc_odd)):
                cv = acc + bdw_ref[...]
                mu = jnp.mean(cv, axis=-1, keepdims=True)
                xc = cv - mu
                y = xc * lax.rsqrt(jnp.mean(xc * xc, axis=-1, keepdims=True) + EPS)
                y = y * gcn_ref[...] + bcn_ref[...]
                y = y * jax.nn.sigmoid(y)
                base = s * seg_rows + r0 + par * half
                act_ref[base:base + half, :] = y.astype(_BF16)

    c_perm = _dot(act_ref[...], wpw_ref[...])
    for s in range(n_seg):
        for r in range(seg_rows // CONV_ROWS):
            r0 = r * CONV_ROWS
            base = s * seg_rows + r0
            for par in range(2):
                rows_par = c_perm[base + par * half:base + (par + 1) * half]
                for g, ls in enumerate(lane_sl):
                    cs_ref[s, g, pl.ds(r0 + par, half, stride=2), :] = rows_par[:, ls]
    for g, ls in enumerate(lane_sl):
        c_ref[:, :, ls] = cs_ref[:, g, :, :].astype(_BF16)

    @pl.when(t == n_t - 1)
    def _():
        for g, ls in enumerate(lane_sl):
            nconv_ref[:, :, ls] = uext_ref[:, g, seg_rows + (HIST_PAD - HIST):seg_rows + HIST_PAD, :]

    uext_ref[:, :, 0:HIST_PAD, :] = uext_ref[:, :, seg_rows:seg_rows + HIST_PAD, :]


def _inproj(x, hist, hist_layer, w, *, layer, n_seg, seg_rows, kv_stack=None):
    batch, t_len, d_model = x.shape
    depth = w["win"].shape[0]
    assert batch % n_seg == 0 and t_len % seg_rows == 0 and seg_rows % CONV_ROWS == 0
    n_t = t_len // seg_rows
    assert n_seg == 1 or n_t == 1
    feature_major_kv = kv_stack is not None
    rows_spec = lambda width: pl.BlockSpec((n_seg, seg_rows, width), lambda b, t: (b, t, 0))
    shared = lambda arr: pl.BlockSpec(arr.shape, lambda b, t: (0,) * arr.ndim)
    const = lambda arr: pl.BlockSpec((None,) + arr.shape[1:], lambda b, t: (layer,) + (0,) * (arr.ndim - 1))
    hist_spec = pl.BlockSpec((None, n_seg, HIST_PAD, CONV_CH), lambda b, t: (hist_layer, b, 0, 0))
    nconv_spec = pl.BlockSpec((n_seg, HIST, CONV_CH), lambda b, t: (b, 0, 0))
    small = [w["wdw"], w["bdw"], w["gcn"], w["bcn"], w["wpw"]]
    aliases = {}
    if feature_major_kv:
        assert n_seg == 1 and seg_rows % LANES == 0
        args = [x, hist, w["gmix"], w["win"], w["w_kv_t"]] + small + [w["gq"], w["gk_col"], w["gmat"]]
        in_specs = [rows_spec(d_model), hist_spec] + [const(a) for a in args[2:-1]] + [shared(w["gmat"])]
        kv_shape = jax.ShapeDtypeStruct((depth, batch, ATTN_W, t_len), _F32)
        first_layer = kv_stack[0] is None
        if first_layer:
            kv_spec = pl.BlockSpec((depth, None, ATTN_W, seg_rows), lambda b, t: (0, b, 0, t))
            dummy = jnp.zeros((SUBLANES, LANES), _F32)
            args += [dummy, dummy]
            in_specs += [shared(dummy)] * 2
        else:
            kv_spec = pl.BlockSpec((None, None, ATTN_W, seg_rows), lambda b, t: (layer, b, 0, t))
            aliases = {len(args): 2, len(args) + 1: 3}
            args += list(kv_stack)
            in_specs += [pl.BlockSpec(memory_space=pl.ANY)] * 2
    else:
        args = [x, hist, w["gmix"], w["win"]] + small + [w["gq"], w["gk"], w["gmat"]]
        in_specs = [rows_spec(d_model), hist_spec] + [const(a) for a in args[2:-1]] + [shared(w["gmat"])]
        kv_shape = jax.ShapeDtypeStruct((batch, t_len, ATTN_W), _F32)
        kv_spec = rows_spec(ATTN_W)
        first_layer = False
    out_shape = (
        jax.ShapeDtypeStruct((batch, t_len, CONV_CH), _BF16),
        jax.ShapeDtypeStruct((batch, t_len, ATTN_W), _BF16),
        kv_shape, kv_shape,
        jax.ShapeDtypeStruct((batch, HIST, CONV_CH), _F32),
    )
    return pl.pallas_call(
        functools.partial(_inproj_kernel, n_seg=n_seg, seg_rows=seg_rows, feature_major_kv=feature_major_kv,
                          first_layer=first_layer),
        out_shape=out_shape,
        grid=(batch // n_seg, n_t),
        in_specs=in_specs,
        out_specs=(rows_spec(CONV_CH), rows_spec(ATTN_W), kv_spec, kv_spec, nconv_spec),
        scratch_shapes=[pltpu.VMEM((n_seg, CONV_CH // LANES, seg_rows + HIST_PAD, LANES), _F32),
                        pltpu.VMEM((n_seg, CONV_CH // LANES, seg_rows, LANES), _F32),
                        pltpu.VMEM((n_seg * seg_rows, CONV_CH), _BF16)],
        input_output_aliases=aliases,
        compiler_params=pltpu.CompilerParams(dimension_semantics=("arbitrary", "arbitrary"),
                                             vmem_limit_bytes=VMEM_LIMIT_BYTES),
        name="inproj_conv",
    )(*args)


def _attn_kernel(*refs, n_sub, past_chunks0, chunks_per_step, layer, own_feature_major):
    n_fm = n_sub + 2 if own_feature_major else 2
    q_ref, k_fm, v_fm, rest = refs[0], refs[1:1 + n_fm], refs[1 + n_fm:1 + 2 * n_fm], refs[1 + 2 * n_fm:]
    if not own_feature_major:
        kd_ref, vd_ref, rest = rest[0], rest[1], rest[2:]
    kany_ref, vany_ref, ud_ref, up_ref, o_ref, oacc_ref, run_ref, kbuf_ref, vbuf_ref, sem_ref = rest
    b = pl.program_id(0)
    i = pl.program_id(1)
    tq = q_ref.shape[0] // n_sub
    n_pairs = ATTN_W // LANES
    subs = range(n_sub)
    n_past = [past_chunks0 + i * chunks_per_step + j for j in subs]
    even_head = lax.broadcasted_iota(jnp.int32, (tq, LANES), 1) < HEAD_DIM
    pair_sl = [slice(p * LANES, (p + 1) * LANES) for p in range(n_pairs)]

    def stacked_queries(j):
        qs = []
        for sl in pair_sl:
            qp = q_ref[j * tq:(j + 1) * tq, sl]
            qs.append(jnp.concatenate([jnp.where(even_head, qp, 0.0), jnp.where(even_head, 0.0, qp)], axis=0))
        return qs

    def scores_fm(qs, key_refs):
        out = []
        for p, sl in enumerate(pair_sl):
            keys = jnp.concatenate([r[sl, :] for r in key_refs], axis=1).astype(_BF16)
            out.append(_dot(qs[p], keys))
        return jnp.concatenate(out, axis=0)

    def log_terms(z):
        sp = jnp.log(1.0 + jnp.exp(-jnp.abs(z)))
        neg_part = jnp.minimum(z, 0.0)
        return neg_part - sp, (neg_part - z) - sp

    def suffix_sums(log_1m, u_ref):
        hi = log_1m.astype(_BF16)
        lo = (log_1m - hi.astype(_F32)).astype(_BF16)
        r = _dot(jnp.concatenate([hi, lo], axis=1), u_ref[...])
        return r[:, LANES:], r[:, :LANES]

    def weighted_values(fm_weights, fm_value_refs, rm_weight=None, rm_value_ref=None):
        wb = jnp.concatenate(fm_weights, axis=1).astype(_BF16)
        outs = []
        for p, sl in enumerate(pair_sl):
            rows = slice(2 * tq * p, 2 * tq * (p + 1))
            vals = jnp.concatenate([r[sl, :] for r in fm_value_refs], axis=1).astype(_BF16)
            pv = _dot_nt(wb[rows], vals)
            if rm_weight is not None:
                pv = pv + _dot(rm_weight[rows].astype(_BF16), rm_value_ref[:, sl].astype(_BF16))
            outs.append(jnp.where(even_head, pv[:tq], pv[tq:]))
        return jnp.concatenate(outs, axis=1)

    qs = [stacked_queries(j) for j in subs]
    if own_feature_major:
        z = [scores_fm(qs[j], k_fm[j:j + 3]) for j in subs]
        z_own = [zj[:, 2 * KEY_CHUNK:] for zj in z]
    else:
        z = [scores_fm(qs[0], k_fm)]
        z_own = [jnp.concatenate([_dot_nt(qs[0][p], kd_ref[:, sl].astype(_BF16))
                                  for p, sl in enumerate(pair_sl)], axis=0)]
    q_row = lax.broadcasted_iota(jnp.int32, (N_HEADS * tq, tq), 0) & (tq - 1)
    causal = lax.broadcasted_iota(jnp.int32, (N_HEADS * tq, tq), 1) < q_row
    def past_scores(j, back):
        zc = z[j][:, (2 - back) * KEY_CHUNK:(3 - back) * KEY_CHUNK]
        if past_chunks0 + j >= back:
            return zc
        return zc + jnp.where(n_past[j] >= back, 0.0, MASKED_SCORE)

    z_2 = [past_scores(j, 2) for j in subs]
    z_1 = [past_scores(j, 1) for j in subs]
    z_d = [jnp.where(causal, z_own[j], MASKED_SCORE) for j in subs]
    t_d = [log_terms(zz) for zz in z_d]
    t_1 = [log_terms(zz) for zz in z_1]
    t_2 = [log_terms(zz) for zz in z_2]
    s_d = [suffix_sums(t_d[j][1], ud_ref) for j in subs]
    s_1 = [suffix_sums(t_1[j][1], up_ref) for j in subs]
    s_2 = [suffix_sums(t_2[j][1], up_ref) for j in subs]
    a_d = [jnp.exp(t_d[j][0] + s_d[j][0]) for j in subs]
    a_1 = [jnp.exp(t_1[j][0] + (s_1[j][0] + s_d[j][1])) for j in subs]
    run_1 = [s_d[j][1] + s_1[j][1] for j in subs]
    a_2 = [jnp.exp(t_2[j][0] + (s_2[j][0] + run_1[j])) for j in subs]
    run = [run_1[j] + s_2[j][1] for j in subs]
    for j in subs:
        if own_feature_major:
            out = weighted_values((a_2[j], a_1[j], a_d[j]), v_fm[j:j + 3])
        else:
            out = weighted_values((a_2[j], a_1[j]), v_fm, a_d[j], vd_ref)
        oacc_ref[j * tq:(j + 1) * tq, :] = out
        run_ref[j] = run[j]

    def more(carry):
        c, m = carry
        return jnp.logical_and(c >= 0, m > STICK_LOG_FLOOR)

    def older(j, carry):
        c, _ = carry
        start = pl.multiple_of(c * KEY_CHUNK, KEY_CHUNK)
        k_copy = pltpu.make_async_copy(kany_ref.at[layer, b, :, pl.ds(start, KEY_CHUNK)], kbuf_ref, sem_ref.at[0])
        v_copy = pltpu.make_async_copy(vany_ref.at[layer, b, :, pl.ds(start, KEY_CHUNK)], vbuf_ref, sem_ref.at[1])
        k_copy.start()
        v_copy.start()
        k_copy.wait()
        v_copy.wait()
        lb, l1 = log_terms(scores_fm(stacked_queries(j), (kbuf_ref,)))
        inside, tot = suffix_sums(l1, up_ref)
        run_c = run_ref[j]
        oacc_ref[j * tq:(j + 1) * tq, :] += weighted_values((jnp.exp(lb + (inside + run_c)),), (vbuf_ref,))
        run_c = run_c + tot
        run_ref[j] = run_c
        return c - 1, jnp.max(run_c)

    for j in subs:
        lax.while_loop(more, functools.partial(older, j), (n_past[j] - 3, jnp.max(run[j])))

    o_ref[...] = oacc_ref[...].astype(o_ref.dtype)


def _attention(q, k_own, v_own, k_past, v_past, *, layer, past_is_own):
    batch, t_len, _ = q.shape
    if past_is_own:
        tq = KEY_CHUNK
        n_sub = ATTN_BLOCKS_PER_STEP if t_len % (ATTN_BLOCKS_PER_STEP * tq) == 0 else 1
        past_chunks0, chunks_per_step, n_fm = 0, n_sub, n_sub + 2
    else:
        tq, n_sub = t_len, 1
        assert k_past.shape[-1] % KEY_CHUNK == 0
        past_chunks0, chunks_per_step, n_fm = k_past.shape[-1] // KEY_CHUNK, 0, 2
    assert t_len % (n_sub * tq) == 0 and tq <= KEY_CHUNK and tq & (tq - 1) == 0
    n_steps = t_len // (n_sub * tq)
    assert past_is_own or n_steps == 1

    def newer(n):
        u = lax.broadcasted_iota(jnp.int32, (n, n), 0) > lax.broadcasted_iota(jnp.int32, (n, n), 1)
        once = jnp.concatenate([jnp.ones((n, LANES), _BF16), u.astype(_BF16)], axis=1)
        return jnp.concatenate([once, once], axis=0)
    u_diag, u_past = newer(tq), newer(KEY_CHUNK)

    def fm_chunk(m):
        imap = lambda b, i: (layer, b, 0, jnp.maximum(past_chunks0 + i * chunks_per_step + m - 2, 0))
        return pl.BlockSpec((None, None, ATTN_W, KEY_CHUNK), imap)

    rm_block = pl.BlockSpec((None, n_sub * tq, ATTN_W), lambda b, i: (b, i, 0))
    const = lambda arr: pl.BlockSpec(arr.shape, lambda b, i: (0,) * arr.ndim)
    any_spec = pl.BlockSpec(memory_space=pl.ANY)
    fm_specs = [fm_chunk(m) for m in range(n_fm)]
    args = [q] + [k_past] * n_fm + [v_past] * n_fm
    in_specs = [rm_block] + fm_specs + fm_specs
    if not past_is_own:
        args += [k_own, v_own]
        in_specs += [rm_block, rm_block]
    args += [k_past, v_past, u_diag, u_past]
    in_specs += [any_spec, any_spec, const(u_diag), const(u_past)]
    return pl.pallas_call(
        functools.partial(_attn_kernel, n_sub=n_sub, past_chunks0=past_chunks0, chunks_per_step=chunks_per_step,
                          layer=layer, own_feature_major=past_is_own),
        out_shape=jax.ShapeDtypeStruct((batch, t_len, ATTN_W), _BF16),
        grid=(batch, n_steps),
        in_specs=in_specs,
        out_specs=rm_block,
        scratch_shapes=[pltpu.VMEM((n_sub * tq, ATTN_W), _F32),
                        pltpu.VMEM((n_sub, N_HEADS * tq, LANES), _F32),
                        pltpu.VMEM((ATTN_W, KEY_CHUNK), _F32),
                        pltpu.VMEM((ATTN_W, KEY_CHUNK), _F32),
                        pltpu.SemaphoreType.DMA((2,))],
        compiler_params=pltpu.CompilerParams(dimension_semantics=("arbitrary", "arbitrary"),
                                             vmem_limit_bytes=VMEM_LIMIT_BYTES),
        name="stick_attention",
    )(*args)


def _outffn_kernel(xa_ref, ca_ref, oa_ref, xb_ref, cb_ref, ob_ref, wout_any, gff_ref, w1_any, w2_any,
                   ya_ref, yb_ref, wout_ref, w1_ref, w2_ref, stage_ref, sem_ref, *, n_first, layer):
    d_ff = w1_ref.shape[1]
    r = pl.program_id(0)

    @pl.when(r == 0)
    def _():
        slots, rows, cols = stage_ref.shape
        tiles = [(src, dst, i, j)
                 for src, dst in ((wout_any, wout_ref), (w1_any, w1_ref), (w2_any, w2_ref))
                 for i in range(dst.shape[0] // rows) for j in range(dst.shape[1] // cols)]

        def tile_copy(n):
            src, _, i, j = tiles[n]
            return pltpu.make_async_copy(src.at[layer, pl.ds(i * rows, rows), pl.ds(j * cols, cols)],
                                         stage_ref.at[n % slots], sem_ref.at[n % slots])
        for n in range(min(slots - 1, len(tiles))):
            tile_copy(n).start()
        for n, (_, dst, i, j) in enumerate(tiles):
            if n + slots - 1 < len(tiles):
                tile_copy(n + slots - 1).start()
            tile_copy(n).wait()
            dst[i * rows:(i + 1) * rows, j * cols:(j + 1) * cols] = stage_ref[n % slots].astype(_BF16)

    def mix_and_ffn(x_ref, c_ref, o_ref, y_ref):
        x1 = x_ref[...] + _dot(c_ref[...], wout_ref[0:CONV_CH, :]) + _dot(o_ref[...], wout_ref[CONV_CH:, :])
        h = x1 * lax.rsqrt(jnp.mean(x1 * x1, axis=-1, keepdims=True) + EPS) * gff_ref[...]
        h = h.astype(_BF16)
        ffn = None
        for cc in range(d_ff // FF_CHUNK):
            f = jnp.maximum(_dot(h, w1_ref[:, cc * FF_CHUNK:(cc + 1) * FF_CHUNK]), 0.0)
            part = _dot((f * f).astype(_BF16), w2_ref[cc * FF_CHUNK:(cc + 1) * FF_CHUNK, :])
            ffn = part if ffn is None else ffn + part
        y_ref[...] = x1 + ffn

    @pl.when(r < n_first)
    def _():
        mix_and_ffn(xa_ref, ca_ref, oa_ref, ya_ref)

    @pl.when(r >= n_first)
    def _():
        mix_and_ffn(xb_ref, cb_ref, ob_ref, yb_ref)


def _outffn(xa, ca, oa, xb, cb, ob, wout, gff, w1, w2, *, layer):
    d_model = xa.shape[1]
    d_ff = w1.shape[-1]
    tb = ROW_BLOCK
    assert xa.shape[0] % tb == 0 and xb.shape[0] % tb == 0 and d_ff % FF_CHUNK == 0
    assert d_model % WEIGHT_STAGE_ROWS == 0 and d_ff % WEIGHT_STAGE_ROWS == 0
    na, nb = xa.shape[0] // tb, xb.shape[0] // tb
    first = lambda width: pl.BlockSpec((tb, width), lambda r: (jnp.minimum(r, na - 1), 0))
    second = lambda width: pl.BlockSpec((tb, width), lambda r: (jnp.maximum(r - na, 0), 0))
    any_spec = pl.BlockSpec(memory_space=pl.ANY)
    gff_spec = pl.BlockSpec((None,) + gff.shape[1:], lambda r: (layer, 0, 0))
    return pl.pallas_call(
        functools.partial(_outffn_kernel, n_first=na, layer=layer),
        out_shape=(jax.ShapeDtypeStruct(xa.shape, _F32), jax.ShapeDtypeStruct(xb.shape, _F32)),
        grid=(na + nb,),
        in_specs=[first(d_model), first(CONV_CH), first(ATTN_W), second(d_model), second(CONV_CH), second(ATTN_W),
                  any_spec, gff_spec, any_spec, any_spec],
        out_specs=(first(d_model), second(d_model)),
        scratch_shapes=[pltpu.VMEM(wout.shape[1:], _BF16), pltpu.VMEM(w1.shape[1:], _BF16),
                        pltpu.VMEM(w2.shape[1:], _BF16),
                        pltpu.VMEM((WEIGHT_STAGE_SLOTS, WEIGHT_STAGE_ROWS, d_model), _F32),
                        pltpu.SemaphoreType.DMA((WEIGHT_STAGE_SLOTS,))],
        compiler_params=pltpu.CompilerParams(dimension_semantics=("arbitrary",),
                                             vmem_limit_bytes=VMEM_LIMIT_BYTES),
        name="outproj_ffn",
    )(xa, ca, oa, xb, cb, ob, wout, gff, w1, w2)


def kernel(x_prompt, x_sample, cache_k, cache_v, state_conv, g_mix, w_in, w_dw, b_dw, g_cn, b_cn, w_pw,
           g_q, g_k, w_out, g_ff, w_ff1, w_ff2):
    depth = w_in.shape[0]
    bp, tp, _ = x_prompt.shape
    bs, ts, _ = x_sample.shape
    n_past = cache_k.shape[2]
    head_id = lax.broadcasted_iota(jnp.int32, (ATTN_W, ATTN_W), 0) // HEAD_DIM
    gmat = (head_id == head_id.T).astype(_BF16)
    zero_hist = jnp.zeros((1, bp, HIST_PAD, CONV_CH), _F32)
    sample_hist = jnp.pad(state_conv, ((0, 0), (0, 0), (HIST_PAD - HIST, 0), (0, 0)))
    row = lambda a: a.reshape(depth, 1, -1)
    per_head = lambda g: jnp.tile(g, (1, N_HEADS))
    win = w_in.astype(_BF16)
    w = dict(gmix=row(g_mix), win=win, w_kv_t=jnp.transpose(win[:, :, 2 * CONV_CH + ATTN_W:], (0, 2, 1)),
             wdw=jnp.broadcast_to(w_dw[:, :, None, :], w_dw.shape[:2] + (SUBLANES, CONV_CH)), bdw=row(b_dw), gcn=row(g_cn), bcn=row(b_cn), wpw=w_pw.astype(_BF16),
             gq=row(per_head(g_q)), gk=row(per_head(g_k)),
             gk_col=jnp.broadcast_to(per_head(g_k)[:, :, None], (depth, ATTN_W, LANES)), gmat=gmat,
             wout=w_out, gff=row(g_ff), w1=w_ff1, w2=w_ff2)
    to_fm = lambda a: jnp.transpose(a, (0, 1, 3, 4, 2)).reshape(a.shape[0], a.shape[1], ATTN_W, a.shape[2])
    from_fm = lambda a: jnp.transpose(a.reshape(a.shape[0], a.shape[1], N_HEADS, HEAD_DIM, a.shape[3]),
                                      (0, 1, 4, 2, 3))
    cache_k_fm, cache_v_fm = to_fm(cache_k), to_fm(cache_v)

    hp, hs = x_prompt, x_sample
    kv_stack = (None, None)
    conv_p, k_s, v_s, conv_s = [], [], [], []
    flat = lambda a: a.reshape(-1, a.shape[-1])
    for l in range(depth):
        cp, q, k_fm, v_fm, cv = _inproj(hp, zero_hist, 0, w, layer=l, n_seg=1, seg_rows=min(ROW_BLOCK, tp),
                                        kv_stack=kv_stack)
        kv_stack = (k_fm, v_fm)
        op = _attention(q, None, None, k_fm, v_fm, layer=l, past_is_own=True)
        conv_p.append(cv)
        cs, q, k, v, cv = _inproj(hs, sample_hist, l, w, layer=l, n_seg=bs, seg_rows=ts)
        os_ = _attention(q, k, v, cache_k_fm, cache_v_fm, layer=l, past_is_own=False)
        k_s.append(k.reshape(bs, ts, N_HEADS, HEAD_DIM))
        v_s.append(v.reshape(bs, ts, N_HEADS, HEAD_DIM))
        conv_s.append(cv)
        yp, ys = _outffn(flat(hp), flat(cp), flat(op), flat(hs), flat(cs), flat(os_),
                         w["wout"], w["gff"], w["w1"], w["w2"], layer=l)
        hp, hs = yp.reshape(hp.shape), ys.reshape(hs.shape)
    return (hp, hs, from_fm(kv_stack[0]), from_fm(kv_stack[1]), jnp.stack(conv_p),
            jnp.stack(k_s), jnp.stack(v_s), jnp.stack(conv_s))
```

```python
import functools

import jax
import jax.numpy as jnp
from jax import lax
from jax.experimental import pallas as pl
from jax.experimental.pallas import tpu as pltpu

EPS = 1e-6
CONV_WIDTH = 31
HIST = CONV_WIDTH - 1
HIST_PAD = 32
SUBLANES = 8
N_HEADS = 8
HEAD_DIM = 64
ATTN_W = N_HEADS * HEAD_DIM
CONV_CH = 512
LANES = 128
KEY_CHUNK = 128
CONV_ROWS = 64
FF_CHUNK = 1024
ROW_BLOCK = 512
INPROJ_ROW_BLOCK = 1024
WEIGHT_STAGE_ROWS = 256
WEIGHT_STAGE_SLOTS = 8
VMEM_LIMIT_BYTES = 56 * 1024 * 1024
ATTN_BLOCKS_PER_STEP = 4
STICK_LOG_FLOOR = -87.5
MASKED_SCORE = -1e30

_F32 = jnp.float32
_BF16 = jnp.bfloat16


def _dot(a, b):
    return jnp.dot(a, b, preferred_element_type=_F32)


def _dot_nt(a, b):
    return lax.dot_general(a, b, (((1,), (1,)), ((), ())), preferred_element_type=_F32)


def _dot_split2(x, m):
    hi = x.astype(_BF16)
    lo = (x - hi.astype(_F32)).astype(_BF16)
    return _dot(hi, m) + _dot(lo, m)


def _inproj_kernel(*refs, n_seg, seg_rows, feature_major_kv, first_layer):
    if feature_major_kv:
        (x_ref, hist_ref, gmix_ref, win_ref, wkv_ref, wdw_ref, bdw_ref, gcn_ref, bcn_ref, wpw_ref,
         gq_ref, gk_ref, gmat_ref, _, _,
         c_ref, q_ref, k_ref, v_ref, nconv_ref, uext_ref, cs_ref, act_ref) = refs
        if first_layer:
            for stack_ref in (k_ref, v_ref):
                if stack_ref.shape[0] > 1:
                    stack_ref[1:] = jnp.zeros((stack_ref.shape[0] - 1,) + stack_ref.shape[1:], _F32)
            k_ref, v_ref = k_ref.at[0], v_ref.at[0]
    else:
        (x_ref, hist_ref, gmix_ref, win_ref, wdw_ref, bdw_ref, gcn_ref, bcn_ref, wpw_ref,
         gq_ref, gk_ref, gmat_ref,
         c_ref, q_ref, k_ref, v_ref, nconv_ref, uext_ref, cs_ref, act_ref) = refs
    t = pl.program_id(1)
    n_t = pl.num_programs(1)
    rows = n_seg * seg_rows
    d_model = x_ref.shape[-1]

    x = x_ref[...].reshape(rows, d_model)
    h = x * lax.rsqrt(jnp.mean(x * x, axis=-1, keepdims=True) + EPS) * gmix_ref[...]
    h = h.astype(_BF16)

    a = _dot(h, win_ref[:, 0:CONV_CH])
    gate = _dot(h, win_ref[:, CONV_CH:2 * CONV_CH])
    u = a * jax.nn.sigmoid(gate)

    def head_norm(z, g_ref):
        ssq = _dot_split2(z * z, gmat_ref[...])
        return z * lax.rsqrt(ssq * (1.0 / HEAD_DIM) + EPS) * g_ref[...]

    q = _dot(h, win_ref[:, 2 * CONV_CH:2 * CONV_CH + ATTN_W])
    ssq_q = _dot((q * q).astype(_BF16), gmat_ref[...])
    q = q * lax.rsqrt(ssq_q * (1.0 / HEAD_DIM) + EPS) * gq_ref[...]
    q_ref[...] = (q * (HEAD_DIM ** -0.5)).astype(_BF16).reshape(q_ref.shape)
    if feature_major_kv:
        k_t = _dot_nt(wkv_ref[0:ATTN_W, :], h)
        gain = jnp.concatenate([gk_ref[...]] * (rows // LANES), axis=1)
        for hh in range(N_HEADS):
            rs = slice(hh * HEAD_DIM, (hh + 1) * HEAD_DIM)
            kh = k_t[rs]
            ssq = jnp.sum(kh * kh, axis=0, keepdims=True)
            k_ref[rs, :] = kh * lax.rsqrt(ssq * (1.0 / HEAD_DIM) + EPS) * gain[rs]
        v_ref[...] = _dot_nt(wkv_ref[ATTN_W:, :], h)
    else:
        k = _dot(h, win_ref[:, 2 * CONV_CH + ATTN_W:2 * CONV_CH + 2 * ATTN_W])
        k_ref[...] = head_norm(k, gk_ref).reshape(k_ref.shape)
        v_ref[...] = _dot(h, win_ref[:, 2 * CONV_CH + 2 * ATTN_W:]).reshape(v_ref.shape)

    lane_sl = [slice(g * LANES, (g + 1) * LANES) for g in range(CONV_CH // LANES)]
    u3 = u.reshape(n_seg, seg_rows, CONV_CH)

    @pl.when(t == 0)
    def _():
        for g, ls in enumerate(lane_sl):
            uext_ref[:, g, 0:HIST_PAD, :] = hist_ref[:, :, ls]

    for g, ls in enumerate(lane_sl):
        uext_ref[:, g, HIST_PAD:HIST_PAD + seg_rows, :] = u3[:, :, ls]

    half = CONV_ROWS // 2
    for s in range(n_seg):
        for r in range(seg_rows // CONV_ROWS):
            r0 = r * CONV_ROWS
            acc_even = jnp.zeros((half // SUBLANES, SUBLANES, CONV_CH), _F32)
            acc_odd = jnp.zeros((half // SUBLANES, SUBLANES, CONV_CH), _F32)
            for m in range(CONV_WIDTH + 1):
                start = r0 + m + (HIST_PAD - HIST)
                window = jnp.concatenate(
                    [uext_ref[s, g, pl.ds(start, half, stride=2), :] for g in range(len(lane_sl))], axis=1)
                window = window.reshape(half // SUBLANES, SUBLANES, CONV_CH)
                if m < CONV_WIDTH:
                    acc_even = acc_even + wdw_ref[m] * window
                if m >= 1:
                    acc_odd = acc_odd + wdw_ref[m - 1] * window
            acc_even = acc_even.reshape(half, CONV_CH)
            acc_odd = acc_odd.reshape(half, CONV_CH)
            for par, acc in enumerate((acc_even, acc_odd)):
                cv = acc + bdw_ref[...]
                mu = jnp.mean(cv, axis=-1, keepdims=True)
                xc = cv - mu
                y = xc * lax.rsqrt(jnp.mean(xc * xc, axis=-1, keepdims=True) + EPS)
                y = y * gcn_ref[...] + bcn_ref[...]
                y = y * jax.nn.sigmoid(y)
                base = s * seg_rows + r0 + par * half
                act_ref[base:base + half, :] = y.astype(_BF16)

    c_perm = _dot(act_ref[...], wpw_ref[...])
    for s in range(n_seg):
        for r in range(seg_rows // CONV_ROWS):
            r0 = r * CONV_ROWS
            base = s * seg_rows + r0
            for par in range(2):
                rows_par = c_perm[base + par * half:base + (par + 1) * half]
                for g, ls in enumerate(lane_sl):
                    cs_ref[s, g, pl.ds(r0 + par, half, stride=2), :] = rows_par[:, ls]
    for g, ls in enumerate(lane_sl):
        c_ref[:, :, ls] = cs_ref[:, g, :, :].astype(_BF16)

    @pl.when(t == n_t - 1)
    def _():
        for g, ls in enumerate(lane_sl):
            nconv_ref[:, :, ls] = uext_ref[:, g, seg_rows + (HIST_PAD - HIST):seg_rows + HIST_PAD, :]

    uext_ref[:, :, 0:HIST_PAD, :] = uext_ref[:, :, seg_rows:seg_rows + HIST_PAD, :]


def _inproj(x, hist, hist_layer, w, *, layer, n_seg, seg_rows, kv_stack=None):
    batch, t_len, d_model = x.shape
    depth = w["win"].shape[0]
    assert batch % n_seg == 0 and t_len % seg_rows == 0 and seg_rows % CONV_ROWS == 0
    n_t = t_len // seg_rows
    assert n_seg == 1 or n_t == 1
    feature_major_kv = kv_stack is not None
    rows_spec = lambda width: pl.BlockSpec((n_seg, seg_rows, width), lambda b, t: (b, t, 0))
    shared = lambda arr: pl.BlockSpec(arr.shape, lambda b, t: (0,) * arr.ndim)
    const = lambda arr: pl.BlockSpec((None,) + arr.shape[1:], lambda b, t: (layer,) + (0,) * (arr.ndim - 1),
                                     pipeline_mode=pl.Buffered(1))
    hist_spec = pl.BlockSpec((None, n_seg, HIST_PAD, CONV_CH), lambda b, t: (hist_layer, b, 0, 0))
    nconv_spec = pl.BlockSpec((n_seg, HIST, CONV_CH), lambda b, t: (b, 0, 0))
    small = [w["wdw"], w["bdw"], w["gcn"], w["bcn"], w["wpw"]]
    aliases = {}
    if feature_major_kv:
        assert n_seg == 1 and seg_rows % LANES == 0
        args = [x, hist, w["gmix"], w["win"], w["w_kv_t"]] + small + [w["gq"], w["gk_col"], w["gmat"]]
        in_specs = [rows_spec(d_model), hist_spec] + [const(a) for a in args[2:-1]] + [shared(w["gmat"])]
        kv_shape = jax.ShapeDtypeStruct((depth, batch, ATTN_W, t_len), _F32)
        first_layer = kv_stack[0] is None
        if first_layer:
            kv_spec = pl.BlockSpec((depth, None, ATTN_W, seg_rows), lambda b, t: (0, b, 0, t))
            dummy = jnp.zeros((SUBLANES, LANES), _F32)
            args += [dummy, dummy]
            in_specs += [shared(dummy)] * 2
        else:
            kv_spec = pl.BlockSpec((None, None, ATTN_W, seg_rows), lambda b, t: (layer, b, 0, t))
            aliases = {len(args): 2, len(args) + 1: 3}
            args += list(kv_stack)
            in_specs += [pl.BlockSpec(memory_space=pl.ANY)] * 2
    else:
        args = [x, hist, w["gmix"], w["win"]] + small + [w["gq"], w["gk"], w["gmat"]]
        in_specs = [rows_spec(d_model), hist_spec] + [const(a) for a in args[2:-1]] + [shared(w["gmat"])]
        kv_shape = jax.ShapeDtypeStruct((batch, t_len, ATTN_W), _F32)
        kv_spec = rows_spec(ATTN_W)
        first_layer = False
    out_shape = (
        jax.ShapeDtypeStruct((batch, t_len, CONV_CH), _BF16),
        jax.ShapeDtypeStruct((batch, t_len, ATTN_W), _BF16),
        kv_shape, kv_shape,
        jax.ShapeDtypeStruct((batch, HIST, CONV_CH), _F32),
    )
    return pl.pallas_call(
        functools.partial(_inproj_kernel, n_seg=n_seg, seg_rows=seg_rows, feature_major_kv=feature_major_kv,
                          first_layer=first_layer),
        out_shape=out_shape,
        grid=(batch // n_seg, n_t),
        in_specs=in_specs,
        out_specs=(rows_spec(CONV_CH), rows_spec(ATTN_W), kv_spec, kv_spec, nconv_spec),
        scratch_shapes=[pltpu.VMEM((n_seg, CONV_CH // LANES, seg_rows + HIST_PAD, LANES), _F32),
                        pltpu.VMEM((n_seg, CONV_CH // LANES, seg_rows, LANES), _F32),
                        pltpu.VMEM((n_seg * seg_rows, CONV_CH), _BF16)],
        input_output_aliases=aliases,
        compiler_params=pltpu.CompilerParams(dimension_semantics=("arbitrary", "arbitrary"),
                                             vmem_limit_bytes=VMEM_LIMIT_BYTES),
        name="inproj_conv",
    )(*args)


def _attn_kernel(*refs, n_sub, past_chunks0, chunks_per_step, layer, own_feature_major):
    n_fm = n_sub + 2 if own_feature_major else 2
    q_ref, k_fm, v_fm, rest = refs[0], refs[1:1 + n_fm], refs[1 + n_fm:1 + 2 * n_fm], refs[1 + 2 * n_fm:]
    if not own_feature_major:
        kd_ref, vd_ref, rest = rest[0], rest[1], rest[2:]
    kany_ref, vany_ref, ud_ref, up_ref, o_ref, oacc_ref, run_ref, kbuf_ref, vbuf_ref, sem_ref = rest
    b = pl.program_id(0)
    i = pl.program_id(1)
    tq = q_ref.shape[0] // n_sub
    n_pairs = ATTN_W // LANES
    subs = range(n_sub)
    n_past = [past_chunks0 + i * chunks_per_step + j for j in subs]
    even_head = lax.broadcasted_iota(jnp.int32, (tq, LANES), 1) < HEAD_DIM
    pair_sl = [slice(p * LANES, (p + 1) * LANES) for p in range(n_pairs)]

    def stacked_queries(j):
        qs = []
        for sl in pair_sl:
            qp = q_ref[j * tq:(j + 1) * tq, sl]
            qs.append(jnp.concatenate([jnp.where(even_head, qp, 0.0), jnp.where(even_head, 0.0, qp)], axis=0))
        return qs

    def scores_fm(qs, key_refs):
        out = []
        for p, sl in enumerate(pair_sl):
            keys = jnp.concatenate([r[sl, :] for r in key_refs], axis=1).astype(_BF16)
            out.append(_dot(qs[p], keys))
        return jnp.concatenate(out, axis=0)

    def log_terms(z):
        sp = jnp.log(1.0 + jnp.exp(-jnp.abs(z)))
        neg_part = jnp.minimum(z, 0.0)
        return neg_part - sp, (neg_part - z) - sp

    def suffix_sums(log_1m, u_ref):
        hi = log_1m.astype(_BF16)
        lo = (log_1m - hi.astype(_F32)).astype(_BF16)
        r = _dot(jnp.concatenate([hi, lo], axis=1), u_ref[...])
        return r[:, LANES:], r[:, :LANES]

    def weighted_values(fm_weights, fm_value_refs, rm_weight=None, rm_value_ref=None):
        wb = jnp.concatenate(fm_weights, axis=1).astype(_BF16)
        outs = []
        for p, sl in enumerate(pair_sl):
            rows = slice(2 * tq * p, 2 * tq * (p + 1))
            vals = jnp.concatenate([r[sl, :] for r in fm_value_refs], axis=1).astype(_BF16)
            pv = _dot_nt(wb[rows], vals)
            if rm_weight is not None:
                pv = pv + _dot(rm_weight[rows].astype(_BF16), rm_value_ref[:, sl].astype(_BF16))
            outs.append(jnp.where(even_head, pv[:tq], pv[tq:]))
        return jnp.concatenate(outs, axis=1)

    qs = [stacked_queries(j) for j in subs]
    if own_feature_major:
        z = [scores_fm(qs[j], k_fm[j:j + 3]) for j in subs]
        z_own = [zj[:, 2 * KEY_CHUNK:] for zj in z]
    else:
        z = [scores_fm(qs[0], k_fm)]
        z_own = [jnp.concatenate([_dot_nt(qs[0][p], kd_ref[:, sl].astype(_BF16))
                                  for p, sl in enumerate(pair_sl)], axis=0)]
    q_row = lax.broadcasted_iota(jnp.int32, (N_HEADS * tq, tq), 0) & (tq - 1)
    causal = lax.broadcasted_iota(jnp.int32, (N_HEADS * tq, tq), 1) < q_row
    def past_scores(j, back):
        zc = z[j][:, (2 - back) * KEY_CHUNK:(3 - back) * KEY_CHUNK]
        if past_chunks0 + j >= back:
            return zc
        return zc + jnp.where(n_past[j] >= back, 0.0, MASKED_SCORE)

    z_2 = [past_scores(j, 2) for j in subs]
    z_1 = [past_scores(j, 1) for j in subs]
    z_d = [jnp.where(causal, z_own[j], MASKED_SCORE) for j in subs]
    t_d = [log_terms(zz) for zz in z_d]
    t_1 = [log_terms(zz) for zz in z_1]
    t_2 = [log_terms(zz) for zz in z_2]
    s_d = [suffix_sums(t_d[j][1], ud_ref) for j in subs]
    s_1 = [suffix_sums(t_1[j][1], up_ref) for j in subs]
    s_2 = [suffix_sums(t_2[j][1], up_ref) for j in subs]
    a_d = [jnp.exp(t_d[j][0] + s_d[j][0]) for j in subs]
    a_1 = [jnp.exp(t_1[j][0] + (s_1[j][0] + s_d[j][1])) for j in subs]
    run_1 = [s_d[j][1] + s_1[j][1] for j in subs]
    a_2 = [jnp.exp(t_2[j][0] + (s_2[j][0] + run_1[j])) for j in subs]
    run = [run_1[j] + s_2[j][1] for j in subs]
    for j in subs:
        if own_feature_major:
            out = weighted_values((a_2[j], a_1[j], a_d[j]), v_fm[j:j + 3])
        else:
            out = weighted_values((a_2[j], a_1[j]), v_fm, a_d[j], vd_ref)
        oacc_ref[j * tq:(j + 1) * tq, :] = out
        run_ref[j] = run[j]

    def more(carry):
        c, m = carry
        return jnp.logical_and(c >= 0, m > STICK_LOG_FLOOR)

    def older(j, carry):
        c, _ = carry
        start = pl.multiple_of(c * KEY_CHUNK, KEY_CHUNK)
        k_copy = pltpu.make_async_copy(kany_ref.at[layer, b, :, pl.ds(start, KEY_CHUNK)], kbuf_ref, sem_ref.at[0])
        v_copy = pltpu.make_async_copy(vany_ref.at[layer, b, :, pl.ds(start, KEY_CHUNK)], vbuf_ref, sem_ref.at[1])
        k_copy.start()
        v_copy.start()
        k_copy.wait()
        v_copy.wait()
        lb, l1 = log_terms(scores_fm(stacked_queries(j), (kbuf_ref,)))
        inside, tot = suffix_sums(l1, up_ref)
        run_c = run_ref[j]
        oacc_ref[j * tq:(j + 1) * tq, :] += weighted_values((jnp.exp(lb + (inside + run_c)),), (vbuf_ref,))
        run_c = run_c + tot
        run_ref[j] = run_c
        return c - 1, jnp.max(run_c)

    for j in subs:
        lax.while_loop(more, functools.partial(older, j), (n_past[j] - 3, jnp.max(run[j])))

    o_ref[...] = oacc_ref[...].astype(o_ref.dtype)


def _attention(q, k_own, v_own, k_past, v_past, *, layer, past_is_own):
    batch, t_len, _ = q.shape
    if past_is_own:
        tq = KEY_CHUNK
        n_sub = ATTN_BLOCKS_PER_STEP if t_len % (ATTN_BLOCKS_PER_STEP * tq) == 0 else 1
        past_chunks0, chunks_per_step, n_fm = 0, n_sub, n_sub + 2
    else:
        tq, n_sub = t_len, 1
        assert k_past.shape[-1] % KEY_CHUNK == 0
        past_chunks0, chunks_per_step, n_fm = k_past.shape[-1] // KEY_CHUNK, 0, 2
    assert t_len % (n_sub * tq) == 0 and tq <= KEY_CHUNK and tq & (tq - 1) == 0
    n_steps = t_len // (n_sub * tq)
    assert past_is_own or n_steps == 1

    def newer(n):
        u = lax.broadcasted_iota(jnp.int32, (n, n), 0) > lax.broadcasted_iota(jnp.int32, (n, n), 1)
        once = jnp.concatenate([jnp.ones((n, LANES), _BF16), u.astype(_BF16)], axis=1)
        return jnp.concatenate([once, once], axis=0)
    u_diag, u_past = newer(tq), newer(KEY_CHUNK)

    def fm_chunk(m):
        imap = lambda b, i: (layer, b, 0, jnp.maximum(past_chunks0 + i * chunks_per_step + m - 2, 0))
        return pl.BlockSpec((None, None, ATTN_W, KEY_CHUNK), imap)

    rm_block = pl.BlockSpec((None, n_sub * tq, ATTN_W), lambda b, i: (b, i, 0))
    const = lambda arr: pl.BlockSpec(arr.shape, lambda b, i: (0,) * arr.ndim)
    any_spec = pl.BlockSpec(memory_space=pl.ANY)
    fm_specs = [fm_chunk(m) for m in range(n_fm)]
    args = [q] + [k_past] * n_fm + [v_past] * n_fm
    in_specs = [rm_block] + fm_specs + fm_specs
    if not past_is_own:
        args += [k_own, v_own]
        in_specs += [rm_block, rm_block]
    args += [k_past, v_past, u_diag, u_past]
    in_specs += [any_spec, any_spec, const(u_diag), const(u_past)]
    return pl.pallas_call(
        functools.partial(_attn_kernel, n_sub=n_sub, past_chunks0=past_chunks0, chunks_per_step=chunks_per_step,
                          layer=layer, own_feature_major=past_is_own),
        out_shape=jax.ShapeDtypeStruct((batch, t_len, ATTN_W), _BF16),
        grid=(batch, n_steps),
        in_specs=in_specs,
        out_specs=rm_block,
        scratch_shapes=[pltpu.VMEM((n_sub * tq, ATTN_W), _F32),
                        pltpu.VMEM((n_sub, N_HEADS * tq, LANES), _F32),
                        pltpu.VMEM((ATTN_W, KEY_CHUNK), _F32),
                        pltpu.VMEM((ATTN_W, KEY_CHUNK), _F32),
                        pltpu.SemaphoreType.DMA((2,))],
        compiler_params=pltpu.CompilerParams(dimension_semantics=("arbitrary", "arbitrary"),
                                             vmem_limit_bytes=VMEM_LIMIT_BYTES),
        name="stick_attention",
    )(*args)


def _outffn_kernel(xa_ref, ca_ref, oa_ref, xb_ref, cb_ref, ob_ref, wout_any, gff_ref, w1_any, w2_any,
                   ya_ref, yb_ref, wout_ref, w1_ref, w2_ref, stage_ref, sem_ref, *, n_first, layer):
    d_ff = w1_ref.shape[1]
    r = pl.program_id(0)

    @pl.when(r == 0)
    def _():
        slots, rows, cols = stage_ref.shape
        tiles = [(src, dst, i, j)
                 for src, dst in ((wout_any, wout_ref), (w1_any, w1_ref), (w2_any, w2_ref))
                 for i in range(dst.shape[0] // rows) for j in range(dst.shape[1] // cols)]

        def tile_copy(n):
            src, _, i, j = tiles[n]
            return pltpu.make_async_copy(src.at[layer, pl.ds(i * rows, rows), pl.ds(j * cols, cols)],
                                         stage_ref.at[n % slots], sem_ref.at[n % slots])
        for n in range(min(slots - 1, len(tiles))):
            tile_copy(n).start()
        for n, (_, dst, i, j) in enumerate(tiles):
            if n + slots - 1 < len(tiles):
                tile_copy(n + slots - 1).start()
            tile_copy(n).wait()
            dst[i * rows:(i + 1) * rows, j * cols:(j + 1) * cols] = stage_ref[n % slots].astype(_BF16)

    def mix_and_ffn(x_ref, c_ref, o_ref, y_ref):
        x1 = x_ref[...] + _dot(c_ref[...], wout_ref[0:CONV_CH, :]) + _dot(o_ref[...], wout_ref[CONV_CH:, :])
        h = x1 * lax.rsqrt(jnp.mean(x1 * x1, axis=-1, keepdims=True) + EPS) * gff_ref[...]
        h = h.astype(_BF16)
        ffn = None
        for cc in range(d_ff // FF_CHUNK):
            f = jnp.maximum(_dot(h, w1_ref[:, cc * FF_CHUNK:(cc + 1) * FF_CHUNK]), 0.0)
            part = _dot((f * f).astype(_BF16), w2_ref[cc * FF_CHUNK:(cc + 1) * FF_CHUNK, :])
            ffn = part if ffn is None else ffn + part
        y_ref[...] = x1 + ffn

    @pl.when(r < n_first)
    def _():
        mix_and_ffn(xa_ref, ca_ref, oa_ref, ya_ref)

    @pl.when(r >= n_first)
    def _():
        mix_and_ffn(xb_ref, cb_ref, ob_ref, yb_ref)


def _outffn(xa, ca, oa, xb, cb, ob, wout, gff, w1, w2, *, layer):
    d_model = xa.shape[1]
    d_ff = w1.shape[-1]
    tb = ROW_BLOCK
    assert xa.shape[0] % tb == 0 and xb.shape[0] % tb == 0 and d_ff % FF_CHUNK == 0
    assert d_model % WEIGHT_STAGE_ROWS == 0 and d_ff % WEIGHT_STAGE_ROWS == 0
    na, nb = xa.shape[0] // tb, xb.shape[0] // tb
    first = lambda width: pl.BlockSpec((tb, width), lambda r: (jnp.minimum(r, na - 1), 0))
    second = lambda width: pl.BlockSpec((tb, width), lambda r: (jnp.maximum(r - na, 0), 0))
    any_spec = pl.BlockSpec(memory_space=pl.ANY)
    gff_spec = pl.BlockSpec((None,) + gff.shape[1:], lambda r: (layer, 0, 0))
    return pl.pallas_call(
        functools.partial(_outffn_kernel, n_first=na, layer=layer),
        out_shape=(jax.ShapeDtypeStruct(xa.shape, _F32), jax.ShapeDtypeStruct(xb.shape, _F32)),
        grid=(na + nb,),
        in_specs=[first(d_model), first(CONV_CH), first(ATTN_W), second(d_model), second(CONV_CH), second(ATTN_W),
                  any_spec, gff_spec, any_spec, any_spec],
        out_specs=(first(d_model), second(d_model)),
        scratch_shapes=[pltpu.VMEM(wout.shape[1:], _BF16), pltpu.VMEM(w1.shape[1:], _BF16),
                        pltpu.VMEM(w2.shape[1:], _BF16),
                        pltpu.VMEM((WEIGHT_STAGE_SLOTS, WEIGHT_STAGE_ROWS, d_model), _F32),
                        pltpu.SemaphoreType.DMA((WEIGHT_STAGE_SLOTS,))],
        compiler_params=pltpu.CompilerParams(dimension_semantics=("arbitrary",),
                                             vmem_limit_bytes=VMEM_LIMIT_BYTES),
        name="outproj_ffn",
    )(xa, ca, oa, xb, cb, ob, wout, gff, w1, w2)


def kernel(x_prompt, x_sample, cache_k, cache_v, state_conv, g_mix, w_in, w_dw, b_dw, g_cn, b_cn, w_pw,
           g_q, g_k, w_out, g_ff, w_ff1, w_ff2):
    depth = w_in.shape[0]
    bp, tp, _ = x_prompt.shape
    bs, ts, _ = x_sample.shape
    n_past = cache_k.shape[2]
    head_id = lax.broadcasted_iota(jnp.int32, (ATTN_W, ATTN_W), 0) // HEAD_DIM
    gmat = (head_id == head_id.T).astype(_BF16)
    zero_hist = jnp.zeros((1, bp, HIST_PAD, CONV_CH), _F32)
    sample_hist = jnp.pad(state_conv, ((0, 0), (0, 0), (HIST_PAD - HIST, 0), (0, 0)))
    row = lambda a: a.reshape(depth, 1, -1)
    per_head = lambda g: jnp.tile(g, (1, N_HEADS))
    win = w_in.astype(_BF16)
    w = dict(gmix=row(g_mix), win=win, w_kv_t=jnp.transpose(win[:, :, 2 * CONV_CH + ATTN_W:], (0, 2, 1)),
             wdw=jnp.broadcast_to(w_dw[:, :, None, :], w_dw.shape[:2] + (SUBLANES, CONV_CH)), bdw=row(b_dw), gcn=row(g_cn), bcn=row(b_cn), wpw=w_pw.astype(_BF16),
             gq=row(per_head(g_q)), gk=row(per_head(g_k)),
             gk_col=jnp.broadcast_to(per_head(g_k)[:, :, None], (depth, ATTN_W, LANES)), gmat=gmat,
             wout=w_out, gff=row(g_ff), w1=w_ff1, w2=w_ff2)
    to_fm = lambda a: jnp.transpose(a, (0, 1, 3, 4, 2)).reshape(a.shape[0], a.shape[1], ATTN_W, a.shape[2])
    from_fm = lambda a: jnp.transpose(a.reshape(a.shape[0], a.shape[1], N_HEADS, HEAD_DIM, a.shape[3]),
                                      (0, 1, 4, 2, 3))
    cache_k_fm, cache_v_fm = to_fm(cache_k), to_fm(cache_v)

    hp, hs = x_prompt, x_sample
    kv_stack = (None, None)
    conv_p, k_s, v_s, conv_s = [], [], [], []
    flat = lambda a: a.reshape(-1, a.shape[-1])
    for l in range(depth):
        cp, q, k_fm, v_fm, cv = _inproj(hp, zero_hist, 0, w, layer=l, n_seg=1, seg_rows=min(INPROJ_ROW_BLOCK, tp),
                                        kv_stack=kv_stack)
        kv_stack = (k_fm, v_fm)
        op = _attention(q, None, None, k_fm, v_fm, layer=l, past_is_own=True)
        conv_p.append(cv)
        cs, q, k, v, cv = _inproj(hs, sample_hist, l, w, layer=l, n_seg=bs, seg_rows=ts)
        os_ = _attention(q, k, v, cache_k_fm, cache_v_fm, layer=l, past_is_own=False)
        k_s.append(k.reshape(bs, ts, N_HEADS, HEAD_DIM))
        v_s.append(v.reshape(bs, ts, N_HEADS, HEAD_DIM))
        conv_s.append(cv)
        yp, ys = _outffn(flat(hp), flat(cp), flat(op), flat(hs), flat(cs), flat(os_),
                         w["wout"], w["gff"], w["w1"], w["w2"], layer=l)
        hp, hs = yp.reshape(hp.shape), ys.reshape(hs.shape)
    return (hp, hs, from_fm(kv_stack[0]), from_fm(kv_stack[1]), jnp.stack(conv_p),
            jnp.stack(k_s), jnp.stack(v_s), jnp.stack(conv_s))
```

```python
import functools

import jax
import jax.numpy as jnp
from jax import lax
from jax.experimental import pallas as pl
from jax.experimental.pallas import tpu as pltpu

EPS = 1e-6
CONV_WIDTH = 31
HIST = CONV_WIDTH - 1
HIST_PAD = 32
SUBLANES = 8
N_HEADS = 8
HEAD_DIM = 64
ATTN_W = N_HEADS * HEAD_DIM
CONV_CH = 512
LANES = 128
KEY_CHUNK = 128
CONV_ROWS = 64
FF_CHUNK = 1024
ROW_BLOCK = 512
INPROJ_ROW_BLOCK = 1024
WEIGHT_STAGE_ROWS = 256
WEIGHT_STAGE_SLOTS = 8
VMEM_LIMIT_BYTES = 56 * 1024 * 1024
ATTN_BLOCKS_PER_STEP = 4
STICK_LOG_FLOOR = -87.5
MASKED_SCORE = -1e30

_F32 = jnp.float32
_BF16 = jnp.bfloat16


def _dot(a, b):
    return jnp.dot(a, b, preferred_element_type=_F32)


def _dot_nt(a, b):
    return lax.dot_general(a, b, (((1,), (1,)), ((), ())), preferred_element_type=_F32)


def _dot_tn(a, b):
    return lax.dot_general(a, b, (((0,), (1,)), ((), ())), preferred_element_type=_F32)


def _dot_split2(x, m):
    hi = x.astype(_BF16)
    lo = (x - hi.astype(_F32)).astype(_BF16)
    return _dot(hi, m) + _dot(lo, m)


def _inproj_kernel(*refs, n_seg, seg_rows, feature_major_kv, first_layer):
    if feature_major_kv:
        (x_ref, hist_ref, gmix_ref, win_ref, wdw_ref, bdw_ref, gcn_ref, bcn_ref, wpw_ref,
         gq_ref, gk_ref, gmat_ref, _, _,
         c_ref, q_ref, k_ref, v_ref, nconv_ref, uext_ref, cs_ref, act_ref) = refs
        if first_layer:
            for stack_ref in (k_ref, v_ref):
                if stack_ref.shape[0] > 1:
                    stack_ref[1:] = jnp.zeros((stack_ref.shape[0] - 1,) + stack_ref.shape[1:], _F32)
            k_ref, v_ref = k_ref.at[0], v_ref.at[0]
    else:
        (x_ref, hist_ref, gmix_ref, win_ref, wdw_ref, bdw_ref, gcn_ref, bcn_ref, wpw_ref,
         gq_ref, gk_ref, gmat_ref,
         c_ref, q_ref, k_ref, v_ref, nconv_ref, uext_ref, cs_ref, act_ref) = refs
    t = pl.program_id(1)
    n_t = pl.num_programs(1)
    rows = n_seg * seg_rows
    d_model = x_ref.shape[-1]

    x = x_ref[...].reshape(rows, d_model)
    h = x * lax.rsqrt(jnp.mean(x * x, axis=-1, keepdims=True) + EPS) * gmix_ref[...]
    h = h.astype(_BF16)

    a = _dot(h, win_ref[:, 0:CONV_CH])
    gate = _dot(h, win_ref[:, CONV_CH:2 * CONV_CH])
    u = a * jax.nn.sigmoid(gate)

    def head_norm(z, g_ref):
        ssq = _dot_split2(z * z, gmat_ref[...])
        return z * lax.rsqrt(ssq * (1.0 / HEAD_DIM) + EPS) * g_ref[...]

    q = _dot(h, win_ref[:, 2 * CONV_CH:2 * CONV_CH + ATTN_W])
    ssq_q = _dot((q * q).astype(_BF16), gmat_ref[...])
    q = q * lax.rsqrt(ssq_q * (1.0 / HEAD_DIM) + EPS) * gq_ref[...]
    q_ref[...] = (q * (HEAD_DIM ** -0.5)).astype(_BF16).reshape(q_ref.shape)
    if feature_major_kv:
        k_t = _dot_tn(win_ref[:, 2 * CONV_CH + ATTN_W:2 * CONV_CH + 2 * ATTN_W], h)
        gain = jnp.concatenate([gk_ref[...]] * (rows // LANES), axis=1)
        for hh in range(N_HEADS):
            rs = slice(hh * HEAD_DIM, (hh + 1) * HEAD_DIM)
            kh = k_t[rs]
            ssq = jnp.sum(kh * kh, axis=0, keepdims=True)
            k_ref[rs, :] = kh * lax.rsqrt(ssq * (1.0 / HEAD_DIM) + EPS) * gain[rs]
        v_ref[...] = _dot_tn(win_ref[:, 2 * CONV_CH + 2 * ATTN_W:], h)
    else:
        k = _dot(h, win_ref[:, 2 * CONV_CH + ATTN_W:2 * CONV_CH + 2 * ATTN_W])
        k_ref[...] = head_norm(k, gk_ref).reshape(k_ref.shape)
        v_ref[...] = _dot(h, win_ref[:, 2 * CONV_CH + 2 * ATTN_W:]).reshape(v_ref.shape)

    lane_sl = [slice(g * LANES, (g + 1) * LANES) for g in range(CONV_CH // LANES)]
    u3 = u.reshape(n_seg, seg_rows, CONV_CH)

    @pl.when(t == 0)
    def _():
        for g, ls in enumerate(lane_sl):
            uext_ref[:, g, 0:HIST_PAD, :] = hist_ref[:, :, ls]

    for g, ls in enumerate(lane_sl):
        uext_ref[:, g, HIST_PAD:HIST_PAD + seg_rows, :] = u3[:, :, ls]

    half = CONV_ROWS // 2
    for s in range(n_seg):
        for r in range(seg_rows // CONV_ROWS):
            r0 = r * CONV_ROWS
            acc_even = jnp.zeros((half // SUBLANES, SUBLANES, CONV_CH), _F32)
            acc_odd = jnp.zeros((half // SUBLANES, SUBLANES, CONV_CH), _F32)
            for m in range(CONV_WIDTH + 1):
                start = r0 + m + (HIST_PAD - HIST)
                window = jnp.concatenate(
                    [uext_ref[s, g, pl.ds(start, half, stride=2), :] for g in range(len(lane_sl))], axis=1)
                window = window.reshape(half // SUBLANES, SUBLANES, CONV_CH)
                if m < CONV_WIDTH:
                    acc_even = acc_even + wdw_ref[m] * window
                if m >= 1:
                    acc_odd = acc_odd + wdw_ref[m - 1] * window
            acc_even = acc_even.reshape(half, CONV_CH)
            acc_odd = acc_odd.reshape(half, CONV_CH)
            for par, acc in enumerate((acc_even, acc_odd)):
                cv = acc + bdw_ref[...]
                mu = jnp.mean(cv, axis=-1, keepdims=True)
                xc = cv - mu
                y = xc * lax.rsqrt(jnp.mean(xc * xc, axis=-1, keepdims=True) + EPS)
                y = y * gcn_ref[...] + bcn_ref[...]
                y = y * jax.nn.sigmoid(y)
                base = s * seg_rows + r0 + par * half
                act_ref[base:base + half, :] = y.astype(_BF16)

    c_perm = _dot(act_ref[...], wpw_ref[...])
    for s in range(n_seg):
        for r in range(seg_rows // CONV_ROWS):
            r0 = r * CONV_ROWS
            base = s * seg_rows + r0
            for par in range(2):
                rows_par = c_perm[base + par * half:base + (par + 1) * half]
                for g, ls in enumerate(lane_sl):
                    cs_ref[s, g, pl.ds(r0 + par, half, stride=2), :] = rows_par[:, ls]
    for g, ls in enumerate(lane_sl):
        c_ref[:, :, ls] = cs_ref[:, g, :, :].astype(_BF16)

    @pl.when(t == n_t - 1)
    def _():
        for g, ls in enumerate(lane_sl):
            nconv_ref[:, :, ls] = uext_ref[:, g, seg_rows + (HIST_PAD - HIST):seg_rows + HIST_PAD, :]

    uext_ref[:, :, 0:HIST_PAD, :] = uext_ref[:, :, seg_rows:seg_rows + HIST_PAD, :]


def _inproj(x, hist, hist_layer, w, *, layer, n_seg, seg_rows, kv_stack=None):
    batch, t_len, d_model = x.shape
    depth = w["win"].shape[0]
    assert batch % n_seg == 0 and t_len % seg_rows == 0 and seg_rows % CONV_ROWS == 0
    n_t = t_len // seg_rows
    assert n_seg == 1 or n_t == 1
    feature_major_kv = kv_stack is not None
    rows_spec = lambda width: pl.BlockSpec((n_seg, seg_rows, width), lambda b, t: (b, t, 0))
    shared = lambda arr: pl.BlockSpec(arr.shape, lambda b, t: (0,) * arr.ndim)
    const = lambda arr: pl.BlockSpec((None,) + arr.shape[1:], lambda b, t: (layer,) + (0,) * (arr.ndim - 1),
                                     pipeline_mode=pl.Buffered(1))
    hist_spec = pl.BlockSpec((None, n_seg, HIST_PAD, CONV_CH), lambda b, t: (hist_layer, b, 0, 0))
    nconv_spec = pl.BlockSpec((n_seg, HIST, CONV_CH), lambda b, t: (b, 0, 0))
    small = [w["wdw"], w["bdw"], w["gcn"], w["bcn"], w["wpw"]]
    aliases = {}
    if feature_major_kv:
        assert n_seg == 1 and seg_rows % LANES == 0
        args = [x, hist, w["gmix"], w["win"]] + small + [w["gq"], w["gk_col"], w["gmat"]]
        in_specs = [rows_spec(d_model), hist_spec] + [const(a) for a in args[2:-1]] + [shared(w["gmat"])]
        kv_shape = jax.ShapeDtypeStruct((depth, batch, ATTN_W, t_len), _F32)
        first_layer = kv_stack[0] is None
        if first_layer:
            kv_spec = pl.BlockSpec((depth, None, ATTN_W, seg_rows), lambda b, t: (0, b, 0, t))
            dummy = jnp.zeros((SUBLANES, LANES), _F32)
            args += [dummy, dummy]
            in_specs += [shared(dummy)] * 2
        else:
            kv_spec = pl.BlockSpec((None, None, ATTN_W, seg_rows), lambda b, t: (layer, b, 0, t))
            aliases = {len(args): 2, len(args) + 1: 3}
            args += list(kv_stack)
            in_specs += [pl.BlockSpec(memory_space=pl.ANY)] * 2
    else:
        args = [x, hist, w["gmix"], w["win"]] + small + [w["gq"], w["gk"], w["gmat"]]
        in_specs = [rows_spec(d_model), hist_spec] + [const(a) for a in args[2:-1]] + [shared(w["gmat"])]
        kv_shape = jax.ShapeDtypeStruct((batch, t_len, ATTN_W), _F32)
        kv_spec = rows_spec(ATTN_W)
        first_layer = False
    out_shape = (
        jax.ShapeDtypeStruct((batch, t_len, CONV_CH), _BF16),
        jax.ShapeDtypeStruct((batch, t_len, ATTN_W), _BF16),
        kv_shape, kv_shape,
        jax.ShapeDtypeStruct((batch, HIST, CONV_CH), _F32),
    )
    return pl.pallas_call(
        functools.partial(_inproj_kernel, n_seg=n_seg, seg_rows=seg_rows, feature_major_kv=feature_major_kv,
                          first_layer=first_layer),
        out_shape=out_shape,
        grid=(batch // n_seg, n_t),
        in_specs=in_specs,
        out_specs=(rows_spec(CONV_CH), rows_spec(ATTN_W), kv_spec, kv_spec, nconv_spec),
        scratch_shapes=[pltpu.VMEM((n_seg, CONV_CH // LANES, seg_rows + HIST_PAD, LANES), _F32),
                        pltpu.VMEM((n_seg, CONV_CH // LANES, seg_rows, LANES), _F32),
                        pltpu.VMEM((n_seg * seg_rows, CONV_CH), _BF16)],
        input_output_aliases=aliases,
        compiler_params=pltpu.CompilerParams(dimension_semantics=("arbitrary", "arbitrary"),
                                             vmem_limit_bytes=VMEM_LIMIT_BYTES),
        name="inproj_conv",
    )(*args)


def _attn_kernel(*refs, n_sub, past_chunks0, chunks_per_step, layer, own_feature_major):
    n_fm = n_sub + 2 if own_feature_major else 2
    q_ref, k_fm, v_fm, rest = refs[0], refs[1:1 + n_fm], refs[1 + n_fm:1 + 2 * n_fm], refs[1 + 2 * n_fm:]
    if not own_feature_major:
        kd_ref, vd_ref, rest = rest[0], rest[1], rest[2:]
    kany_ref, vany_ref, ud_ref, up_ref, o_ref, oacc_ref, run_ref, kbuf_ref, vbuf_ref, sem_ref = rest
    b = pl.program_id(0)
    i = pl.program_id(1)
    tq = q_ref.shape[0] // n_sub
    n_pairs = ATTN_W // LANES
    subs = range(n_sub)
    n_past = [past_chunks0 + i * chunks_per_step + j for j in subs]
    even_head = lax.broadcasted_iota(jnp.int32, (tq, LANES), 1) < HEAD_DIM
    pair_sl = [slice(p * LANES, (p + 1) * LANES) for p in range(n_pairs)]

    def stacked_queries(j):
        qs = []
        for sl in pair_sl:
            qp = q_ref[j * tq:(j + 1) * tq, sl]
            qs.append(jnp.concatenate([jnp.where(even_head, qp, 0.0), jnp.where(even_head, 0.0, qp)], axis=0))
        return qs

    def scores_fm(qs, key_refs):
        out = []
        for p, sl in enumerate(pair_sl):
            keys = jnp.concatenate([r[sl, :] for r in key_refs], axis=1).astype(_BF16)
            out.append(_dot(qs[p], keys))
        return jnp.concatenate(out, axis=0)

    def log_terms(z):
        sp = jnp.log(1.0 + jnp.exp(-jnp.abs(z)))
        neg_part = jnp.minimum(z, 0.0)
        return neg_part - sp, (neg_part - z) - sp

    def suffix_sums(log_1m, u_ref):
        hi = log_1m.astype(_BF16)
        lo = (log_1m - hi.astype(_F32)).astype(_BF16)
        r = _dot(jnp.concatenate([hi, lo], axis=1), u_ref[...])
        return r[:, LANES:], r[:, :LANES]

    def weighted_values(fm_weights, fm_value_refs, rm_weight=None, rm_value_ref=None):
        wb = jnp.concatenate(fm_weights, axis=1).astype(_BF16)
        outs = []
        for p, sl in enumerate(pair_sl):
            rows = slice(2 * tq * p, 2 * tq * (p + 1))
            vals = jnp.concatenate([r[sl, :] for r in fm_value_refs], axis=1).astype(_BF16)
            pv = _dot_nt(wb[rows], vals)
            if rm_weight is not None:
                pv = pv + _dot(rm_weight[rows].astype(_BF16), rm_value_ref[:, sl].astype(_BF16))
            outs.append(jnp.where(even_head, pv[:tq], pv[tq:]))
        return jnp.concatenate(outs, axis=1)

    qs = [stacked_queries(j) for j in subs]
    if own_feature_major:
        z = [scores_fm(qs[j], k_fm[j:j + 3]) for j in subs]
        z_own = [zj[:, 2 * KEY_CHUNK:] for zj in z]
    else:
        z = [scores_fm(qs[0], k_fm)]
        z_own = [jnp.concatenate([_dot_nt(qs[0][p], kd_ref[:, sl].astype(_BF16))
                                  for p, sl in enumerate(pair_sl)], axis=0)]
    q_row = lax.broadcasted_iota(jnp.int32, (N_HEADS * tq, tq), 0) & (tq - 1)
    causal = lax.broadcasted_iota(jnp.int32, (N_HEADS * tq, tq), 1) < q_row
    def past_scores(j, back):
        zc = z[j][:, (2 - back) * KEY_CHUNK:(3 - back) * KEY_CHUNK]
        if past_chunks0 + j >= back:
            return zc
        return zc + jnp.where(n_past[j] >= back, 0.0, MASKED_SCORE)

    z_2 = [past_scores(j, 2) for j in subs]
    z_1 = [past_scores(j, 1) for j in subs]
    z_d = [jnp.where(causal, z_own[j], MASKED_SCORE) for j in subs]
    t_d = [log_terms(zz) for zz in z_d]
    t_1 = [log_terms(zz) for zz in z_1]
    t_2 = [log_terms(zz) for zz in z_2]
    s_d = [suffix_sums(t_d[j][1], ud_ref) for j in subs]
    s_1 = [suffix_sums(t_1[j][1], up_ref) for j in subs]
    s_2 = [suffix_sums(t_2[j][1], up_ref) for j in subs]
    a_d = [jnp.exp(t_d[j][0] + s_d[j][0]) for j in subs]
    a_1 = [jnp.exp(t_1[j][0] + (s_1[j][0] + s_d[j][1])) for j in subs]
    run_1 = [s_d[j][1] + s_1[j][1] for j in subs]
    a_2 = [jnp.exp(t_2[j][0] + (s_2[j][0] + run_1[j])) for j in subs]
    run = [run_1[j] + s_2[j][1] for j in subs]
    for j in subs:
        if own_feature_major:
            out = weighted_values((a_2[j], a_1[j], a_d[j]), v_fm[j:j + 3])
        else:
            out = weighted_values((a_2[j], a_1[j]), v_fm, a_d[j], vd_ref)
        oacc_ref[j * tq:(j + 1) * tq, :] = out
        run_ref[j] = run[j]

    def more(carry):
        c, m = carry
        return jnp.logical_and(c >= 0, m > STICK_LOG_FLOOR)

    def older(j, carry):
        c, _ = carry
        start = pl.multiple_of(c * KEY_CHUNK, KEY_CHUNK)
        k_copy = pltpu.make_async_copy(kany_ref.at[layer, b, :, pl.ds(start, KEY_CHUNK)], kbuf_ref, sem_ref.at[0])
        v_copy = pltpu.make_async_copy(vany_ref.at[layer, b, :, pl.ds(start, KEY_CHUNK)], vbuf_ref, sem_ref.at[1])
        k_copy.start()
        v_copy.start()
        k_copy.wait()
        v_copy.wait()
        lb, l1 = log_terms(scores_fm(stacked_queries(j), (kbuf_ref,)))
        inside, tot = suffix_sums(l1, up_ref)
        run_c = run_ref[j]
        oacc_ref[j * tq:(j + 1) * tq, :] += weighted_values((jnp.exp(lb + (inside + run_c)),), (vbuf_ref,))
        run_c = run_c + tot
        run_ref[j] = run_c
        return c - 1, jnp.max(run_c)

    for j in subs:
        lax.while_loop(more, functools.partial(older, j), (n_past[j] - 3, jnp.max(run[j])))

    o_ref[...] = oacc_ref[...].astype(o_ref.dtype)


def _attention(q, k_own, v_own, k_past, v_past, *, layer, past_is_own):
    batch, t_len, _ = q.shape
    if past_is_own:
        tq = KEY_CHUNK
        n_sub = ATTN_BLOCKS_PER_STEP if t_len % (ATTN_BLOCKS_PER_STEP * tq) == 0 else 1
        past_chunks0, chunks_per_step, n_fm = 0, n_sub, n_sub + 2
    else:
        tq, n_sub = t_len, 1
        assert k_past.shape[-1] % KEY_CHUNK == 0
        past_chunks0, chunks_per_step, n_fm = k_past.shape[-1] // KEY_CHUNK, 0, 2
    assert t_len % (n_sub * tq) == 0 and tq <= KEY_CHUNK and tq & (tq - 1) == 0
    n_steps = t_len // (n_sub * tq)
    assert past_is_own or n_steps == 1

    def newer(n):
        u = lax.broadcasted_iota(jnp.int32, (n, n), 0) > lax.broadcasted_iota(jnp.int32, (n, n), 1)
        once = jnp.concatenate([jnp.ones((n, LANES), _BF16), u.astype(_BF16)], axis=1)
        return jnp.concatenate([once, once], axis=0)
    u_diag, u_past = newer(tq), newer(KEY_CHUNK)

    def fm_chunk(m):
        imap = lambda b, i: (layer, b, 0, jnp.maximum(past_chunks0 + i * chunks_per_step + m - 2, 0))
        return pl.BlockSpec((None, None, ATTN_W, KEY_CHUNK), imap)

    rm_block = pl.BlockSpec((None, n_sub * tq, ATTN_W), lambda b, i: (b, i, 0))
    const = lambda arr: pl.BlockSpec(arr.shape, lambda b, i: (0,) * arr.ndim)
    any_spec = pl.BlockSpec(memory_space=pl.ANY)
    fm_specs = [fm_chunk(m) for m in range(n_fm)]
    args = [q] + [k_past] * n_fm + [v_past] * n_fm
    in_specs = [rm_block] + fm_specs + fm_specs
    if not past_is_own:
        args += [k_own, v_own]
        in_specs += [rm_block, rm_block]
    args += [k_past, v_past, u_diag, u_past]
    in_specs += [any_spec, any_spec, const(u_diag), const(u_past)]
    return pl.pallas_call(
        functools.partial(_attn_kernel, n_sub=n_sub, past_chunks0=past_chunks0, chunks_per_step=chunks_per_step,
                          layer=layer, own_feature_major=past_is_own),
        out_shape=jax.ShapeDtypeStruct((batch, t_len, ATTN_W), _BF16),
        grid=(batch, n_steps),
        in_specs=in_specs,
        out_specs=rm_block,
        scratch_shapes=[pltpu.VMEM((n_sub * tq, ATTN_W), _F32),
                        pltpu.VMEM((n_sub, N_HEADS * tq, LANES), _F32),
                        pltpu.VMEM((ATTN_W, KEY_CHUNK), _F32),
                        pltpu.VMEM((ATTN_W, KEY_CHUNK), _F32),
                        pltpu.SemaphoreType.DMA((2,))],
        compiler_params=pltpu.CompilerParams(dimension_semantics=("arbitrary", "arbitrary"),
                                             vmem_limit_bytes=VMEM_LIMIT_BYTES),
        name="stick_attention",
    )(*args)


def _outffn_kernel(xa_ref, ca_ref, oa_ref, xb_ref, cb_ref, ob_ref, wout_any, gff_ref, w1_any, w2_any,
                   ya_ref, yb_ref, wout_ref, w1_ref, w2_ref, stage_ref, sem_ref, *, n_first, layer):
    d_ff = w1_ref.shape[1]
    r = pl.program_id(0)

    @pl.when(r == 0)
    def _():
        slots, rows, cols = stage_ref.shape
        tiles = [(src, dst, i, j)
                 for src, dst in ((wout_any, wout_ref), (w1_any, w1_ref), (w2_any, w2_ref))
                 for i in range(dst.shape[0] // rows) for j in range(dst.shape[1] // cols)]

        def tile_copy(n):
            src, _, i, j = tiles[n]
            return pltpu.make_async_copy(src.at[layer, pl.ds(i * rows, rows), pl.ds(j * cols, cols)],
                                         stage_ref.at[n % slots], sem_ref.at[n % slots])
        for n in range(min(slots - 1, len(tiles))):
            tile_copy(n).start()
        for n, (_, dst, i, j) in enumerate(tiles):
            if n + slots - 1 < len(tiles):
                tile_copy(n + slots - 1).start()
            tile_copy(n).wait()
            dst[i * rows:(i + 1) * rows, j * cols:(j + 1) * cols] = stage_ref[n % slots].astype(_BF16)

    def mix_and_ffn(x_ref, c_ref, o_ref, y_ref):
        x1 = x_ref[...] + _dot(c_ref[...], wout_ref[0:CONV_CH, :]) + _dot(o_ref[...], wout_ref[CONV_CH:, :])
        h = x1 * lax.rsqrt(jnp.mean(x1 * x1, axis=-1, keepdims=True) + EPS) * gff_ref[...]
        h = h.astype(_BF16)
        ffn = None
        for cc in range(d_ff // FF_CHUNK):
            f = jnp.maximum(_dot(h, w1_ref[:, cc * FF_CHUNK:(cc + 1) * FF_CHUNK]), 0.0)
            part = _dot((f * f).astype(_BF16), w2_ref[cc * FF_CHUNK:(cc + 1) * FF_CHUNK, :])
            ffn = part if ffn is None else ffn + part
        y_ref[...] = x1 + ffn

    @pl.when(r < n_first)
    def _():
        mix_and_ffn(xa_ref, ca_ref, oa_ref, ya_ref)

    @pl.when(r >= n_first)
    def _():
        mix_and_ffn(xb_ref, cb_ref, ob_ref, yb_ref)


def _outffn(xa, ca, oa, xb, cb, ob, wout, gff, w1, w2, *, layer):
    d_model = xa.shape[1]
    d_ff = w1.shape[-1]
    tb = ROW_BLOCK
    assert xa.shape[0] % tb == 0 and xb.shape[0] % tb == 0 and d_ff % FF_CHUNK == 0
    assert d_model % WEIGHT_STAGE_ROWS == 0 and d_ff % WEIGHT_STAGE_ROWS == 0
    na, nb = xa.shape[0] // tb, xb.shape[0] // tb
    first = lambda width: pl.BlockSpec((tb, width), lambda r: (jnp.minimum(r, na - 1), 0))
    second = lambda width: pl.BlockSpec((tb, width), lambda r: (jnp.maximum(r - na, 0), 0))
    any_spec = pl.BlockSpec(memory_space=pl.ANY)
    gff_spec = pl.BlockSpec((None,) + gff.shape[1:], lambda r: (layer, 0, 0))
    return pl.pallas_call(
        functools.partial(_outffn_kernel, n_first=na, layer=layer),
        out_shape=(jax.ShapeDtypeStruct(xa.shape, _F32), jax.ShapeDtypeStruct(xb.shape, _F32)),
        grid=(na + nb,),
        in_specs=[first(d_model), first(CONV_CH), first(ATTN_W), second(d_model), second(CONV_CH), second(ATTN_W),
                  any_spec, gff_spec, any_spec, any_spec],
        out_specs=(first(d_model), second(d_model)),
        scratch_shapes=[pltpu.VMEM(wout.shape[1:], _BF16), pltpu.VMEM(w1.shape[1:], _BF16),
                        pltpu.VMEM(w2.shape[1:], _BF16),
                        pltpu.VMEM((WEIGHT_STAGE_SLOTS, WEIGHT_STAGE_ROWS, d_model), _F32),
                        pltpu.SemaphoreType.DMA((WEIGHT_STAGE_SLOTS,))],
        compiler_params=pltpu.CompilerParams(dimension_semantics=("arbitrary",),
                                             vmem_limit_bytes=VMEM_LIMIT_BYTES),
        name="outproj_ffn",
    )(xa, ca, oa, xb, cb, ob, wout, gff, w1, w2)


def kernel(x_prompt, x_sample, cache_k, cache_v, state_conv, g_mix, w_in, w_dw, b_dw, g_cn, b_cn, w_pw,
           g_q, g_k, w_out, g_ff, w_ff1, w_ff2):
    depth = w_in.shape[0]
    bp, tp, _ = x_prompt.shape
    bs, ts, _ = x_sample.shape
    n_past = cache_k.shape[2]
    head_id = lax.broadcasted_iota(jnp.int32, (ATTN_W, ATTN_W), 0) // HEAD_DIM
    gmat = (head_id == head_id.T).astype(_BF16)
    zero_hist = jnp.zeros((1, bp, HIST_PAD, CONV_CH), _F32)
    sample_hist = jnp.pad(state_conv, ((0, 0), (0, 0), (HIST_PAD - HIST, 0), (0, 0)))
    row = lambda a: a.reshape(depth, 1, -1)
    per_head = lambda g: jnp.tile(g, (1, N_HEADS))
    win = w_in.astype(_BF16)
    w = dict(gmix=row(g_mix), win=win,
             wdw=jnp.broadcast_to(w_dw[:, :, None, :], w_dw.shape[:2] + (SUBLANES, CONV_CH)), bdw=row(b_dw), gcn=row(g_cn), bcn=row(b_cn), wpw=w_pw.astype(_BF16),
             gq=row(per_head(g_q)), gk=row(per_head(g_k)),
             gk_col=jnp.broadcast_to(per_head(g_k)[:, :, None], (depth, ATTN_W, LANES)), gmat=gmat,
             wout=w_out, gff=row(g_ff), w1=w_ff1, w2=w_ff2)
    to_fm = lambda a: jnp.transpose(a, (0, 1, 3, 4, 2)).reshape(a.shape[0], a.shape[1], ATTN_W, a.shape[2])
    from_fm = lambda a: jnp.transpose(a.reshape(a.shape[0], a.shape[1], N_HEADS, HEAD_DIM, a.shape[3]),
                                      (0, 1, 4, 2, 3))
    cache_k_fm, cache_v_fm = to_fm(cache_k), to_fm(cache_v)

    hp, hs = x_prompt, x_sample
    kv_stack = (None, None)
    conv_p, k_s, v_s, conv_s = [], [], [], []
    flat = lambda a: a.reshape(-1, a.shape[-1])
    for l in range(depth):
        cp, q, k_fm, v_fm, cv = _inproj(hp, zero_hist, 0, w, layer=l, n_seg=1, seg_rows=min(INPROJ_ROW_BLOCK, tp),
                                        kv_stack=kv_stack)
        kv_stack = (k_fm, v_fm)
        op = _attention(q, None, None, k_fm, v_fm, layer=l, past_is_own=True)
        conv_p.append(cv)
        cs, q, k, v, cv = _inproj(hs, sample_hist, l, w, layer=l, n_seg=bs, seg_rows=ts)
        os_ = _attention(q, k, v, cache_k_fm, cache_v_fm, layer=l, past_is_own=False)
        k_s.append(k.reshape(bs, ts, N_HEADS, HEAD_DIM))
        v_s.append(v.reshape(bs, ts, N_HEADS, HEAD_DIM))
        conv_s.append(cv)
        yp, ys = _outffn(flat(hp), flat(cp), flat(op), flat(hs), flat(cs), flat(os_),
                         w["wout"], w["gff"], w["w1"], w["w2"], layer=l)
        hp, hs = yp.reshape(hp.shape), ys.reshape(hs.shape)
    return (hp, hs, from_fm(kv_stack[0]), from_fm(kv_stack[1]), jnp.stack(conv_p),
            jnp.stack(k_s), jnp.stack(v_s), jnp.stack(conv_s))
```

```python
import functools

import jax
import jax.numpy as jnp
from jax import lax
from jax.experimental import pallas as pl
from jax.experimental.pallas import tpu as pltpu

EPS = 1e-6
CONV_WIDTH = 31
HIST = CONV_WIDTH - 1
HIST_PAD = 32
SUBLANES = 8
N_HEADS = 8
HEAD_DIM = 64
ATTN_W = N_HEADS * HEAD_DIM
CONV_CH = 512
LANES = 128
KEY_CHUNK = 128
CONV_ROWS = 32
FF_CHUNK = 1024
ROW_BLOCK = 512
INPROJ_ROW_BLOCK = 1024
WEIGHT_STAGE_ROWS = 256
WEIGHT_STAGE_SLOTS = 8
VMEM_LIMIT_BYTES = 56 * 1024 * 1024
ATTN_BLOCKS_PER_STEP = 4
STICK_LOG_FLOOR = -87.5
MASKED_SCORE = -1e30

_F32 = jnp.float32
_BF16 = jnp.bfloat16


def _dot(a, b):
    return jnp.dot(a, b, preferred_element_type=_F32)


def _dot_nt(a, b):
    return lax.dot_general(a, b, (((1,), (1,)), ((), ())), preferred_element_type=_F32)


def _dot_tn(a, b):
    return lax.dot_general(a, b, (((0,), (1,)), ((), ())), preferred_element_type=_F32)


def _dot_split2(x, m):
    hi = x.astype(_BF16)
    lo = (x - hi.astype(_F32)).astype(_BF16)
    return _dot(hi, m) + _dot(lo, m)


def _inproj_kernel(*refs, n_seg, seg_rows, feature_major_kv, first_layer):
    if feature_major_kv:
        (x_ref, hist_ref, gmix_ref, win_ref, wdw_ref, bdw_ref, gcn_ref, bcn_ref, wpw_ref,
         gq_ref, gk_ref, gmat_ref, _, _,
         c_ref, q_ref, k_ref, v_ref, nconv_ref, uext_ref, cs_ref, act_ref) = refs
        if first_layer:
            for stack_ref in (k_ref, v_ref):
                if stack_ref.shape[0] > 1:
                    stack_ref[1:] = jnp.zeros((stack_ref.shape[0] - 1,) + stack_ref.shape[1:], _F32)
            k_ref, v_ref = k_ref.at[0], v_ref.at[0]
    else:
        (x_ref, hist_ref, gmix_ref, win_ref, wdw_ref, bdw_ref, gcn_ref, bcn_ref, wpw_ref,
         gq_ref, gk_ref, gmat_ref,
         c_ref, q_ref, k_ref, v_ref, nconv_ref, uext_ref, cs_ref, act_ref) = refs
    t = pl.program_id(1)
    n_t = pl.num_programs(1)
    rows = n_seg * seg_rows
    d_model = x_ref.shape[-1]

    x = x_ref[...].reshape(rows, d_model)
    h = x * lax.rsqrt(jnp.mean(x * x, axis=-1, keepdims=True) + EPS) * gmix_ref[...]
    h = h.astype(_BF16)

    a = _dot(h, win_ref[:, 0:CONV_CH])
    gate = _dot(h, win_ref[:, CONV_CH:2 * CONV_CH])
    u = a * jax.nn.sigmoid(gate)

    def head_norm(z, g_ref):
        ssq = _dot_split2(z * z, gmat_ref[...])
        return z * lax.rsqrt(ssq * (1.0 / HEAD_DIM) + EPS) * g_ref[...]

    q = _dot(h, win_ref[:, 2 * CONV_CH:2 * CONV_CH + ATTN_W])
    ssq_q = _dot((q * q).astype(_BF16), gmat_ref[...])
    q = q * lax.rsqrt(ssq_q * (1.0 / HEAD_DIM) + EPS) * gq_ref[...]
    q_ref[...] = (q * (HEAD_DIM ** -0.5)).astype(_BF16).reshape(q_ref.shape)
    if feature_major_kv:
        k_t = _dot_tn(win_ref[:, 2 * CONV_CH + ATTN_W:2 * CONV_CH + 2 * ATTN_W], h)
        gain = jnp.concatenate([gk_ref[...]] * (rows // LANES), axis=1)
        for hh in range(N_HEADS):
            rs = slice(hh * HEAD_DIM, (hh + 1) * HEAD_DIM)
            kh = k_t[rs]
            ssq = jnp.sum(kh * kh, axis=0, keepdims=True)
            k_ref[rs, :] = kh * lax.rsqrt(ssq * (1.0 / HEAD_DIM) + EPS) * gain[rs]
        v_ref[...] = _dot_tn(win_ref[:, 2 * CONV_CH + 2 * ATTN_W:], h)
    else:
        k = _dot(h, win_ref[:, 2 * CONV_CH + ATTN_W:2 * CONV_CH + 2 * ATTN_W])
        k_ref[...] = head_norm(k, gk_ref).reshape(k_ref.shape)
        v_ref[...] = _dot(h, win_ref[:, 2 * CONV_CH + 2 * ATTN_W:]).reshape(v_ref.shape)

    lane_sl = [slice(g * LANES, (g + 1) * LANES) for g in range(CONV_CH // LANES)]
    u3 = u.reshape(n_seg, seg_rows, CONV_CH)

    @pl.when(t == 0)
    def _():
        for g, ls in enumerate(lane_sl):
            uext_ref[:, g, 0:HIST_PAD, :] = hist_ref[:, :, ls]

    for g, ls in enumerate(lane_sl):
        uext_ref[:, g, HIST_PAD:HIST_PAD + seg_rows, :] = u3[:, :, ls]

    half = CONV_ROWS // 2
    for s in range(n_seg):
        for r in range(seg_rows // CONV_ROWS):
            r0 = r * CONV_ROWS
            acc_even = jnp.zeros((half // SUBLANES, SUBLANES, CONV_CH), _F32)
            acc_odd = jnp.zeros((half // SUBLANES, SUBLANES, CONV_CH), _F32)
            for m in range(CONV_WIDTH + 1):
                start = r0 + m + (HIST_PAD - HIST)
                window = jnp.concatenate(
                    [uext_ref[s, g, pl.ds(start, half, stride=2), :] for g in range(len(lane_sl))], axis=1)
                window = window.reshape(half // SUBLANES, SUBLANES, CONV_CH)
                if m < CONV_WIDTH:
                    acc_even = acc_even + wdw_ref[m] * window
                if m >= 1:
                    acc_odd = acc_odd + wdw_ref[m - 1] * window
            acc_even = acc_even.reshape(half, CONV_CH)
            acc_odd = acc_odd.reshape(half, CONV_CH)
            for par, acc in enumerate((acc_even, acc_odd)):
                cv = acc + bdw_ref[...]
                mu = jnp.mean(cv, axis=-1, keepdims=True)
                xc = cv - mu
                y = xc * lax.rsqrt(jnp.mean(xc * xc, axis=-1, keepdims=True) + EPS)
                y = y * gcn_ref[...] + bcn_ref[...]
                y = y * jax.nn.sigmoid(y)
                base = s * seg_rows + r0 + par * half
                act_ref[base:base + half, :] = y.astype(_BF16)

    c_perm = _dot(act_ref[...], wpw_ref[...])
    for s in range(n_seg):
        for r in range(seg_rows // CONV_ROWS):
            r0 = r * CONV_ROWS
            base = s * seg_rows + r0
            for par in range(2):
                rows_par = c_perm[base + par * half:base + (par + 1) * half]
                for g, ls in enumerate(lane_sl):
                    cs_ref[s, g, pl.ds(r0 + par, half, stride=2), :] = rows_par[:, ls]
    for g, ls in enumerate(lane_sl):
        c_ref[:, :, ls] = cs_ref[:, g, :, :].astype(_BF16)

    @pl.when(t == n_t - 1)
    def _():
        for g, ls in enumerate(lane_sl):
            nconv_ref[:, :, ls] = uext_ref[:, g, seg_rows + (HIST_PAD - HIST):seg_rows + HIST_PAD, :]

    uext_ref[:, :, 0:HIST_PAD, :] = uext_ref[:, :, seg_rows:seg_rows + HIST_PAD, :]


def _inproj(x, hist, hist_layer, w, *, layer, n_seg, seg_rows, kv_stack=None):
    batch, t_len, d_model = x.shape
    depth = w["win"].shape[0]
    assert batch % n_seg == 0 and t_len % seg_rows == 0 and seg_rows % CONV_ROWS == 0
    n_t = t_len // seg_rows
    assert n_seg == 1 or n_t == 1
    feature_major_kv = kv_stack is not None
    rows_spec = lambda width: pl.BlockSpec((n_seg, seg_rows, width), lambda b, t: (b, t, 0))
    shared = lambda arr: pl.BlockSpec(arr.shape, lambda b, t: (0,) * arr.ndim)
    const = lambda arr: pl.BlockSpec((None,) + arr.shape[1:], lambda b, t: (layer,) + (0,) * (arr.ndim - 1),
                                     pipeline_mode=pl.Buffered(1))
    hist_spec = pl.BlockSpec((None, n_seg, HIST_PAD, CONV_CH), lambda b, t: (hist_layer, b, 0, 0))
    nconv_spec = pl.BlockSpec((n_seg, HIST, CONV_CH), lambda b, t: (b, 0, 0))
    small = [w["wdw"], w["bdw"], w["gcn"], w["bcn"], w["wpw"]]
    aliases = {}
    if feature_major_kv:
        assert n_seg == 1 and seg_rows % LANES == 0
        args = [x, hist, w["gmix"], w["win"]] + small + [w["gq"], w["gk_col"], w["gmat"]]
        in_specs = [rows_spec(d_model), hist_spec] + [const(a) for a in args[2:-1]] + [shared(w["gmat"])]
        kv_shape = jax.ShapeDtypeStruct((depth, batch, ATTN_W, t_len), _F32)
        first_layer = kv_stack[0] is None
        if first_layer:
            kv_spec = pl.BlockSpec((depth, None, ATTN_W, seg_rows), lambda b, t: (0, b, 0, t))
            dummy = jnp.zeros((SUBLANES, LANES), _F32)
            args += [dummy, dummy]
            in_specs += [shared(dummy)] * 2
        else:
            kv_spec = pl.BlockSpec((None, None, ATTN_W, seg_rows), lambda b, t: (layer, b, 0, t))
            aliases = {len(args): 2, len(args) + 1: 3}
            args += list(kv_stack)
            in_specs += [pl.BlockSpec(memory_space=pl.ANY)] * 2
    else:
        args = [x, hist, w["gmix"], w["win"]] + small + [w["gq"], w["gk"], w["gmat"]]
        in_specs = [rows_spec(d_model), hist_spec] + [const(a) for a in args[2:-1]] + [shared(w["gmat"])]
        kv_shape = jax.ShapeDtypeStruct((batch, t_len, ATTN_W), _F32)
        kv_spec = rows_spec(ATTN_W)
        first_layer = False
    out_shape = (
        jax.ShapeDtypeStruct((batch, t_len, CONV_CH), _BF16),
        jax.ShapeDtypeStruct((batch, t_len, ATTN_W), _BF16),
        kv_shape, kv_shape,
        jax.ShapeDtypeStruct((batch, HIST, CONV_CH), _F32),
    )
    return pl.pallas_call(
        functools.partial(_inproj_kernel, n_seg=n_seg, seg_rows=seg_rows, feature_major_kv=feature_major_kv,
                          first_layer=first_layer),
        out_shape=out_shape,
        grid=(batch // n_seg, n_t),
        in_specs=in_specs,
        out_specs=(rows_spec(CONV_CH), rows_spec(ATTN_W), kv_spec, kv_spec, nconv_spec),
        scratch_shapes=[pltpu.VMEM((n_seg, CONV_CH // LANES, seg_rows + HIST_PAD, LANES), _F32),
                        pltpu.VMEM((n_seg, CONV_CH // LANES, seg_rows, LANES), _F32),
                        pltpu.VMEM((n_seg * seg_rows, CONV_CH), _BF16)],
        input_output_aliases=aliases,
        compiler_params=pltpu.CompilerParams(dimension_semantics=("arbitrary", "arbitrary"),
                                             vmem_limit_bytes=VMEM_LIMIT_BYTES),
        name="inproj_conv",
    )(*args)


def _attn_kernel(*refs, n_sub, past_chunks0, chunks_per_step, layer, own_feature_major):
    n_fm = n_sub + 2 if own_feature_major else 2
    q_ref, k_fm, v_fm, rest = refs[0], refs[1:1 + n_fm], refs[1 + n_fm:1 + 2 * n_fm], refs[1 + 2 * n_fm:]
    if not own_feature_major:
        kd_ref, vd_ref, rest = rest[0], rest[1], rest[2:]
    kany_ref, vany_ref, ud_ref, up_ref, o_ref, oacc_ref, run_ref, kbuf_ref, vbuf_ref, sem_ref = rest
    b = pl.program_id(0)
    i = pl.program_id(1)
    tq = q_ref.shape[0] // n_sub
    n_pairs = ATTN_W // LANES
    subs = range(n_sub)
    n_past = [past_chunks0 + i * chunks_per_step + j for j in subs]
    even_head = lax.broadcasted_iota(jnp.int32, (tq, LANES), 1) < HEAD_DIM
    pair_sl = [slice(p * LANES, (p + 1) * LANES) for p in range(n_pairs)]

    def stacked_queries(j):
        qs = []
        for sl in pair_sl:
            qp = q_ref[j * tq:(j + 1) * tq, sl]
            qs.append(jnp.concatenate([jnp.where(even_head, qp, 0.0), jnp.where(even_head, 0.0, qp)], axis=0))
        return qs

    def scores_fm(qs, key_refs):
        out = []
        for p, sl in enumerate(pair_sl):
            keys = jnp.concatenate([r[sl, :] for r in key_refs], axis=1).astype(_BF16)
            out.append(_dot(qs[p], keys))
        return jnp.concatenate(out, axis=0)

    def log_terms(z):
        sp = jnp.log(1.0 + jnp.exp(-jnp.abs(z)))
        neg_part = jnp.minimum(z, 0.0)
        return neg_part - sp, (neg_part - z) - sp

    def suffix_sums(log_1m, u_ref):
        hi = log_1m.astype(_BF16)
        lo = (log_1m - hi.astype(_F32)).astype(_BF16)
        r = _dot(jnp.concatenate([hi, lo], axis=1), u_ref[...])
        return r[:, LANES:], r[:, :LANES]

    def weighted_values(fm_weights, fm_value_refs, rm_weight=None, rm_value_ref=None):
        wb = jnp.concatenate(fm_weights, axis=1).astype(_BF16)
        outs = []
        for p, sl in enumerate(pair_sl):
            rows = slice(2 * tq * p, 2 * tq * (p + 1))
            vals = jnp.concatenate([r[sl, :] for r in fm_value_refs], axis=1).astype(_BF16)
            pv = _dot_nt(wb[rows], vals)
            if rm_weight is not None:
                pv = pv + _dot(rm_weight[rows].astype(_BF16), rm_value_ref[:, sl].astype(_BF16))
            outs.append(jnp.where(even_head, pv[:tq], pv[tq:]))
        return jnp.concatenate(outs, axis=1)

    qs = [stacked_queries(j) for j in subs]
    if own_feature_major:
        z = [scores_fm(qs[j], k_fm[j:j + 3]) for j in subs]
        z_own = [zj[:, 2 * KEY_CHUNK:] for zj in z]
    else:
        z = [scores_fm(qs[0], k_fm)]
        z_own = [jnp.concatenate([_dot_nt(qs[0][p], kd_ref[:, sl].astype(_BF16))
                                  for p, sl in enumerate(pair_sl)], axis=0)]
    q_row = lax.broadcasted_iota(jnp.int32, (N_HEADS * tq, tq), 0) & (tq - 1)
    causal = lax.broadcasted_iota(jnp.int32, (N_HEADS * tq, tq), 1) < q_row
    def past_scores(j, back):
        zc = z[j][:, (2 - back) * KEY_CHUNK:(3 - back) * KEY_CHUNK]
        if past_chunks0 + j >= back:
            return zc
        return zc + jnp.where(n_past[j] >= back, 0.0, MASKED_SCORE)

    z_2 = [past_scores(j, 2) for j in subs]
    z_1 = [past_scores(j, 1) for j in subs]
    z_d = [jnp.where(causal, z_own[j], MASKED_SCORE) for j in subs]
    t_d = [log_terms(zz) for zz in z_d]
    t_1 = [log_terms(zz) for zz in z_1]
    t_2 = [log_terms(zz) for zz in z_2]
    s_d = [suffix_sums(t_d[j][1], ud_ref) for j in subs]
    s_1 = [suffix_sums(t_1[j][1], up_ref) for j in subs]
    s_2 = [suffix_sums(t_2[j][1], up_ref) for j in subs]
    a_d = [jnp.exp(t_d[j][0] + s_d[j][0]) for j in subs]
    a_1 = [jnp.exp(t_1[j][0] + (s_1[j][0] + s_d[j][1])) for j in subs]
    run_1 = [s_d[j][1] + s_1[j][1] for j in subs]
    a_2 = [jnp.exp(t_2[j][0] + (s_2[j][0] + run_1[j])) for j in subs]
    run = [run_1[j] + s_2[j][1] for j in subs]
    for j in subs:
        if own_feature_major:
            out = weighted_values((a_2[j], a_1[j], a_d[j]), v_fm[j:j + 3])
        else:
            out = weighted_values((a_2[j], a_1[j]), v_fm, a_d[j], vd_ref)
        oacc_ref[j * tq:(j + 1) * tq, :] = out
        run_ref[j] = run[j]

    def more(carry):
        c, m = carry
        return jnp.logical_and(c >= 0, m > STICK_LOG_FLOOR)

    def older(j, carry):
        c, _ = carry
        start = pl.multiple_of(c * KEY_CHUNK, KEY_CHUNK)
        k_copy = pltpu.make_async_copy(kany_ref.at[layer, b, :, pl.ds(start, KEY_CHUNK)], kbuf_ref, sem_ref.at[0])
        v_copy = pltpu.make_async_copy(vany_ref.at[layer, b, :, pl.ds(start, KEY_CHUNK)], vbuf_ref, sem_ref.at[1])
        k_copy.start()
        v_copy.start()
        k_copy.wait()
        v_copy.wait()
        lb, l1 = log_terms(scores_fm(stacked_queries(j), (kbuf_ref,)))
        inside, tot = suffix_sums(l1, up_ref)
        run_c = run_ref[j]
        oacc_ref[j * tq:(j + 1) * tq, :] += weighted_values((jnp.exp(lb + (inside + run_c)),), (vbuf_ref,))
        run_c = run_c + tot
        run_ref[j] = run_c
        return c - 1, jnp.max(run_c)

    for j in subs:
        lax.while_loop(more, functools.partial(older, j), (n_past[j] - 3, jnp.max(run[j])))

    o_ref[...] = oacc_ref[...].astype(o_ref.dtype)


def _attention(q, k_own, v_own, k_past, v_past, *, layer, past_is_own):
    batch, t_len, _ = q.shape
    if past_is_own:
        tq = KEY_CHUNK
        n_sub = ATTN_BLOCKS_PER_STEP if t_len % (ATTN_BLOCKS_PER_STEP * tq) == 0 else 1
        past_chunks0, chunks_per_step, n_fm = 0, n_sub, n_sub + 2
    else:
        tq, n_sub = t_len, 1
        assert k_past.shape[-1] % KEY_CHUNK == 0
        past_chunks0, chunks_per_step, n_fm = k_past.shape[-1] // KEY_CHUNK, 0, 2
    assert t_len % (n_sub * tq) == 0 and tq <= KEY_CHUNK and tq & (tq - 1) == 0
    n_steps = t_len // (n_sub * tq)
    assert past_is_own or n_steps == 1

    def newer(n):
        u = lax.broadcasted_iota(jnp.int32, (n, n), 0) > lax.broadcasted_iota(jnp.int32, (n, n), 1)
        once = jnp.concatenate([jnp.ones((n, LANES), _BF16), u.astype(_BF16)], axis=1)
        return jnp.concatenate([once, once], axis=0)
    u_diag, u_past = newer(tq), newer(KEY_CHUNK)

    def fm_chunk(m):
        imap = lambda b, i: (layer, b, 0, jnp.maximum(past_chunks0 + i * chunks_per_step + m - 2, 0))
        return pl.BlockSpec((None, None, ATTN_W, KEY_CHUNK), imap)

    rm_block = pl.BlockSpec((None, n_sub * tq, ATTN_W), lambda b, i: (b, i, 0))
    const = lambda arr: pl.BlockSpec(arr.shape, lambda b, i: (0,) * arr.ndim)
    any_spec = pl.BlockSpec(memory_space=pl.ANY)
    fm_specs = [fm_chunk(m) for m in range(n_fm)]
    args = [q] + [k_past] * n_fm + [v_past] * n_fm
    in_specs = [rm_block] + fm_specs + fm_specs
    if not past_is_own:
        args += [k_own, v_own]
        in_specs += [rm_block, rm_block]
    args += [k_past, v_past, u_diag, u_past]
    in_specs += [any_spec, any_spec, const(u_diag), const(u_past)]
    return pl.pallas_call(
        functools.partial(_attn_kernel, n_sub=n_sub, past_chunks0=past_chunks0, chunks_per_step=chunks_per_step,
                          layer=layer, own_feature_major=past_is_own),
        out_shape=jax.ShapeDtypeStruct((batch, t_len, ATTN_W), _BF16),
        grid=(batch, n_steps),
        in_specs=in_specs,
        out_specs=rm_block,
        scratch_shapes=[pltpu.VMEM((n_sub * tq, ATTN_W), _F32),
                        pltpu.VMEM((n_sub, N_HEADS * tq, LANES), _F32),
                        pltpu.VMEM((ATTN_W, KEY_CHUNK), _F32),
                        pltpu.VMEM((ATTN_W, KEY_CHUNK), _F32),
                        pltpu.SemaphoreType.DMA((2,))],
        compiler_params=pltpu.CompilerParams(dimension_semantics=("arbitrary", "arbitrary"),
                                             vmem_limit_bytes=VMEM_LIMIT_BYTES),
        name="stick_attention",
    )(*args)


def _outffn_kernel(xa_ref, ca_ref, oa_ref, xb_ref, cb_ref, ob_ref, wout_any, gff_ref, w1_any, w2_any,
                   ya_ref, yb_ref, wout_ref, w1_ref, w2_ref, stage_ref, sem_ref, *, n_first, layer):
    d_ff = w1_ref.shape[1]
    r = pl.program_id(0)

    @pl.when(r == 0)
    def _():
        slots, rows, cols = stage_ref.shape
        tiles = [(src, dst, i, j)
                 for src, dst in ((wout_any, wout_ref), (w1_any, w1_ref), (w2_any, w2_ref))
                 for i in range(dst.shape[0] // rows) for j in range(dst.shape[1] // cols)]

        def tile_copy(n):
            src, _, i, j = tiles[n]
            return pltpu.make_async_copy(src.at[layer, pl.ds(i * rows, rows), pl.ds(j * cols, cols)],
                                         stage_ref.at[n % slots], sem_ref.at[n % slots])
        for n in range(min(slots - 1, len(tiles))):
            tile_copy(n).start()
        for n, (_, dst, i, j) in enumerate(tiles):
            if n + slots - 1 < len(tiles):
                tile_copy(n + slots - 1).start()
            tile_copy(n).wait()
            dst[i * rows:(i + 1) * rows, j * cols:(j + 1) * cols] = stage_ref[n % slots].astype(_BF16)

    def mix_and_ffn(x_ref, c_ref, o_ref, y_ref):
        x1 = x_ref[...] + _dot(c_ref[...], wout_ref[0:CONV_CH, :]) + _dot(o_ref[...], wout_ref[CONV_CH:, :])
        h = x1 * lax.rsqrt(jnp.mean(x1 * x1, axis=-1, keepdims=True) + EPS) * gff_ref[...]
        h = h.astype(_BF16)
        ffn = None
        for cc in range(d_ff // FF_CHUNK):
            f = jnp.maximum(_dot(h, w1_ref[:, cc * FF_CHUNK:(cc + 1) * FF_CHUNK]), 0.0)
            part = _dot((f * f).astype(_BF16), w2_ref[cc * FF_CHUNK:(cc + 1) * FF_CHUNK, :])
            ffn = part if ffn is None else ffn + part
        y_ref[...] = x1 + ffn

    @pl.when(r < n_first)
    def _():
        mix_and_ffn(xa_ref, ca_ref, oa_ref, ya_ref)

    @pl.when(r >= n_first)
    def _():
        mix_and_ffn(xb_ref, cb_ref, ob_ref, yb_ref)


def _outffn(xa, ca, oa, xb, cb, ob, wout, gff, w1, w2, *, layer):
    d_model = xa.shape[1]
    d_ff = w1.shape[-1]
    tb = ROW_BLOCK
    assert xa.shape[0] % tb == 0 and xb.shape[0] % tb == 0 and d_ff % FF_CHUNK == 0
    assert d_model % WEIGHT_STAGE_ROWS == 0 and d_ff % WEIGHT_STAGE_ROWS == 0
    na, nb = xa.shape[0] // tb, xb.shape[0] // tb
    first = lambda width: pl.BlockSpec((tb, width), lambda r: (jnp.minimum(r, na - 1), 0))
    second = lambda width: pl.BlockSpec((tb, width), lambda r: (jnp.maximum(r - na, 0), 0))
    any_spec = pl.BlockSpec(memory_space=pl.ANY)
    gff_spec = pl.BlockSpec((None,) + gff.shape[1:], lambda r: (layer, 0, 0))
    return pl.pallas_call(
        functools.partial(_outffn_kernel, n_first=na, layer=layer),
        out_shape=(jax.ShapeDtypeStruct(xa.shape, _F32), jax.ShapeDtypeStruct(xb.shape, _F32)),
        grid=(na + nb,),
        in_specs=[first(d_model), first(CONV_CH), first(ATTN_W), second(d_model), second(CONV_CH), second(ATTN_W),
                  any_spec, gff_spec, any_spec, any_spec],
        out_specs=(first(d_model), second(d_model)),
        scratch_shapes=[pltpu.VMEM(wout.shape[1:], _BF16), pltpu.VMEM(w1.shape[1:], _BF16),
                        pltpu.VMEM(w2.shape[1:], _BF16),
                        pltpu.VMEM((WEIGHT_STAGE_SLOTS, WEIGHT_STAGE_ROWS, d_model), _F32),
                        pltpu.SemaphoreType.DMA((WEIGHT_STAGE_SLOTS,))],
        compiler_params=pltpu.CompilerParams(dimension_semantics=("arbitrary",),
                                             vmem_limit_bytes=VMEM_LIMIT_BYTES),
        name="outproj_ffn",
    )(xa, ca, oa, xb, cb, ob, wout, gff, w1, w2)


def kernel(x_prompt, x_sample, cache_k, cache_v, state_conv, g_mix, w_in, w_dw, b_dw, g_cn, b_cn, w_pw,
           g_q, g_k, w_out, g_ff, w_ff1, w_ff2):
    depth = w_in.shape[0]
    bp, tp, _ = x_prompt.shape
    bs, ts, _ = x_sample.shape
    n_past = cache_k.shape[2]
    head_id = lax.broadcasted_iota(jnp.int32, (ATTN_W, ATTN_W), 0) // HEAD_DIM
    gmat = (head_id == head_id.T).astype(_BF16)
    zero_hist = jnp.zeros((1, bp, HIST_PAD, CONV_CH), _F32)
    sample_hist = jnp.pad(state_conv, ((0, 0), (0, 0), (HIST_PAD - HIST, 0), (0, 0)))
    row = lambda a: a.reshape(depth, 1, -1)
    per_head = lambda g: jnp.tile(g, (1, N_HEADS))
    win = w_in.astype(_BF16)
    w = dict(gmix=row(g_mix), win=win,
             wdw=jnp.broadcast_to(w_dw[:, :, None, :], w_dw.shape[:2] + (SUBLANES, CONV_CH)), bdw=row(b_dw), gcn=row(g_cn), bcn=row(b_cn), wpw=w_pw.astype(_BF16),
             gq=row(per_head(g_q)), gk=row(per_head(g_k)),
             gk_col=jnp.broadcast_to(per_head(g_k)[:, :, None], (depth, ATTN_W, LANES)), gmat=gmat,
             wout=w_out, gff=row(g_ff), w1=w_ff1, w2=w_ff2)
    to_fm = lambda a: jnp.transpose(a, (0, 1, 3, 4, 2)).reshape(a.shape[0], a.shape[1], ATTN_W, a.shape[2])
    from_fm = lambda a: jnp.transpose(a.reshape(a.shape[0], a.shape[1], N_HEADS, HEAD_DIM, a.shape[3]),
                                      (0, 1, 4, 2, 3))
    cache_k_fm, cache_v_fm = to_fm(cache_k), to_fm(cache_v)

    hp, hs = x_prompt, x_sample
    kv_stack = (None, None)
    conv_p, k_s, v_s, conv_s = [], [], [], []
    flat = lambda a: a.reshape(-1, a.shape[-1])
    for l in range(depth):
        cp, q, k_fm, v_fm, cv = _inproj(hp, zero_hist, 0, w, layer=l, n_seg=1, seg_rows=min(INPROJ_ROW_BLOCK, tp),
                                        kv_stack=kv_stack)
        kv_stack = (k_fm, v_fm)
        op = _attention(q, None, None, k_fm, v_fm, layer=l, past_is_own=True)
        conv_p.append(cv)
        cs, q, k, v, cv = _inproj(hs, sample_hist, l, w, layer=l, n_seg=bs, seg_rows=ts)
        os_ = _attention(q, k, v, cache_k_fm, cache_v_fm, layer=l, past_is_own=False)
        k_s.append(k.reshape(bs, ts, N_HEADS, HEAD_DIM))
        v_s.append(v.reshape(bs, ts, N_HEADS, HEAD_DIM))
        conv_s.append(cv)
        yp, ys = _outffn(flat(hp), flat(cp), flat(op), flat(hs), flat(cs), flat(os_),
                         w["wout"], w["gff"], w["w1"], w["w2"], layer=l)
        hp, hs = yp.reshape(hp.shape), ys.reshape(hs.shape)
    return (hp, hs, from_fm(kv_stack[0]), from_fm(kv_stack[1]), jnp.stack(conv_p),
            jnp.stack(k_s), jnp.stack(v_s), jnp.stack(conv_s))
```

```python
import functools

import jax
import jax.numpy as jnp
from jax import lax
from jax.experimental import pallas as pl
from jax.experimental.pallas import tpu as pltpu

EPS = 1e-6
CONV_WIDTH = 31
HIST = CONV_WIDTH - 1
HIST_PAD = 32
SUBLANES = 8
N_HEADS = 8
HEAD_DIM = 64
ATTN_W = N_HEADS * HEAD_DIM
CONV_CH = 512
LANES = 128
KEY_CHUNK = 128
CONV_ROWS = 32
FF_CHUNK = 1024
ROW_BLOCK = 512
INPROJ_ROW_BLOCK = 1024
WEIGHT_STAGE_ROWS = 256
WEIGHT_STAGE_SLOTS = 8
VMEM_LIMIT_BYTES = 56 * 1024 * 1024
ATTN_BLOCKS_PER_STEP = 4
STICK_LOG_FLOOR = -87.5
MASKED_SCORE = -1e30

_F32 = jnp.float32
_BF16 = jnp.bfloat16


def _dot(a, b):
    return jnp.dot(a, b, preferred_element_type=_F32)


def _dot_nt(a, b):
    return lax.dot_general(a, b, (((1,), (1,)), ((), ())), preferred_element_type=_F32)


def _dot_tn(a, b):
    return lax.dot_general(a, b, (((0,), (1,)), ((), ())), preferred_element_type=_F32)


def _dot_split2(x, m):
    hi = x.astype(_BF16)
    lo = (x - hi.astype(_F32)).astype(_BF16)
    return _dot(hi, m) + _dot(lo, m)


def _inproj_kernel(*refs, n_seg, seg_rows, feature_major_kv, first_layer):
    if feature_major_kv:
        (x_ref, hist_ref, gmix_ref, win_ref, wdw_ref, bdw_ref, gcn_ref, bcn_ref, wpw_ref,
         gq_ref, gk_ref, gmat_ref, _, _,
         c_ref, q_ref, k_ref, v_ref, nconv_ref, uext_ref, cs_ref, act_ref) = refs
        if first_layer:
            for stack_ref in (k_ref, v_ref):
                if stack_ref.shape[0] > 1:
                    stack_ref[1:] = jnp.zeros((stack_ref.shape[0] - 1,) + stack_ref.shape[1:], _F32)
            k_ref, v_ref = k_ref.at[0], v_ref.at[0]
    else:
        (x_ref, hist_ref, gmix_ref, win_ref, wdw_ref, bdw_ref, gcn_ref, bcn_ref, wpw_ref,
         gq_ref, gk_ref, gmat_ref,
         c_ref, q_ref, k_ref, v_ref, nconv_ref, uext_ref, cs_ref, act_ref) = refs
    t = pl.program_id(1)
    n_t = pl.num_programs(1)
    rows = n_seg * seg_rows
    d_model = x_ref.shape[-1]

    x = x_ref[...].reshape(rows, d_model)
    h = x * lax.rsqrt(jnp.mean(x * x, axis=-1, keepdims=True) + EPS) * gmix_ref[...]
    h = h.astype(_BF16)

    a = _dot(h, win_ref[:, 0:CONV_CH])
    gate = _dot(h, win_ref[:, CONV_CH:2 * CONV_CH])
    u = a * jax.nn.sigmoid(gate)

    def head_norm(z, g_ref):
        ssq = _dot_split2(z * z, gmat_ref[...])
        return z * lax.rsqrt(ssq * (1.0 / HEAD_DIM) + EPS) * g_ref[...]

    q = _dot(h, win_ref[:, 2 * CONV_CH:2 * CONV_CH + ATTN_W])
    ssq_q = _dot((q * q).astype(_BF16), gmat_ref[...])
    q = q * lax.rsqrt(ssq_q * (1.0 / HEAD_DIM) + EPS) * gq_ref[...]
    q_ref[...] = (q * (HEAD_DIM ** -0.5)).astype(_BF16).reshape(q_ref.shape)
    if feature_major_kv:
        k_t = _dot_tn(win_ref[:, 2 * CONV_CH + ATTN_W:2 * CONV_CH + 2 * ATTN_W], h)
        gain = jnp.concatenate([gk_ref[...]] * (rows // LANES), axis=1)
        for hh in range(N_HEADS):
            rs = slice(hh * HEAD_DIM, (hh + 1) * HEAD_DIM)
            kh = k_t[rs]
            ssq = jnp.sum(kh * kh, axis=0, keepdims=True)
            k_ref[rs, :] = kh * lax.rsqrt(ssq * (1.0 / HEAD_DIM) + EPS) * gain[rs]
        v_ref[...] = _dot_tn(win_ref[:, 2 * CONV_CH + 2 * ATTN_W:], h)
    else:
        k = _dot(h, win_ref[:, 2 * CONV_CH + ATTN_W:2 * CONV_CH + 2 * ATTN_W])
        k_ref[...] = head_norm(k, gk_ref).reshape(k_ref.shape)
        v_ref[...] = _dot(h, win_ref[:, 2 * CONV_CH + 2 * ATTN_W:]).reshape(v_ref.shape)

    lane_sl = [slice(g * LANES, (g + 1) * LANES) for g in range(CONV_CH // LANES)]
    u3 = u.reshape(n_seg, seg_rows, CONV_CH)

    @pl.when(t == 0)
    def _():
        for g, ls in enumerate(lane_sl):
            uext_ref[:, g, 0:HIST_PAD, :] = hist_ref[:, :, ls]

    for g, ls in enumerate(lane_sl):
        uext_ref[:, g, HIST_PAD:HIST_PAD + seg_rows, :] = u3[:, :, ls]

    half = CONV_ROWS // 2
    for s in range(n_seg):
        for r in range(seg_rows // CONV_ROWS):
            r0 = r * CONV_ROWS
            acc_even = jnp.zeros((half // SUBLANES, SUBLANES, CONV_CH), _F32)
            acc_odd = jnp.zeros((half // SUBLANES, SUBLANES, CONV_CH), _F32)
            for m in range(CONV_WIDTH + 1):
                start = r0 + m + (HIST_PAD - HIST)
                window = jnp.concatenate(
                    [uext_ref[s, g, pl.ds(start, half, stride=2), :] for g in range(len(lane_sl))], axis=1)
                window = window.reshape(half // SUBLANES, SUBLANES, CONV_CH)
                if m < CONV_WIDTH:
                    acc_even = acc_even + wdw_ref[m] * window
                if m >= 1:
                    acc_odd = acc_odd + wdw_ref[m - 1] * window
            acc_even = acc_even.reshape(half, CONV_CH)
            acc_odd = acc_odd.reshape(half, CONV_CH)
            for par, acc in enumerate((acc_even, acc_odd)):
                cv = acc + bdw_ref[...]
                mu = jnp.mean(cv, axis=-1, keepdims=True)
                xc = cv - mu
                y = xc * lax.rsqrt(jnp.mean(xc * xc, axis=-1, keepdims=True) + EPS)
                y = y * gcn_ref[...] + bcn_ref[...]
                y = y * jax.nn.sigmoid(y)
                base = s * seg_rows + r0 + par * half
                act_ref[base:base + half, :] = y.astype(_BF16)

    c_perm = _dot(act_ref[...], wpw_ref[...])
    for s in range(n_seg):
        for r in range(seg_rows // CONV_ROWS):
            r0 = r * CONV_ROWS
            base = s * seg_rows + r0
            for par in range(2):
                rows_par = c_perm[base + par * half:base + (par + 1) * half]
                for g, ls in enumerate(lane_sl):
                    cs_ref[s, g, pl.ds(r0 + par, half, stride=2), :] = rows_par[:, ls]
    for g, ls in enumerate(lane_sl):
        c_ref[:, :, ls] = cs_ref[:, g, :, :].astype(_BF16)

    @pl.when(t == n_t - 1)
    def _():
        for g, ls in enumerate(lane_sl):
            nconv_ref[:, :, ls] = uext_ref[:, g, seg_rows + (HIST_PAD - HIST):seg_rows + HIST_PAD, :]

    uext_ref[:, :, 0:HIST_PAD, :] = uext_ref[:, :, seg_rows:seg_rows + HIST_PAD, :]


def _inproj(x, hist, hist_layer, w, *, layer, n_seg, seg_rows, kv_stack=None):
    batch, t_len, d_model = x.shape
    depth = w["win"].shape[0]
    assert batch % n_seg == 0 and t_len % seg_rows == 0 and seg_rows % CONV_ROWS == 0
    n_t = t_len // seg_rows
    assert n_seg == 1 or n_t == 1
    feature_major_kv = kv_stack is not None
    rows_spec = lambda width: pl.BlockSpec((n_seg, seg_rows, width), lambda b, t: (b, t, 0))
    shared = lambda arr: pl.BlockSpec(arr.shape, lambda b, t: (0,) * arr.ndim)
    const = lambda arr: pl.BlockSpec((None,) + arr.shape[1:], lambda b, t: (layer,) + (0,) * (arr.ndim - 1),
                                     pipeline_mode=pl.Buffered(1))
    hist_spec = pl.BlockSpec((None, n_seg, HIST_PAD, CONV_CH), lambda b, t: (hist_layer, b, 0, 0))
    nconv_spec = pl.BlockSpec((n_seg, HIST, CONV_CH), lambda b, t: (b, 0, 0))
    small = [w["wdw"], w["bdw"], w["gcn"], w["bcn"], w["wpw"]]
    aliases = {}
    if feature_major_kv:
        assert n_seg == 1 and seg_rows % LANES == 0
        args = [x, hist, w["gmix"], w["win"]] + small + [w["gq"], w["gk_col"], w["gmat"]]
        in_specs = [rows_spec(d_model), hist_spec] + [const(a) for a in args[2:-1]] + [shared(w["gmat"])]
        kv_shape = jax.ShapeDtypeStruct((depth, batch, ATTN_W, t_len), _F32)
        first_layer = kv_stack[0] is None
        if first_layer:
            kv_spec = pl.BlockSpec((depth, None, ATTN_W, seg_rows), lambda b, t: (0, b, 0, t))
            dummy = jnp.zeros((SUBLANES, LANES), _F32)
            args += [dummy, dummy]
            in_specs += [shared(dummy)] * 2
        else:
            kv_spec = pl.BlockSpec((None, None, ATTN_W, seg_rows), lambda b, t: (layer, b, 0, t))
            aliases = {len(args): 2, len(args) + 1: 3}
            args += list(kv_stack)
            in_specs += [pl.BlockSpec(memory_space=pl.ANY)] * 2
    else:
        args = [x, hist, w["gmix"], w["win"]] + small + [w["gq"], w["gk"], w["gmat"]]
        in_specs = [rows_spec(d_model), hist_spec] + [const(a) for a in args[2:-1]] + [shared(w["gmat"])]
        kv_shape = jax.ShapeDtypeStruct((batch, t_len, ATTN_W), _F32)
        kv_spec = rows_spec(ATTN_W)
        first_layer = False
    out_shape = (
        jax.ShapeDtypeStruct((batch, t_len, CONV_CH), _BF16),
        jax.ShapeDtypeStruct((batch, t_len, ATTN_W), _BF16),
        kv_shape, kv_shape,
        jax.ShapeDtypeStruct((batch, HIST, CONV_CH), _F32),
    )
    return pl.pallas_call(
        functools.partial(_inproj_kernel, n_seg=n_seg, seg_rows=seg_rows, feature_major_kv=feature_major_kv,
                          first_layer=first_layer),
        out_shape=out_shape,
        grid=(batch // n_seg, n_t),
        in_specs=in_specs,
        out_specs=(rows_spec(CONV_CH), rows_spec(ATTN_W), kv_spec, kv_spec, nconv_spec),
        scratch_shapes=[pltpu.VMEM((n_seg, CONV_CH // LANES, seg_rows + HIST_PAD, LANES), _F32),
                        pltpu.VMEM((n_seg, CONV_CH // LANES, seg_rows, LANES), _F32),
                        pltpu.VMEM((n_seg * seg_rows, CONV_CH), _BF16)],
        input_output_aliases=aliases,
        compiler_params=pltpu.CompilerParams(dimension_semantics=("arbitrary", "arbitrary"),
                                             vmem_limit_bytes=VMEM_LIMIT_BYTES),
        name="inproj_conv",
    )(*args)


def _attn_kernel(*refs, n_sub, past_chunks0, chunks_per_step, layer, own_feature_major):
    n_fm = n_sub + 2 if own_feature_major else 2
    q_ref, k_fm, v_fm, rest = refs[0], refs[1:1 + n_fm], refs[1 + n_fm:1 + 2 * n_fm], refs[1 + 2 * n_fm:]
    if not own_feature_major:
        kd_ref, vd_ref, rest = rest[0], rest[1], rest[2:]
    kany_ref, vany_ref, ud_ref, up_ref, o_ref, oacc_ref, run_ref, kbuf_ref, vbuf_ref, sem_ref = rest
    b = pl.program_id(0)
    i = pl.program_id(1)
    tq = q_ref.shape[0] // n_sub
    n_pairs = ATTN_W // LANES
    subs = range(n_sub)
    n_past = [past_chunks0 + i * chunks_per_step + j for j in subs]
    even_head = lax.broadcasted_iota(jnp.int32, (tq, LANES), 1) < HEAD_DIM
    pair_sl = [slice(p * LANES, (p + 1) * LANES) for p in range(n_pairs)]

    def stacked_queries(j):
        qs = []
        for sl in pair_sl:
            qp = q_ref[j * tq:(j + 1) * tq, sl]
            qs.append(jnp.concatenate([jnp.where(even_head, qp, 0.0), jnp.where(even_head, 0.0, qp)], axis=0))
        return qs

    def scores_fm(qs, key_refs):
        out = []
        for p, sl in enumerate(pair_sl):
            keys = jnp.concatenate([r[sl, :] for r in key_refs], axis=1).astype(_BF16)
            out.append(_dot(qs[p], keys))
        return jnp.concatenate(out, axis=0)

    def log_terms(z):
        sp = jnp.log(1.0 + jnp.exp(-jnp.abs(z)))
        neg_part = jnp.minimum(z, 0.0)
        return neg_part - sp, (neg_part - z) - sp

    def suffix_sums(log_1m, u_ref):
        hi = log_1m.astype(_BF16)
        lo = (log_1m - hi.astype(_F32)).astype(_BF16)
        r = _dot(jnp.concatenate([hi, lo], axis=1), u_ref[...])
        return r[:, LANES:], r[:, :LANES]

    def weighted_values(fm_weights, fm_value_refs, rm_weight=None, rm_value_ref=None):
        wb = jnp.concatenate(fm_weights, axis=1).astype(_BF16)
        outs = []
        for p, sl in enumerate(pair_sl):
            rows = slice(2 * tq * p, 2 * tq * (p + 1))
            vals = jnp.concatenate([r[sl, :] for r in fm_value_refs], axis=1).astype(_BF16)
            pv = _dot_nt(wb[rows], vals)
            if rm_weight is not None:
                pv = pv + _dot(rm_weight[rows].astype(_BF16), rm_value_ref[:, sl].astype(_BF16))
            outs.append(jnp.where(even_head, pv[:tq], pv[tq:]))
        return jnp.concatenate(outs, axis=1)

    qs = [stacked_queries(j) for j in subs]
    if own_feature_major:
        z = [scores_fm(qs[j], k_fm[j:j + 3]) for j in subs]
        z_own = [zj[:, 2 * KEY_CHUNK:] for zj in z]
    else:
        z = [scores_fm(qs[0], k_fm)]
        z_own = [jnp.concatenate([_dot_nt(qs[0][p], kd_ref[:, sl].astype(_BF16))
                                  for p, sl in enumerate(pair_sl)], axis=0)]
    q_row = lax.broadcasted_iota(jnp.int32, (N_HEADS * tq, tq), 0) & (tq - 1)
    causal = lax.broadcasted_iota(jnp.int32, (N_HEADS * tq, tq), 1) < q_row
    def past_scores(j, back):
        zc = z[j][:, (2 - back) * KEY_CHUNK:(3 - back) * KEY_CHUNK]
        if past_chunks0 + j >= back:
            return zc
        return zc + jnp.where(n_past[j] >= back, 0.0, MASKED_SCORE)

    z_2 = [past_scores(j, 2) for j in subs]
    z_1 = [past_scores(j, 1) for j in subs]
    z_d = [jnp.where(causal, z_own[j], MASKED_SCORE) for j in subs]
    t_d = [log_terms(zz) for zz in z_d]
    t_1 = [log_terms(zz) for zz in z_1]
    t_2 = [log_terms(zz) for zz in z_2]
    s_d = [suffix_sums(t_d[j][1], ud_ref) for j in subs]
    s_1 = [suffix_sums(t_1[j][1], up_ref) for j in subs]
    s_2 = [suffix_sums(t_2[j][1], up_ref) for j in subs]
    a_d = [jnp.exp(t_d[j][0] + s_d[j][0]) for j in subs]
    a_1 = [jnp.exp(t_1[j][0] + (s_1[j][0] + s_d[j][1])) for j in subs]
    run_1 = [s_d[j][1] + s_1[j][1] for j in subs]
    a_2 = [jnp.exp(t_2[j][0] + (s_2[j][0] + run_1[j])) for j in subs]
    run = [run_1[j] + s_2[j][1] for j in subs]
    for j in subs:
        if own_feature_major:
            out = weighted_values((a_2[j], a_1[j], a_d[j]), v_fm[j:j + 3])
        else:
            out = weighted_values((a_2[j], a_1[j]), v_fm, a_d[j], vd_ref)
        oacc_ref[j * tq:(j + 1) * tq, :] = out
        run_ref[j] = run[j]

    def more(carry):
        c, m = carry
        return jnp.logical_and(c >= 0, m > STICK_LOG_FLOOR)

    def older(j, carry):
        c, _ = carry
        start = pl.multiple_of(c * KEY_CHUNK, KEY_CHUNK)
        k_copy = pltpu.make_async_copy(kany_ref.at[layer, b, :, pl.ds(start, KEY_CHUNK)], kbuf_ref, sem_ref.at[0])
        v_copy = pltpu.make_async_copy(vany_ref.at[layer, b, :, pl.ds(start, KEY_CHUNK)], vbuf_ref, sem_ref.at[1])
        k_copy.start()
        v_copy.start()
        k_copy.wait()
        v_copy.wait()
        lb, l1 = log_terms(scores_fm(stacked_queries(j), (kbuf_ref,)))
        inside, tot = suffix_sums(l1, up_ref)
        run_c = run_ref[j]
        oacc_ref[j * tq:(j + 1) * tq, :] += weighted_values((jnp.exp(lb + (inside + run_c)),), (vbuf_ref,))
        run_c = run_c + tot
        run_ref[j] = run_c
        return c - 1, jnp.max(run_c)

    for j in subs:
        lax.while_loop(more, functools.partial(older, j), (n_past[j] - 3, jnp.max(run[j])))

    o_ref[...] = oacc_ref[...].astype(o_ref.dtype)


def _attention(q, k_own, v_own, k_past, v_past, *, layer, past_is_own):
    batch, t_len, _ = q.shape
    if past_is_own:
        tq = KEY_CHUNK
        n_sub = ATTN_BLOCKS_PER_STEP if t_len % (ATTN_BLOCKS_PER_STEP * tq) == 0 else 1
        past_chunks0, chunks_per_step, n_fm = 0, n_sub, n_sub + 2
    else:
        tq, n_sub = t_len, 1
        assert k_past.shape[-1] % KEY_CHUNK == 0
        past_chunks0, chunks_per_step, n_fm = k_past.shape[-1] // KEY_CHUNK, 0, 2
    assert t_len % (n_sub * tq) == 0 and tq <= KEY_CHUNK and tq & (tq - 1) == 0
    n_steps = t_len // (n_sub * tq)
    assert past_is_own or n_steps == 1

    def newer(n):
        u = lax.broadcasted_iota(jnp.int32, (n, n), 0) > lax.broadcasted_iota(jnp.int32, (n, n), 1)
        once = jnp.concatenate([jnp.ones((n, LANES), _BF16), u.astype(_BF16)], axis=1)
        return jnp.concatenate([once, once], axis=0)
    u_diag, u_past = newer(tq), newer(KEY_CHUNK)

    def fm_chunk(m):
        imap = lambda b, i: (layer, b, 0, jnp.maximum(past_chunks0 + i * chunks_per_step + m - 2, 0))
        return pl.BlockSpec((None, None, ATTN_W, KEY_CHUNK), imap)

    rm_block = pl.BlockSpec((None, n_sub * tq, ATTN_W), lambda b, i: (b, i, 0))
    const = lambda arr: pl.BlockSpec(arr.shape, lambda b, i: (0,) * arr.ndim)
    any_spec = pl.BlockSpec(memory_space=pl.ANY)
    fm_specs = [fm_chunk(m) for m in range(n_fm)]
    args = [q] + [k_past] * n_fm + [v_past] * n_fm
    in_specs = [rm_block] + fm_specs + fm_specs
    if not past_is_own:
        args += [k_own, v_own]
        in_specs += [rm_block, rm_block]
    args += [k_past, v_past, u_diag, u_past]
    in_specs += [any_spec, any_spec, const(u_diag), const(u_past)]
    return pl.pallas_call(
        functools.partial(_attn_kernel, n_sub=n_sub, past_chunks0=past_chunks0, chunks_per_step=chunks_per_step,
                          layer=layer, own_feature_major=past_is_own),
        out_shape=jax.ShapeDtypeStruct((batch, t_len, ATTN_W), _BF16),
        grid=(batch, n_steps),
        in_specs=in_specs,
        out_specs=rm_block,
        scratch_shapes=[pltpu.VMEM((n_sub * tq, ATTN_W), _F32),
                        pltpu.VMEM((n_sub, N_HEADS * tq, LANES), _F32),
                        pltpu.VMEM((ATTN_W, KEY_CHUNK), _F32),
                        pltpu.VMEM((ATTN_W, KEY_CHUNK), _F32),
                        pltpu.SemaphoreType.DMA((2,))],
        compiler_params=pltpu.CompilerParams(dimension_semantics=("arbitrary", "arbitrary"),
                                             vmem_limit_bytes=VMEM_LIMIT_BYTES),
        name="stick_attention",
    )(*args)


def _outffn_kernel(xa_ref, ca_ref, oa_ref, xb_ref, cb_ref, ob_ref, wout_any, gff_ref, w1_any, w2_any,
                   ya_ref, yb_ref, wout_ref, w1_ref, w2_ref, stage_ref, sem_ref, *, n_first, layer):
    d_ff = w1_ref.shape[1]
    r = pl.program_id(0)

    @pl.when(r == 0)
    def _():
        slots, rows, cols = stage_ref.shape
        tiles = [(src, dst, i, j)
                 for src, dst in ((wout_any, wout_ref), (w1_any, w1_ref), (w2_any, w2_ref))
                 for i in range(dst.shape[0] // rows) for j in range(dst.shape[1] // cols)]

        def tile_copy(n):
            src, _, i, j = tiles[n]
            return pltpu.make_async_copy(src.at[layer, pl.ds(i * rows, rows), pl.ds(j * cols, cols)],
                                         stage_ref.at[n % slots], sem_ref.at[n % slots])
        for n in range(min(slots - 1, len(tiles))):
            tile_copy(n).start(priority=n % 2)
        for n, (_, dst, i, j) in enumerate(tiles):
            if n + slots - 1 < len(tiles):
                tile_copy(n + slots - 1).start(priority=(n + slots - 1) % 2)
            tile_copy(n).wait()
            dst[i * rows:(i + 1) * rows, j * cols:(j + 1) * cols] = stage_ref[n % slots].astype(_BF16)

    def mix_and_ffn(x_ref, c_ref, o_ref, y_ref):
        x1 = x_ref[...] + _dot(c_ref[...], wout_ref[0:CONV_CH, :]) + _dot(o_ref[...], wout_ref[CONV_CH:, :])
        h = x1 * lax.rsqrt(jnp.mean(x1 * x1, axis=-1, keepdims=True) + EPS) * gff_ref[...]
        h = h.astype(_BF16)
        ffn = None
        for cc in range(d_ff // FF_CHUNK):
            f = jnp.maximum(_dot(h, w1_ref[:, cc * FF_CHUNK:(cc + 1) * FF_CHUNK]), 0.0)
            part = _dot((f * f).astype(_BF16), w2_ref[cc * FF_CHUNK:(cc + 1) * FF_CHUNK, :])
            ffn = part if ffn is None else ffn + part
        y_ref[...] = x1 + ffn

    @pl.when(r < n_first)
    def _():
        mix_and_ffn(xa_ref, ca_ref, oa_ref, ya_ref)

    @pl.when(r >= n_first)
    def _():
        mix_and_ffn(xb_ref, cb_ref, ob_ref, yb_ref)


def _outffn(xa, ca, oa, xb, cb, ob, wout, gff, w1, w2, *, layer):
    d_model = xa.shape[1]
    d_ff = w1.shape[-1]
    tb = ROW_BLOCK
    assert xa.shape[0] % tb == 0 and xb.shape[0] % tb == 0 and d_ff % FF_CHUNK == 0
    assert d_model % WEIGHT_STAGE_ROWS == 0 and d_ff % WEIGHT_STAGE_ROWS == 0
    na, nb = xa.shape[0] // tb, xb.shape[0] // tb
    first = lambda width: pl.BlockSpec((tb, width), lambda r: (jnp.minimum(r, na - 1), 0))
    second = lambda width: pl.BlockSpec((tb, width), lambda r: (jnp.maximum(r - na, 0), 0))
    any_spec = pl.BlockSpec(memory_space=pl.ANY)
    gff_spec = pl.BlockSpec((None,) + gff.shape[1:], lambda r: (layer, 0, 0))
    return pl.pallas_call(
        functools.partial(_outffn_kernel, n_first=na, layer=layer),
        out_shape=(jax.ShapeDtypeStruct(xa.shape, _F32), jax.ShapeDtypeStruct(xb.shape, _F32)),
        grid=(na + nb,),
        in_specs=[first(d_model), first(CONV_CH), first(ATTN_W), second(d_model), second(CONV_CH), second(ATTN_W),
                  any_spec, gff_spec, any_spec, any_spec],
        out_specs=(first(d_model), second(d_model)),
        scratch_shapes=[pltpu.VMEM(wout.shape[1:], _BF16), pltpu.VMEM(w1.shape[1:], _BF16),
                        pltpu.VMEM(w2.shape[1:], _BF16),
                        pltpu.VMEM((WEIGHT_STAGE_SLOTS, WEIGHT_STAGE_ROWS, d_model), _F32),
                        pltpu.SemaphoreType.DMA((WEIGHT_STAGE_SLOTS,))],
        compiler_params=pltpu.CompilerParams(dimension_semantics=("arbitrary",),
                                             vmem_limit_bytes=VMEM_LIMIT_BYTES),
        name="outproj_ffn",
    )(xa, ca, oa, xb, cb, ob, wout, gff, w1, w2)


def kernel(x_prompt, x_sample, cache_k, cache_v, state_conv, g_mix, w_in, w_dw, b_dw, g_cn, b_cn, w_pw,
           g_q, g_k, w_out, g_ff, w_ff1, w_ff2):
    depth = w_in.shape[0]
    bp, tp, _ = x_prompt.shape
    bs, ts, _ = x_sample.shape
    n_past = cache_k.shape[2]
    head_id = lax.broadcasted_iota(jnp.int32, (ATTN_W, ATTN_W), 0) // HEAD_DIM
    gmat = (head_id == head_id.T).astype(_BF16)
    zero_hist = jnp.zeros((1, bp, HIST_PAD, CONV_CH), _F32)
    sample_hist = jnp.pad(state_conv, ((0, 0), (0, 0), (HIST_PAD - HIST, 0), (0, 0)))
    row = lambda a: a.reshape(depth, 1, -1)
    per_head = lambda g: jnp.tile(g, (1, N_HEADS))
    win = w_in.astype(_BF16)
    w = dict(gmix=row(g_mix), win=win,
             wdw=jnp.broadcast_to(w_dw[:, :, None, :], w_dw.shape[:2] + (SUBLANES, CONV_CH)), bdw=row(b_dw), gcn=row(g_cn), bcn=row(b_cn), wpw=w_pw.astype(_BF16),
             gq=row(per_head(g_q)), gk=row(per_head(g_k)),
             gk_col=jnp.broadcast_to(per_head(g_k)[:, :, None], (depth, ATTN_W, LANES)), gmat=gmat,
             wout=w_out, gff=row(g_ff), w1=w_ff1, w2=w_ff2)
    to_fm = lambda a: jnp.transpose(a, (0, 1, 3, 4, 2)).reshape(a.shape[0], a.shape[1], ATTN_W, a.shape[2])
    from_fm = lambda a: jnp.transpose(a.reshape(a.shape[0], a.shape[1], N_HEADS, HEAD_DIM, a.shape[3]),
                                      (0, 1, 4, 2, 3))
    cache_k_fm, cache_v_fm = to_fm(cache_k), to_fm(cache_v)

    hp, hs = x_prompt, x_sample
    kv_stack = (None, None)
    conv_p, k_s, v_s, conv_s = [], [], [], []
    flat = lambda a: a.reshape(-1, a.shape[-1])
    for l in range(depth):
        cp, q, k_fm, v_fm, cv = _inproj(hp, zero_hist, 0, w, layer=l, n_seg=1, seg_rows=min(INPROJ_ROW_BLOCK, tp),
                                        kv_stack=kv_stack)
        kv_stack = (k_fm, v_fm)
        op = _attention(q, None, None, k_fm, v_fm, layer=l, past_is_own=True)
        conv_p.append(cv)
        cs, q, k, v, cv = _inproj(hs, sample_hist, l, w, layer=l, n_seg=bs, seg_rows=ts)
        os_ = _attention(q, k, v, cache_k_fm, cache_v_fm, layer=l, past_is_own=False)
        k_s.append(k.reshape(bs, ts, N_HEADS, HEAD_DIM))
        v_s.append(v.reshape(bs, ts, N_HEADS, HEAD_DIM))
        conv_s.append(cv)
        yp, ys = _outffn(flat(hp), flat(cp), flat(op), flat(hs), flat(cs), flat(os_),
                         w["wout"], w["gff"], w["w1"], w["w2"], layer=l)
        hp, hs = yp.reshape(hp.shape), ys.reshape(hs.shape)
    return (hp, hs, from_fm(kv_stack[0]), from_fm(kv_stack[1]), jnp.stack(conv_p),
            jnp.stack(k_s), jnp.stack(v_s), jnp.stack(conv_s))
```
